```python
import math
import jax, jax.numpy as jnp
from jax import lax
import numpy as np


D_MODEL = 2048
BATCH = 4
SEQ = 8192
DEPTH = 1
DEC_BATCH = 8
DEC_SEQ = 64
PAST_LEN = 2048

CHUNK = 64
EPS = 1e-6
NEG_INF = -1e30

HG_HEADS = 8
HG_DK = 128
HG_DV = 128
HG_WIDTH_K = HG_HEADS * HG_DK
HG_WIDTH = HG_HEADS * HG_DV
HG_BLOCK = 16

SWA_HEADS = 16
SWA_KV_HEADS = 4
SWA_GROUP = SWA_HEADS // SWA_KV_HEADS
SWA_HEAD_DIM = 64
SWA_WIDTH = SWA_HEADS * SWA_HEAD_DIM
SWA_KV_WIDTH = SWA_KV_HEADS * SWA_HEAD_DIM
SWA_SCALE = SWA_HEAD_DIM ** -0.5
WINDOW = 128
WIN_CHUNKS = WINDOW // CHUNK

MIX_WIDTH = HG_WIDTH + SWA_WIDTH
IN_WIDTHS = (HG_WIDTH_K, HG_WIDTH_K, HG_WIDTH, HG_WIDTH, SWA_WIDTH, SWA_KV_WIDTH, SWA_KV_WIDTH)
IN_WIDTH = HG_WIDTH_K * 2 + HG_WIDTH * 2 + SWA_WIDTH + SWA_KV_WIDTH * 2
IN_SPLITS = (HG_WIDTH_K,
             2 * HG_WIDTH_K,
             2 * HG_WIDTH_K + HG_WIDTH,
             2 * HG_WIDTH_K + 2 * HG_WIDTH,
             2 * HG_WIDTH_K + 2 * HG_WIDTH + SWA_WIDTH,
             2 * HG_WIDTH_K + 2 * HG_WIDTH + SWA_WIDTH + SWA_KV_WIDTH)

REL_BUCKETS = 32
REL_MAX_DIST = 128

N_MEM = 256
MEM_HEADS = 4
MEM_HEAD_DIM = 128
MEM_INNER = MEM_HEADS * MEM_HEAD_DIM
MEM_SCALE = MEM_HEAD_DIM ** -0.5

PEER_HEADS = 8
PEER_NKEYS = 128
PEER_EXPERTS = PEER_NKEYS * PEER_NKEYS
PEER_DKEY = 256
PEER_DHALF = PEER_DKEY // 2
PEER_TOPK = 16
PEER_BLOCK = 128

kernel_name = "hybrid_streaming_encoder_step"


def rmsnorm(x, w):
    x32 = x.astype(jnp.float32)
    y = x32 * lax.rsqrt(jnp.mean(x32 * x32, axis=-1, keepdims=True) + EPS)
    return (y * w.astype(jnp.float32)).astype(x.dtype)


def t5_bias(rel, table):
    nb = REL_BUCKETS // 2
    max_exact = nb // 2
    side = jnp.where(rel > 0, nb, 0)
    n = jnp.abs(rel)
    n_f = jnp.maximum(n, max_exact).astype(jnp.float32)
    large = max_exact + (jnp.log(n_f / max_exact) / math.log(REL_MAX_DIST / max_exact)
                         * (nb - max_exact)).astype(jnp.int32)
    large = jnp.minimum(large, nb - 1)
    bucket = side + jnp.where(n < max_exact, n, large)
    return jnp.transpose(table[bucket].astype(jnp.float32), (2, 0, 1))


def sink_softmax(logits, sink):
    sink = sink.astype(jnp.float32)
    m = jnp.maximum(jnp.max(logits, axis=-1, keepdims=True), sink)
    e = jnp.exp(logits - m)
    return e / (jnp.sum(e, axis=-1, keepdims=True) + jnp.exp(sink - m))


def hgrn_recurrence(q, k, v, log_f, s0):
    B, T = q.shape[:2]
    pad = (-T) % HG_BLOCK
    padw = ((0, 0), (0, pad), (0, 0), (0, 0))
    q, k, v, log_f = [jnp.pad(a, padw) for a in (q, k, v, log_f)]
    nb = (T + pad) // HG_BLOCK

    def blocks(a):
        return a.reshape(B, nb, HG_BLOCK, HG_HEADS, a.shape[-1]).transpose(1, 0, 3, 2, 4)

    causal = jnp.tril(jnp.ones((HG_BLOCK, HG_BLOCK), jnp.float32))

    def step(S, blk):
        qb, kb, vb, gb = blk
        b = jnp.cumsum(gb, axis=2)
        b_last = b[:, :, -1:, :]
        q_in = qb * jnp.exp(b)
        k_in = kb * jnp.exp(-b)
        attn = jnp.einsum('bhtd,bhsd->bhts', q_in, k_in) * causal
        o = (jnp.einsum('bhtd,bhdv->bhtv', q_in, S)
             + jnp.einsum('bhts,bhsv->bhtv', attn, vb))
        k_out = kb * jnp.exp(b_last - b)
        S_new = (S * jnp.exp(b_last[:, :, 0, :])[..., None]
                 + jnp.einsum('bhsd,bhsv->bhdv', k_out, vb))
        return S_new, o

    S, o = lax.scan(step, s0, (blocks(q), blocks(k), blocks(v), blocks(log_f)))
    o = o.transpose(1, 0, 3, 2, 4).reshape(B, nb * HG_BLOCK, HG_HEADS, HG_DV)[:, :T]
    return o, S


def hgrn_mix(q, f_pre, i, g, lb, norm_w, s0):
    B, T, _ = q.shape
    f = lb + (1.0 - lb) * jax.nn.sigmoid(f_pre.astype(jnp.float32))

    def heads(a, d):
        return a.astype(jnp.float32).reshape(B, T, HG_HEADS, d)

    o, s_fin = hgrn_recurrence(heads(q, HG_DK), heads(1.0 - f, HG_DK), heads(i, HG_DV),
                               heads(jnp.log(f), HG_DK), s0.astype(jnp.float32))
    o = rmsnorm(o, norm_w).reshape(B, T, HG_WIDTH)
    out = o * jax.nn.silu(g.astype(jnp.float32))
    return out.astype(q.dtype), s_fin.astype(s0.dtype)


def swa_banded(q, k, v, sinks, rel_bias):
    B, T = q.shape[:2]
    nc = T // CHUNK
    n_keys = (WIN_CHUNKS + 1) * CHUNK
    qc = q.reshape(B, nc, CHUNK, SWA_KV_HEADS, SWA_GROUP, SWA_HEAD_DIM)
    padw = ((0, 0), (WIN_CHUNKS * CHUNK, 0), (0, 0), (0, 0))

    def band(a):
        a = jnp.pad(a, padw).reshape(B, nc + WIN_CHUNKS, CHUNK, SWA_KV_HEADS, SWA_HEAD_DIM)
        return jnp.concatenate([a[:, j:j + nc] for j in range(WIN_CHUNKS + 1)], axis=2)

    kb, vb = band(k), band(v)
    key_pos = jnp.arange(n_keys)
    q_pos = WIN_CHUNKS * CHUNK + jnp.arange(CHUNK)
    bias = t5_bias(key_pos[None, :] - q_pos[:, None], rel_bias)
    bias = bias.reshape(SWA_KV_HEADS, SWA_GROUP, CHUNK, n_keys)
    valid = (jnp.arange(nc)[:, None] - WIN_CHUNKS + key_pos[None, :] // CHUNK) >= 0
    logits = jnp.einsum('bnqhgd,bnshd->bnhgqs', qc, kb).astype(jnp.float32) * SWA_SCALE + bias
    logits = jnp.where(valid[None, :, None, None, None, :], logits, NEG_INF)
    p = sink_softmax(logits, sinks.reshape(SWA_KV_HEADS, SWA_GROUP, 1, 1))
    o = jnp.einsum('bnhgqs,bnshd->bnqhgd', p.astype(v.dtype), vb)
    return o.reshape(B, T, SWA_WIDTH)


def swa_cached(q, k, v, k_cache, v_cache, sinks, rel_bias):
    B, T = q.shape[:2]
    n_past = k_cache.shape[1]
    kk = jnp.concatenate([k_cache.astype(k.dtype), k], axis=1)
    vv = jnp.concatenate([v_cache.astype(v.dtype), v], axis=1)
    q_pos = PAST_LEN + jnp.arange(T)
    key_pos = jnp.concatenate([PAST_LEN - n_past + jnp.arange(n_past), PAST_LEN + jnp.arange(T)])
    bias = t5_bias(key_pos[None, :] - q_pos[:, None], rel_bias)
    bias = bias.reshape(SWA_KV_HEADS, SWA_GROUP, T, n_past + T)
    logits = jnp.einsum('bqhgd,bshd->bhgqs', q, kk).astype(jnp.float32) * SWA_SCALE + bias
    p = sink_softmax(logits, sinks.reshape(SWA_KV_HEADS, SWA_GROUP, 1, 1))
    o = jnp.einsum('bhgqs,bshd->bqhgd', p.astype(vv.dtype), vv)
    return o.reshape(B, T, SWA_WIDTH)


def memory_kv(mem, norm_w, wk, wv):
    B = mem.shape[0]
    m = rmsnorm(mem, norm_w)
    mk = (m @ wk).reshape(B, N_MEM, MEM_HEADS, MEM_HEAD_DIM)
    mv = (m @ wv).reshape(B, N_MEM, MEM_HEADS, MEM_HEAD_DIM)
    return mk, mv


def memory_cross_attn(h, mk, mv, wq, wo):
    B, T, _ = h.shape
    q = (h @ wq).reshape(B, T, MEM_HEADS, MEM_HEAD_DIM)
    logits = jnp.einsum('bthd,bnhd->bhtn', q, mk.astype(q.dtype)).astype(jnp.float32) * MEM_SCALE
    p = jax.nn.softmax(logits, axis=-1)
    o = jnp.einsum('bhtn,bnhd->bthd', p.astype(h.dtype), mv.astype(h.dtype))
    return o.reshape(B, T, MEM_INNER) @ wo


def peer_ffn(h, wq, subkeys, u_tab, v_tab):
    B, T, D = h.shape
    n = B * T
    pad = (-n) % PEER_BLOCK
    xs = jnp.pad(h.reshape(n, D), ((0, pad), (0, 0))).reshape(-1, PEER_BLOCK, D)

    def block(xb):
        q = (xb @ wq).reshape(PEER_BLOCK, PEER_HEADS, 2, PEER_DHALF)
        s = jnp.einsum('nhcd,chkd->nhck', q, subkeys.astype(q.dtype)).astype(jnp.float32)
        s1, i1 = lax.top_k(s[:, :, 0], PEER_TOPK)
        s2, i2 = lax.top_k(s[:, :, 1], PEER_TOPK)
        cand = (s1[..., :, None] + s2[..., None, :]).reshape(PEER_BLOCK, PEER_HEADS, PEER_TOPK * PEER_TOPK)
        cidx = (i1[..., :, None] * PEER_NKEYS + i2[..., None, :]).reshape(PEER_BLOCK, PEER_HEADS, PEER_TOPK * PEER_TOPK)
        top_s, pos = lax.top_k(cand, PEER_TOPK)
        eidx = jnp.take_along_axis(cidx, pos, axis=-1)
        gate = jax.nn.softmax(top_s, axis=-1)
        ue = u_tab[eidx]
        ve = v_tab[eidx]
        act = jax.nn.gelu(jnp.einsum('nd,nhkd->nhk', xb, ue).astype(jnp.float32), approximate=False)
        return jnp.einsum('nhk,nhkd->nd', (gate * act).astype(xb.dtype), ve)

    y = lax.map(block, xs).reshape(-1, D)[:n]
    return y.reshape(B, T, D)


def trunk_layer(x, mem_k, mem_v, s0, swa_cache_k, swa_cache_v, lb, rel_bias,
                norm_mix_w, w_in, hg_norm_w, swa_sinks, w_out, norm_cross_w, mem_wq, mem_wo,
                norm_ffn_w, peer_wq, peer_subkeys, peer_u, peer_v):
    B, T, _ = x.shape
    h = rmsnorm(x, norm_mix_w)
    q_hg, f_hg, i_hg, g_hg, q_sw, k_sw, v_sw = jnp.split(h @ w_in, IN_SPLITS, axis=-1)
    o_hg, s_fin = hgrn_mix(q_hg, f_hg, i_hg, g_hg, lb, hg_norm_w, s0)
    q_sw = q_sw.reshape(B, T, SWA_KV_HEADS, SWA_GROUP, SWA_HEAD_DIM)
    k_sw = k_sw.reshape(B, T, SWA_KV_HEADS, SWA_HEAD_DIM)
    v_sw = v_sw.reshape(B, T, SWA_KV_HEADS, SWA_HEAD_DIM)
    if swa_cache_k is None:
        o_sw = swa_banded(q_sw, k_sw, v_sw, swa_sinks, rel_bias)
        keep = min(WINDOW, T)
        k_rows, v_rows = k_sw[:, T - keep:], v_sw[:, T - keep:]
    else:
        o_sw = swa_cached(q_sw, k_sw, v_sw, swa_cache_k, swa_cache_v, swa_sinks, rel_bias)
        k_rows, v_rows = k_sw, v_sw
    x = x + jnp.concatenate([o_hg, o_sw], axis=-1) @ w_out
    x = x + memory_cross_attn(rmsnorm(x, norm_cross_w), mem_k, mem_v, mem_wq, mem_wo)
    x = x + peer_ffn(rmsnorm(x, norm_ffn_w), peer_wq, peer_subkeys, peer_u, peer_v)
    return x, s_fin, k_rows, v_rows


def setup_inputs(seed: int = 0) -> dict:
    key = jax.random.key(seed)
    ks = iter(jax.random.split(key, 40))

    def nrm(shape, scale):
        return jax.random.normal(next(ks), shape, jnp.float32) * scale

    def gain(shape):
        return 1.0 + nrm(shape, 0.02)

    swa_keep = min(WINDOW, PAST_LEN)
    return {
        "x_prompt": nrm((BATCH, SEQ, D_MODEL), 1.0),
        "x_sample": nrm((DEC_BATCH, DEC_SEQ, D_MODEL), 1.0),
        "mem_prompt": nrm((BATCH, N_MEM, D_MODEL), 1.0),
        "state_hgrn": nrm((DEPTH, DEC_BATCH, HG_HEADS, HG_DK, HG_DV), 0.3),
        "cache_swa_k": nrm((DEPTH, DEC_BATCH, swa_keep, SWA_KV_HEADS, SWA_HEAD_DIM), 1.0),
        "cache_swa_v": nrm((DEPTH, DEC_BATCH, swa_keep, SWA_KV_HEADS, SWA_HEAD_DIM), 1.0),
        "cache_mem_k": nrm((DEPTH, DEC_BATCH, N_MEM, MEM_HEADS, MEM_HEAD_DIM), 1.0),
        "cache_mem_v": nrm((DEPTH, DEC_BATCH, N_MEM, MEM_HEADS, MEM_HEAD_DIM), 1.0),
        "rel_bias": nrm((REL_BUCKETS, SWA_HEADS), 0.2),
        "hg_lb_logits": nrm((DEPTH + 1, HG_WIDTH_K), 0.1),
        "norm_mix_w": gain((DEPTH, D_MODEL)),
        "w_in": nrm((DEPTH, D_MODEL, IN_WIDTH), D_MODEL ** -0.5),
        "hg_norm_w": gain((DEPTH, HG_DV)),
        "swa_sinks": nrm((DEPTH, SWA_HEADS), 0.5),
        "w_out": nrm((DEPTH, MIX_WIDTH, D_MODEL), MIX_WIDTH ** -0.5),
        "norm_mem_w": gain((DEPTH, D_MODEL)),
        "norm_cross_w": gain((DEPTH, D_MODEL)),
        "mem_wq": nrm((DEPTH, D_MODEL, MEM_INNER), D_MODEL ** -0.5),
        "mem_wk": nrm((DEPTH, D_MODEL, MEM_INNER), D_MODEL ** -0.5),
        "mem_wv": nrm((DEPTH, D_MODEL, MEM_INNER), D_MODEL ** -0.5),
        "mem_wo": nrm((DEPTH, MEM_INNER, D_MODEL), MEM_INNER ** -0.5),
        "norm_ffn_w": gain((DEPTH, D_MODEL)),
        "peer_wq": nrm((DEPTH, D_MODEL, PEER_HEADS * PEER_DKEY), D_MODEL ** -0.5),
        "peer_subkeys": nrm((DEPTH, 2, PEER_HEADS, PEER_NKEYS, PEER_DHALF), PEER_DHALF ** -0.5),
        "peer_u": nrm((DEPTH, PEER_EXPERTS, D_MODEL), D_MODEL ** -0.5),
        "peer_v": nrm((DEPTH, PEER_EXPERTS, D_MODEL), PEER_HEADS ** -0.5),
        "final_norm_w": gain((D_MODEL,)),
    }


def reference(x_prompt, x_sample, mem_prompt, state_hgrn, cache_swa_k, cache_swa_v, cache_mem_k,
              cache_mem_v, rel_bias, hg_lb_logits, norm_mix_w, w_in, hg_norm_w, swa_sinks, w_out,
              norm_mem_w, norm_cross_w, mem_wq, mem_wk, mem_wv, mem_wo, norm_ffn_w, peer_wq,
              peer_subkeys, peer_u, peer_v, final_norm_w):
    lb_all = jnp.cumsum(jax.nn.softmax(hg_lb_logits.astype(jnp.float32), axis=0), axis=0)
    xp, xs = x_prompt, x_sample
    hg_p, swk_p, swv_p, mk_p, mv_p = [], [], [], [], []
    hg_s, swk_s, swv_s = [], [], []
    for l in range(DEPTH):
        lw = (norm_mix_w[l], w_in[l], hg_norm_w[l], swa_sinks[l], w_out[l], norm_cross_w[l],
              mem_wq[l], mem_wo[l], norm_ffn_w[l], peer_wq[l], peer_subkeys[l], peer_u[l], peer_v[l])
        mk, mv = memory_kv(mem_prompt, norm_mem_w[l], mem_wk[l], mem_wv[l])
        s0 = jnp.zeros((xp.shape[0], HG_HEADS, HG_DK, HG_DV), xp.dtype)
        xp, sp, kp, vp = trunk_layer(xp, mk, mv, s0, None, None, lb_all[l], rel_bias, *lw)
        hg_p.append(sp); swk_p.append(kp); swv_p.append(vp); mk_p.append(mk); mv_p.append(mv)
        xs, ss, ks_new, vs_new = trunk_layer(xs, cache_mem_k[l], cache_mem_v[l], state_hgrn[l],
                                             cache_swa_k[l], cache_swa_v[l], lb_all[l], rel_bias, *lw)
        hg_s.append(ss); swk_s.append(ks_new); swv_s.append(vs_new)
    y_prompt = rmsnorm(xp, final_norm_w)
    y_sample = rmsnorm(xs, final_norm_w)
    return (y_prompt, y_sample,
            jnp.stack(hg_p), jnp.stack(swk_p), jnp.stack(swv_p), jnp.stack(mk_p), jnp.stack(mv_p),
            jnp.stack(hg_s), jnp.stack(swk_s), jnp.stack(swv_s))
```

```python
import functools
import math

import numpy as np
import jax
import jax.numpy as jnp
from jax import lax
from jax.experimental import pallas as pl
from jax.experimental.pallas import tpu as pltpu

EPS = 1e-6
NEG_INF = -1e30
CHUNK = 64

HG_HEADS = 8
HG_DK = 128
HG_DV = 128
HG_WIDTH = HG_HEADS * HG_DV
HG_SUB = 16
HG_CHUNK = 64

SWA_HEADS = 16
SWA_KV_HEADS = 4
SWA_GROUP = SWA_HEADS // SWA_KV_HEADS
SWA_HEAD_DIM = 64
SWA_WIDTH = SWA_HEADS * SWA_HEAD_DIM
SWA_KV_WIDTH = SWA_KV_HEADS * SWA_HEAD_DIM
SWA_SCALE = SWA_HEAD_DIM ** -0.5
WINDOW = 128
WIN_CHUNKS = WINDOW // CHUNK
REL_BUCKETS = 32
REL_MAX_DIST = 128

N_MEM = 256
MEM_HEADS = 4
MEM_HEAD_DIM = 128
MEM_INNER = MEM_HEADS * MEM_HEAD_DIM
MEM_SCALE = MEM_HEAD_DIM ** -0.5

PEER_HEADS = 8
PEER_NKEYS = 128
PEER_DHALF = 128
PEER_TOPK = 16
PEER_PAIRS = PEER_HEADS * PEER_TOPK

OFF_Q, OFF_F, OFF_I, OFF_G = 0, 1024, 2048, 3072
OFF_QSW, OFF_KSW, OFF_VSW = 4096, 5120, 5376
IN_WIDTH = 5632

VMEM_LIMIT_BYTES = 56 * 1024 * 1024

_NT = (((1,), (1,)), ((), ()))
_TN = (((0,), (0,)), ((), ()))


def _cparams(sem):
    return pltpu.CompilerParams(dimension_semantics=sem, vmem_limit_bytes=VMEM_LIMIT_BYTES)


def _rms(x, w):
    return x * lax.rsqrt(jnp.mean(x * x, axis=-1, keepdims=True) + EPS) * w


def _norm_matmul_kernel(x_ref, nw_ref, w_ref, o_ref, h_ref):
    @pl.when(pl.program_id(1) == 0)
    def _():
        h_ref[...] = _rms(x_ref[...], nw_ref[...]).astype(h_ref.dtype)

    o_ref[...] = jnp.dot(h_ref[...], w_ref[...], preferred_element_type=jnp.float32)


def norm_matmul(x, nw, w, tm=512, tn=512):
    n, d = x.shape
    m = w.shape[1]
    tm = min(tm, n)
    tn = min(tn, m)
    assert n % tm == 0 and m % tn == 0
    return pl.pallas_call(
        _norm_matmul_kernel,
        grid=(n // tm, m // tn),
        in_specs=[
            pl.BlockSpec((tm, d), lambda i, j: (i, 0)),
            pl.BlockSpec((1, d), lambda i, j: (0, 0)),
            pl.BlockSpec((d, tn), lambda i, j: (0, j)),
        ],
        out_specs=pl.BlockSpec((tm, tn), lambda i, j: (i, j)),
        out_shape=jax.ShapeDtypeStruct((n, m), jnp.float32),
        scratch_shapes=[pltpu.VMEM((tm, d), jnp.bfloat16)],
        compiler_params=_cparams(("parallel", "arbitrary")),
        name="norm_matmul",
    )(x, nw.reshape(1, d), w)


def _cumsum_rows(x):
    n = x.shape[0]
    row = lax.broadcasted_iota(jnp.int32, x.shape, 0)
    s = 1
    while s < n:
        x = x + jnp.where(row >= s, pltpu.roll(x, s, axis=0), 0.0)
        s *= 2
    return x


def _bcast_rows(x, idxs, g):
    return jnp.concatenate(
        [jnp.broadcast_to(x[i:i + 1, :], (g, x.shape[1])) for i in idxs], axis=0)


def _hgrn_chunk(q, fpre, v, lb, st):
    c = q.shape[0]
    f = lb + (1.0 - lb) * jax.nn.sigmoid(fpre)
    k = 1.0 - f
    lf = jnp.log(f)
    bc = _cumsum_rows(lf)
    be = bc - lf
    bf16 = jnp.bfloat16
    ti = lax.broadcasted_iota(jnp.int32, (c, c), 0)
    si = lax.broadcasted_iota(jnp.int32, (c, c), 1)

    ref0 = _bcast_rows(be, range(0, c, HG_SUB), HG_SUB)
    qd = (q * jnp.exp(bc - ref0)).astype(bf16)
    kd = (k * jnp.exp(ref0 - bc)).astype(bf16)
    a = lax.dot_general(qd, kd, _NT, preferred_element_type=jnp.float32)
    attn = jnp.where((ti // HG_SUB == si // HG_SUB) & (si <= ti), a, 0.0)
    g = 2 * HG_SUB
    while g <= c:
        half = g // 2
        ref = _bcast_rows(bc, range(half - 1, c, g), g)
        ql = (q * jnp.exp(jnp.minimum(bc - ref, 0.0))).astype(bf16)
        kl = (k * jnp.exp(jnp.minimum(ref - bc, 0.0))).astype(bf16)
        a = lax.dot_general(ql, kl, _NT, preferred_element_type=jnp.float32)
        m = (ti // g == si // g) & (ti % g >= half) & (si % g < half)
        attn = jnp.where(m, a, attn)
        g *= 2

    vb = v.astype(bf16)
    q_in = (q * jnp.exp(bc)).astype(bf16)
    o = (lax.dot_general(q_in, st.astype(bf16), _NT, preferred_element_type=jnp.float32)
         + jnp.dot(attn.astype(bf16), vb, preferred_element_type=jnp.float32))
    b_last = bc[c - 1:c, :]
    k_out = (k * jnp.exp(b_last - bc)).astype(bf16)
    st_new = st * jnp.exp(b_last) + lax.dot_general(vb, k_out, _TN,
                                                    preferred_element_type=jnp.float32)
    return o, st_new


def _hgrn_kernel(q_ref, f_ref, i_ref, g_ref, s0_ref, lb_ref, nw_ref, o_ref, sfin_ref, st_ref):
    ci = pl.program_id(2)

    @pl.when(ci == 0)
    def _():
        st_ref[...] = s0_ref[0, 0].T

    ct = q_ref.shape[1]
    lb = lb_ref[...]
    nw = nw_ref[...]
    for j in range(ct // HG_CHUNK):
        sl = pl.ds(j * HG_CHUNK, HG_CHUNK)
        o, st_new = _hgrn_chunk(q_ref[0, sl, :], f_ref[0, sl, :], i_ref[0, sl, :], lb, st_ref[...])
        st_ref[...] = st_new
        o = _rms(o, nw)
        o_ref[0, sl, :] = (o * jax.nn.silu(g_ref[0, sl, :])).astype(o_ref.dtype)

    @pl.when(ci == pl.num_programs(2) - 1)
    def _():
        sfin_ref[0, 0] = st_ref[...].T


def hgrn(proj3, s0, lb, nw):
    b, t, _ = proj3.shape
    ct = min(256, t)
    assert t % ct == 0 and ct % HG_CHUNK == 0

    def col(off):
        return pl.BlockSpec((1, ct, HG_DK), lambda bi, h, c: (bi, c, off // HG_DK + h))

    return pl.pallas_call(
        _hgrn_kernel,
        grid=(b, HG_HEADS, t // ct),
        in_specs=[
            col(OFF_Q), col(OFF_F), col(OFF_I), col(OFF_G),
            pl.BlockSpec((1, 1, HG_DK, HG_DV), lambda bi, h, c: (bi, h, 0, 0)),
            pl.BlockSpec((1, HG_DK), lambda bi, h, c: (0, h)),
            pl.BlockSpec((1, HG_DV), lambda bi, h, c: (0, 0)),
        ],
        out_specs=[
            pl.BlockSpec((1, ct, HG_DV), lambda bi, h, c: (bi, c, h)),
            pl.BlockSpec((1, 1, HG_DK, HG_DV), lambda bi, h, c: (bi, h, 0, 0)),
        ],
        out_shape=[
            jax.ShapeDtypeStruct((b, t, HG_WIDTH), jnp.bfloat16),
            jax.ShapeDtypeStruct((b, HG_HEADS, HG_DK, HG_DV), jnp.float32),
        ],
        scratch_shapes=[pltpu.VMEM((HG_DV, HG_DK), jnp.float32)],
        compiler_params=_cparams(("parallel", "parallel", "arbitrary")),
        name="hgrn",
    )(proj3, proj3, proj3, proj3, s0, lb.reshape(1, -1), nw.reshape(1, -1))


def _swa_kernel(q_ref, k2_ref, k1_ref, k0_ref, v2_ref, v1_ref, v0_ref, bias_ref, sink_ref,
                o_ref, *, c_off):
    bf16 = jnp.bfloat16
    c = pl.program_id(1) + c_off
    kk = jnp.concatenate([k2_ref[0], k1_ref[0], k0_ref[0]], axis=0).astype(bf16)
    vv = jnp.concatenate([v2_ref[0], v1_ref[0], v0_ref[0]], axis=0).astype(bf16)
    n_keys = kk.shape[0]
    key_chunk = lax.broadcasted_iota(jnp.int32, (1, n_keys), 1) // CHUNK
    valid = (c - WIN_CHUNKS + key_chunk) >= 0
    q = q_ref[0]
    hd = SWA_HEAD_DIM
    for kv in range(SWA_KV_HEADS):
        kh = kk[:, kv * hd:(kv + 1) * hd]
        vh = vv[:, kv * hd:(kv + 1) * hd]
        qh = jnp.concatenate(
            [q[:, (kv * SWA_GROUP + g) * hd:(kv * SWA_GROUP + g + 1) * hd]
             for g in range(SWA_GROUP)], axis=0).astype(bf16)
        logits = lax.dot_general(qh, kh, _NT, preferred_element_type=jnp.float32)
        logits = logits * SWA_SCALE + bias_ref[kv]
        logits = jnp.where(valid, logits, NEG_INF)
        sink = sink_ref[kv]
        m = jnp.maximum(jnp.max(logits, axis=-1, keepdims=True), sink)
        e = jnp.exp(logits - m)
        p = e / (jnp.sum(e, axis=-1, keepdims=True) + jnp.exp(sink - m))
        o = jnp.dot(p.astype(bf16), vh, preferred_element_type=jnp.float32)
        for g in range(SWA_GROUP):
            col = (kv * SWA_GROUP + g) * hd
            o_ref[0, :, col:col + hd] = o[g * CHUNK:(g + 1) * CHUNK].astype(o_ref.dtype)


def swa(proj3, bias, sink_col, cache_k=None, cache_v=None):
    b, t, _ = proj3.shape
    nc = t // CHUNK
    assert t % CHUNK == 0
    qspec = pl.BlockSpec((1, CHUNK, SWA_WIDTH), lambda bi, c: (bi, c, OFF_QSW // SWA_WIDTH))

    def cur(off):
        return pl.BlockSpec((1, CHUNK, SWA_KV_WIDTH), lambda bi, c: (bi, c, off // SWA_KV_WIDTH))

    if cache_k is None:
        def prev(off, back):
            return pl.BlockSpec((1, CHUNK, SWA_KV_WIDTH),
                                lambda bi, c: (bi, jnp.maximum(c - back, 0), off // SWA_KV_WIDTH))
        kspecs = [prev(OFF_KSW, 2), prev(OFF_KSW, 1), cur(OFF_KSW)]
        vspecs = [prev(OFF_VSW, 2), prev(OFF_VSW, 1), cur(OFF_VSW)]
        karrs = [proj3, proj3, proj3]
        varrs = [proj3, proj3, proj3]
        c_off = 0
    else:
        assert nc == 1 and cache_k.shape[1] == WINDOW

        def past(j):
            return pl.BlockSpec((1, CHUNK, SWA_KV_WIDTH), lambda bi, c: (bi, j, 0))
        kspecs = [past(0), past(1), cur(OFF_KSW)]
        vspecs = [past(0), past(1), cur(OFF_VSW)]
        karrs = [cache_k, cache_k, proj3]
        varrs = [cache_v, cache_v, proj3]
        c_off = WIN_CHUNKS
    rows = SWA_GROUP * CHUNK
    n_keys = (WIN_CHUNKS + 1) * CHUNK
    return pl.pallas_call(
        functools.partial(_swa_kernel, c_off=c_off),
        grid=(b, nc),
        in_specs=[qspec] + kspecs + vspecs + [
            pl.BlockSpec((SWA_KV_HEADS, rows, n_keys), lambda bi, c: (0, 0, 0)),
            pl.BlockSpec((SWA_KV_HEADS, rows, 1), lambda bi, c: (0, 0, 0)),
        ],
        out_specs=pl.BlockSpec((1, CHUNK, SWA_WIDTH), lambda bi, c: (bi, c, 0)),
        out_shape=jax.ShapeDtypeStruct((b, t, SWA_WIDTH), jnp.bfloat16),
        compiler_params=_cparams(("parallel", "parallel")),
        name="swa",
    )(proj3, *karrs, *varrs, bias, sink_col)


def _t5_bias(rel, table):
    nb = REL_BUCKETS // 2
    max_exact = nb // 2
    side = jnp.where(rel > 0, nb, 0)
    n = jnp.abs(rel)
    n_f = jnp.maximum(n, max_exact).astype(jnp.float32)
    large = max_exact + (jnp.log(n_f / max_exact) / math.log(REL_MAX_DIST / max_exact)
                         * (nb - max_exact)).astype(jnp.int32)
    large = jnp.minimum(large, nb - 1)
    bucket = side + jnp.where(n < max_exact, n, large)
    return jnp.transpose(table[bucket].astype(jnp.float32), (2, 0, 1))


def _outproj_kernel(a1_ref, a2_ref, w_ref, x_ref, o_ref):
    k1 = a1_ref.shape[1]
    acc = jnp.dot(a1_ref[...], w_ref[0:k1, :], preferred_element_type=jnp.float32)
    acc += jnp.dot(a2_ref[...], w_ref[k1:, :], preferred_element_type=jnp.float32)
    o_ref[...] = x_ref[...] + acc


def outproj(a1, a2, w, x, tm=512, tn=512):
    n, d = x.shape
    k1, k2 = a1.shape[1], a2.shape[1]
    tm = min(tm, n)
    assert n % tm == 0 and d % tn == 0
    return pl.pallas_call(
        _outproj_kernel,
        grid=(n // tm, d // tn),
        in_specs=[
            pl.BlockSpec((tm, k1), lambda i, j: (i, 0)),
            pl.BlockSpec((tm, k2), lambda i, j: (i, 0)),
            pl.BlockSpec((k1 + k2, tn), lambda i, j: (0, j)),
            pl.BlockSpec((tm, tn), lambda i, j: (i, j)),
        ],
        out_specs=pl.BlockSpec((tm, tn), lambda i, j: (i, j)),
        out_shape=jax.ShapeDtypeStruct((n, d), jnp.float32),
        compiler_params=_cparams(("parallel", "arbitrary")),
        name="outproj",
    )(a1, a2, w, x)


def _memattn_kernel(x_ref, nw_ref, wq_ref, mk_ref, mv_ref, wo_ref, o_ref):
    bf16 = jnp.bfloat16
    x = x_ref[0]
    h = _rms(x, nw_ref[...]).astype(bf16)
    q = jnp.dot(h, wq_ref[...], preferred_element_type=jnp.float32)
    mk = mk_ref[0].astype(bf16)
    mv = mv_ref[0].astype(bf16)
    outs = []
    for hh in range(MEM_HEADS):
        sl = slice(hh * MEM_HEAD_DIM, (hh + 1) * MEM_HEAD_DIM)
        logits = lax.dot_general(q[:, sl].astype(bf16), mk[:, sl], _NT,
                                 preferred_element_type=jnp.float32) * MEM_SCALE
        m = jnp.max(logits, axis=-1, keepdims=True)
        e = jnp.exp(logits - m)
        p = e / jnp.sum(e, axis=-1, keepdims=True)
        outs.append(jnp.dot(p.astype(bf16), mv[:, sl], preferred_element_type=jnp.float32))
    o = jnp.concatenate(outs, axis=-1).astype(bf16)
    o_ref[0] = x + jnp.dot(o, wo_ref[...], preferred_element_type=jnp.float32)


def memattn(x3, nw, wq, mk_arr, mv_arr, mk_col, mv_col, wo, tm=512):
    b, t, d = x3.shape
    tm = min(tm, t)
    assert t % tm == 0
    return pl.pallas_call(
        _memattn_kernel,
        grid=(b, t // tm),
        in_specs=[
            pl.BlockSpec((1, tm, d), lambda bi, i: (bi, i, 0)),
            pl.BlockSpec((1, d), lambda bi, i: (0, 0)),
            pl.BlockSpec((d, MEM_INNER), lambda bi, i: (0, 0)),
            pl.BlockSpec((1, N_MEM, MEM_INNER), lambda bi, i: (bi, 0, mk_col)),
            pl.BlockSpec((1, N_MEM, MEM_INNER), lambda bi, i: (bi, 0, mv_col)),
            pl.BlockSpec((MEM_INNER, d), lambda bi, i: (0, 0)),
        ],
        out_specs=pl.BlockSpec((1, tm, d), lambda bi, i: (bi, i, 0)),
        out_shape=jax.ShapeDtypeStruct((b, t, d), jnp.float32),
        compiler_params=_cparams(("parallel", "arbitrary")),
        name="memattn",
    )(x3, nw.reshape(1, d), wq, mk_arr, mv_arr, wo)


def _topk_rows(s, k, payload=None):
    n = s.shape[0]
    row = lax.broadcasted_iota(jnp.int32, s.shape, 0)
    vals, picks = [], []
    for _ in range(k):
        m = jnp.max(s, axis=0, keepdims=True)
        idx = jnp.min(jnp.where(s == m, row, n), axis=0, keepdims=True)
        hit = row == idx
        vals.append(m)
        if payload is None:
            picks.append(idx)
        else:
            picks.append(jnp.sum(jnp.where(hit, payload, 0), axis=0, keepdims=True))
        s = jnp.where(hit, -jnp.inf, s)
    return jnp.concatenate(vals, axis=0), jnp.concatenate(picks, axis=0)


def _peer_retrieve_kernel(q_ref, sk_ref, eidx_ref, gate_ref):
    bf16 = jnp.bfloat16
    k = PEER_TOPK
    eids, gates = [], []
    for h in range(PEER_HEADS):
        tops = []
        for c in range(2):
            col = (h * 2 + c) * PEER_DHALF
            qh = q_ref[:, col:col + PEER_DHALF].astype(bf16)
            st = lax.dot_general(sk_ref[c, h], qh, _NT,
                                 preferred_element_type=jnp.float32)
            tops.append(_topk_rows(st, k))
        (s1, i1), (s2, i2) = tops
        cand = jnp.concatenate([s1[a:a + 1] + s2 for a in range(k)], axis=0)
        cidx = jnp.concatenate([i1[a:a + 1] * PEER_NKEYS + i2 for a in range(k)], axis=0)
        top_s, eid = _topk_rows(cand, k, payload=cidx)
        e = jnp.exp(top_s - top_s[0:1])
        gates.append(e / jnp.sum(e, axis=0, keepdims=True))
        eids.append(eid)
    eidx_ref[...] = jnp.concatenate(eids, axis=0).T
    gate_ref[...] = jnp.concatenate(gates, axis=0).T


def peer_retrieve(qp, subkeys, tn=128):
    n, d = qp.shape
    tn = min(tn, n)
    assert n % tn == 0
    return pl.pallas_call(
        _peer_retrieve_kernel,
        grid=(n // tn,),
        in_specs=[
            pl.BlockSpec((tn, d), lambda i: (i, 0)),
            pl.BlockSpec(subkeys.shape, lambda i: (0, 0, 0, 0)),
        ],
        out_specs=[
            pl.BlockSpec((tn, PEER_PAIRS), lambda i: (i, 0)),
            pl.BlockSpec((tn, PEER_PAIRS), lambda i: (i, 0)),
        ],
        out_shape=[
            jax.ShapeDtypeStruct((n, PEER_PAIRS), jnp.int32),
            jax.ShapeDtypeStruct((n, PEER_PAIRS), jnp.float32),
        ],
        compiler_params=_cparams(("parallel",)),
        name="peer_retrieve",
    )(qp, subkeys)


PEER_SLOTS = 4


def _peer_main_kernel(eidx_ref, gate_ref, x_ref, nw_ref, fw_ref, u_hbm, v_hbm, o_ref,
                      h_ref, ubuf, vbuf, sem):
    bf16 = jnp.bfloat16
    tb = x_ref.shape[0]
    h_ref[...] = _rms(x_ref[...], nw_ref[...])

    def row_copies(t, p, slot):
        e = eidx_ref[t, p]
        return (pltpu.make_async_copy(u_hbm.at[pl.ds(e, 1)], ubuf.at[slot, pl.ds(p, 1)],
                                      sem.at[0, slot]),
                pltpu.make_async_copy(v_hbm.at[pl.ds(e, 1)], vbuf.at[slot, pl.ds(p, 1)],
                                      sem.at[1, slot]))

    def issue(t, slot):
        for p in range(PEER_PAIRS):
            cu, cv = row_copies(t, p, slot)
            cu.start()
            cv.start()

    def wait(slot):
        pltpu.make_async_copy(u_hbm.at[pl.ds(0, PEER_PAIRS)], ubuf.at[slot], sem.at[0, slot]).wait()
        pltpu.make_async_copy(v_hbm.at[pl.ds(0, PEER_PAIRS)], vbuf.at[slot], sem.at[1, slot]).wait()

    for t in range(PEER_SLOTS):
        issue(t, t)

    def body(t, carry):
        slot = t % PEER_SLOTS
        wait(slot)
        hb = jnp.broadcast_to(h_ref[pl.ds(t, 1), :], (8, h_ref.shape[1])).astype(bf16)
        act = lax.dot_general(hb, ubuf[slot].astype(bf16), _NT,
                              preferred_element_type=jnp.float32)[0:1]
        gelu = 0.5 * act * (1.0 + lax.erf(act * np.float32(math.sqrt(0.5))))
        w = gate_ref[pl.ds(t, 1), :] * gelu
        wb = jnp.broadcast_to(w, (8, PEER_PAIRS)).astype(bf16)
        y = jnp.dot(wb, vbuf[slot].astype(bf16), preferred_element_type=jnp.float32)[0:1]
        o_ref[pl.ds(t, 1), :] = _rms(x_ref[pl.ds(t, 1), :] + y, fw_ref[...])

        @pl.when(t + PEER_SLOTS < tb)
        def _():
            issue(t + PEER_SLOTS, slot)
        return carry

    lax.fori_loop(0, tb, body, 0)


def peer_main(eidx, gate, x, nw, fw, u_tab, v_tab, tb=64):
    n, d = x.shape
    tb = min(tb, n)
    assert n % tb == 0 and tb >= PEER_SLOTS
    return pl.pallas_call(
        _peer_main_kernel,
        grid=(n // tb,),
        in_specs=[
            pl.BlockSpec((tb, PEER_PAIRS), lambda i: (i, 0), memory_space=pltpu.SMEM),
            pl.BlockSpec((tb, PEER_PAIRS), lambda i: (i, 0)),
            pl.BlockSpec((tb, d), lambda i: (i, 0)),
            pl.BlockSpec((1, d), lambda i: (0, 0)),
            pl.BlockSpec((1, d), lambda i: (0, 0)),
            pl.BlockSpec(memory_space=pl.ANY),
            pl.BlockSpec(memory_space=pl.ANY),
        ],
        out_specs=pl.BlockSpec((tb, d), lambda i: (i, 0)),
        out_shape=jax.ShapeDtypeStruct((n, d), jnp.float32),
        scratch_shapes=[
            pltpu.VMEM((tb, d), jnp.float32),
            pltpu.VMEM((PEER_SLOTS, PEER_PAIRS, d), jnp.float32),
            pltpu.VMEM((PEER_SLOTS, PEER_PAIRS, d), jnp.float32),
            pltpu.SemaphoreType.DMA((2, PEER_SLOTS)),
        ],
        compiler_params=_cparams(("arbitrary",)),
        name="peer_main",
    )(eidx, gate, x, nw.reshape(1, d), fw.reshape(1, d), u_tab, v_tab)


def _trunk(x, mk_arr, mv_arr, mk_col, mv_col, s0, cache_k, cache_v, p):
    b, t, d = x.shape
    n = b * t
    proj = norm_matmul(x.reshape(n, d), p["norm_mix_w"], p["w_in"])
    proj3 = proj.reshape(b, t, IN_WIDTH)
    o_hg, s_fin = hgrn(proj3, s0, p["lb"], p["hg_norm_w"])
    o_sw = swa(proj3, p["bias"], p["sink_col"], cache_k, cache_v)
    x1 = outproj(o_hg.reshape(n, HG_WIDTH), o_sw.reshape(n, SWA_WIDTH), p["w_out"], x.reshape(n, d))
    x2 = memattn(x1.reshape(b, t, d), p["norm_cross_w"], p["mem_wq"], mk_arr, mv_arr,
                 mk_col, mv_col, p["mem_wo"]).reshape(n, d)
    qp = norm_matmul(x2, p["norm_ffn_w"], p["peer_wq"])
    eidx, gate = peer_retrieve(qp, p["peer_subkeys"])
    y = peer_main(eidx, gate, x2, p["norm_ffn_w"], p["final_norm_w"], p["peer_u"], p["peer_v"])
    keep = min(WINDOW, t) if cache_k is None else t
    k_rows = proj3[:, t - keep:, OFF_KSW:OFF_KSW + SWA_KV_WIDTH]
    v_rows = proj3[:, t - keep:, OFF_VSW:OFF_VSW + SWA_KV_WIDTH]
    k_rows = k_rows.reshape(b, keep, SWA_KV_HEADS, SWA_HEAD_DIM)
    v_rows = v_rows.reshape(b, keep, SWA_KV_HEADS, SWA_HEAD_DIM)
    return y.reshape(b, t, d), s_fin, k_rows, v_rows


def kernel(x_prompt, x_sample, mem_prompt, state_hgrn, cache_swa_k, cache_swa_v, cache_mem_k, cache_mem_v, rel_bias, hg_lb_logits, norm_mix_w, w_in, hg_norm_w, swa_sinks, w_out, norm_mem_w, norm_cross_w, mem_wq, mem_wk, mem_wv, mem_wo, norm_ffn_w, peer_wq, peer_subkeys, peer_u, peer_v, final_norm_w):
    bf16 = jnp.bfloat16
    depth = w_in.shape[0]
    assert depth == 1
    l = 0
    lb_all = jnp.cumsum(jax.nn.softmax(hg_lb_logits.astype(jnp.float32), axis=0), axis=0)
    n_keys = (WIN_CHUNKS + 1) * CHUNK
    rel = jnp.arange(n_keys)[None, :] - (WIN_CHUNKS * CHUNK + jnp.arange(CHUNK))[:, None]
    bias = _t5_bias(rel, rel_bias).reshape(SWA_KV_HEADS, SWA_GROUP * CHUNK, n_keys)
    sink_col = jnp.broadcast_to(
        swa_sinks[l].astype(jnp.float32).reshape(SWA_KV_HEADS, SWA_GROUP, 1),
        (SWA_KV_HEADS, SWA_GROUP, CHUNK)).reshape(SWA_KV_HEADS, SWA_GROUP * CHUNK, 1)
    p = {
        "norm_mix_w": norm_mix_w[l], "w_in": w_in[l].astype(bf16), "lb": lb_all[l],
        "hg_norm_w": hg_norm_w[l], "bias": bias, "sink_col": sink_col,
        "w_out": w_out[l].astype(bf16), "norm_cross_w": norm_cross_w[l],
        "mem_wq": mem_wq[l].astype(bf16), "mem_wo": mem_wo[l].astype(bf16),
        "norm_ffn_w": norm_ffn_w[l], "peer_wq": peer_wq[l].astype(bf16),
        "peer_subkeys": peer_subkeys[l].astype(bf16), "peer_u": peer_u[l], "peer_v": peer_v[l],
        "final_norm_w": final_norm_w,
    }
    bp, tp, d = x_prompt.shape
    bs, ts, _ = x_sample.shape

    wkv = jnp.concatenate([mem_wk[l], mem_wv[l]], axis=1).astype(bf16)
    kv = norm_matmul(mem_prompt.reshape(bp * N_MEM, d), norm_mem_w[l], wkv)
    kv3 = kv.reshape(bp, N_MEM, 2 * MEM_INNER)
    mk = kv3[:, :, :MEM_INNER].reshape(bp, N_MEM, MEM_HEADS, MEM_HEAD_DIM)
    mv = kv3[:, :, MEM_INNER:].reshape(bp, N_MEM, MEM_HEADS, MEM_HEAD_DIM)

    s0 = jnp.zeros((bp, HG_HEADS, HG_DK, HG_DV), jnp.float32)
    yp, sp, kp, vp = _trunk(x_prompt, kv3, kv3, 0, 1, s0, None, None, p)

    cmk = cache_mem_k[l].reshape(bs, N_MEM, MEM_INNER)
    cmv = cache_mem_v[l].reshape(bs, N_MEM, MEM_INNER)
    ck = cache_swa_k[l].reshape(bs, -1, SWA_KV_WIDTH)
    cv = cache_swa_v[l].reshape(bs, -1, SWA_KV_WIDTH)
    ys, ss, ks_new, vs_new = _trunk(x_sample, cmk, cmv, 0, 0, state_hgrn[l], ck, cv, p)

    return (yp, ys, sp[None], kp[None], vp[None], mk[None], mv[None],
            ss[None], ks_new[None], vs_new[None])
```

```python
import functools
import math

import numpy as np
import jax
import jax.numpy as jnp
from jax import lax
from jax.experimental import pallas as pl
from jax.experimental.pallas import tpu as pltpu

EPS = 1e-6
NEG_INF = -1e30
CHUNK = 64

HG_HEADS = 8
HG_DK = 128
HG_DV = 128
HG_WIDTH = HG_HEADS * HG_DV
HG_SUB = 16
HG_CHUNK = 64

SWA_HEADS = 16
SWA_KV_HEADS = 4
SWA_GROUP = SWA_HEADS // SWA_KV_HEADS
SWA_HEAD_DIM = 64
SWA_WIDTH = SWA_HEADS * SWA_HEAD_DIM
SWA_KV_WIDTH = SWA_KV_HEADS * SWA_HEAD_DIM
SWA_SCALE = SWA_HEAD_DIM ** -0.5
WINDOW = 128
WIN_CHUNKS = WINDOW // CHUNK
REL_BUCKETS = 32
REL_MAX_DIST = 128

N_MEM = 256
MEM_HEADS = 4
MEM_HEAD_DIM = 128
MEM_INNER = MEM_HEADS * MEM_HEAD_DIM
MEM_SCALE = MEM_HEAD_DIM ** -0.5

PEER_HEADS = 8
PEER_NKEYS = 128
PEER_DHALF = 128
PEER_TOPK = 16
PEER_PAIRS = PEER_HEADS * PEER_TOPK

OFF_Q, OFF_F, OFF_I, OFF_G = 0, 1024, 2048, 3072
OFF_QSW, OFF_KSW, OFF_VSW = 4096, 5120, 5376
IN_WIDTH = 5632

VMEM_LIMIT_BYTES = 56 * 1024 * 1024

_NT = (((1,), (1,)), ((), ()))
_TN = (((0,), (0,)), ((), ()))


def _cparams(sem):
    return pltpu.CompilerParams(dimension_semantics=sem, vmem_limit_bytes=VMEM_LIMIT_BYTES)


def _rms(x, w):
    return x * lax.rsqrt(jnp.mean(x * x, axis=-1, keepdims=True) + EPS) * w


def _norm_matmul_kernel(x_ref, nw_ref, w_ref, o_ref, h_ref):
    @pl.when(pl.program_id(1) == 0)
    def _():
        h_ref[...] = _rms(x_ref[...], nw_ref[...]).astype(h_ref.dtype)

    o_ref[...] = jnp.dot(h_ref[...], w_ref[...], preferred_element_type=jnp.float32)


def norm_matmul(x, nw, w, tm=512, tn=512):
    n, d = x.shape
    m = w.shape[1]
    tm = min(tm, n)
    tn = min(tn, m)
    assert n % tm == 0 and m % tn == 0
    return pl.pallas_call(
        _norm_matmul_kernel,
        grid=(n // tm, m // tn),
        in_specs=[
            pl.BlockSpec((tm, d), lambda i, j: (i, 0)),
            pl.BlockSpec((1, d), lambda i, j: (0, 0)),
            pl.BlockSpec((d, tn), lambda i, j: (0, j)),
        ],
        out_specs=pl.BlockSpec((tm, tn), lambda i, j: (i, j)),
        out_shape=jax.ShapeDtypeStruct((n, m), jnp.float32),
        scratch_shapes=[pltpu.VMEM((tm, d), jnp.bfloat16)],
        compiler_params=_cparams(("parallel", "arbitrary")),
        name="norm_matmul",
    )(x, nw.reshape(1, d), w)


def _cumsum_rows(x):
    n = x.shape[0]
    row = lax.broadcasted_iota(jnp.int32, x.shape, 0)
    s = 1
    while s < n:
        x = x + jnp.where(row >= s, pltpu.roll(x, s, axis=0), 0.0)
        s *= 2
    return x


def _bcast_rows(x, idxs, g):
    return jnp.concatenate(
        [jnp.broadcast_to(x[i:i + 1, :], (g, x.shape[1])) for i in idxs], axis=0)


def _hgrn_chunk(q, fpre, v, lb, st):
    c = q.shape[0]
    f = lb + (1.0 - lb) * jax.nn.sigmoid(fpre)
    k = 1.0 - f
    lf = jnp.log(f)
    bc = _cumsum_rows(lf)
    be = bc - lf
    bf16 = jnp.bfloat16
    ti = lax.broadcasted_iota(jnp.int32, (c, c), 0)
    si = lax.broadcasted_iota(jnp.int32, (c, c), 1)

    ref0 = _bcast_rows(be, range(0, c, HG_SUB), HG_SUB)
    qd = (q * jnp.exp(bc - ref0)).astype(bf16)
    kd = (k * jnp.exp(ref0 - bc)).astype(bf16)
    a = lax.dot_general(qd, kd, _NT, preferred_element_type=jnp.float32)
    attn = jnp.where((ti // HG_SUB == si // HG_SUB) & (si <= ti), a, 0.0)
    g = 2 * HG_SUB
    while g <= c:
        half = g // 2
        ref = _bcast_rows(bc, range(half - 1, c, g), g)
        ql = (q * jnp.exp(jnp.minimum(bc - ref, 0.0))).astype(bf16)
        kl = (k * jnp.exp(jnp.minimum(ref - bc, 0.0))).astype(bf16)
        a = lax.dot_general(ql, kl, _NT, preferred_element_type=jnp.float32)
        m = (ti // g == si // g) & (ti % g >= half) & (si % g < half)
        attn = jnp.where(m, a, attn)
        g *= 2

    vb = v.astype(bf16)
    q_in = (q * jnp.exp(bc)).astype(bf16)
    o = (lax.dot_general(q_in, st.astype(bf16), _NT, preferred_element_type=jnp.float32)
         + jnp.dot(attn.astype(bf16), vb, preferred_element_type=jnp.float32))
    b_last = bc[c - 1:c, :]
    k_out = (k * jnp.exp(b_last - bc)).astype(bf16)
    st_new = st * jnp.exp(b_last) + lax.dot_general(vb, k_out, _TN,
                                                    preferred_element_type=jnp.float32)
    return o, st_new


def _hgrn_kernel(q_ref, f_ref, i_ref, g_ref, s0_ref, lb_ref, nw_ref, o_ref, sfin_ref, st_ref):
    ci = pl.program_id(2)

    @pl.when(ci == 0)
    def _():
        st_ref[...] = s0_ref[0, 0].T

    ct = q_ref.shape[1]
    lb = lb_ref[...]
    nw = nw_ref[...]
    for j in range(ct // HG_CHUNK):
        sl = pl.ds(j * HG_CHUNK, HG_CHUNK)
        o, st_new = _hgrn_chunk(q_ref[0, sl, :], f_ref[0, sl, :], i_ref[0, sl, :], lb, st_ref[...])
        st_ref[...] = st_new
        o = _rms(o, nw)
        o_ref[0, sl, :] = (o * jax.nn.silu(g_ref[0, sl, :])).astype(o_ref.dtype)

    @pl.when(ci == pl.num_programs(2) - 1)
    def _():
        sfin_ref[0, 0] = st_ref[...].T


def hgrn(proj3, s0, lb, nw):
    b, t, _ = proj3.shape
    ct = min(256, t)
    assert t % ct == 0 and ct % HG_CHUNK == 0

    def col(off):
        return pl.BlockSpec((1, ct, HG_DK), lambda bi, h, c: (bi, c, off // HG_DK + h))

    return pl.pallas_call(
        _hgrn_kernel,
        grid=(b, HG_HEADS, t // ct),
        in_specs=[
            col(OFF_Q), col(OFF_F), col(OFF_I), col(OFF_G),
            pl.BlockSpec((1, 1, HG_DK, HG_DV), lambda bi, h, c: (bi, h, 0, 0)),
            pl.BlockSpec((1, HG_DK), lambda bi, h, c: (0, h)),
            pl.BlockSpec((1, HG_DV), lambda bi, h, c: (0, 0)),
        ],
        out_specs=[
            pl.BlockSpec((1, ct, HG_DV), lambda bi, h, c: (bi, c, h)),
            pl.BlockSpec((1, 1, HG_DK, HG_DV), lambda bi, h, c: (bi, h, 0, 0)),
        ],
        out_shape=[
            jax.ShapeDtypeStruct((b, t, HG_WIDTH), jnp.bfloat16),
            jax.ShapeDtypeStruct((b, HG_HEADS, HG_DK, HG_DV), jnp.float32),
        ],
        scratch_shapes=[pltpu.VMEM((HG_DV, HG_DK), jnp.float32)],
        compiler_params=_cparams(("parallel", "parallel", "arbitrary")),
        name="hgrn",
    )(proj3, proj3, proj3, proj3, s0, lb.reshape(1, -1), nw.reshape(1, -1))


def _swa_kernel(q_ref, k2_ref, k1_ref, k0_ref, v2_ref, v1_ref, v0_ref, bias_ref, sink_ref,
                o_ref, *, c_off):
    bf16 = jnp.bfloat16
    c = pl.program_id(1) + c_off
    kk = jnp.concatenate([k2_ref[0], k1_ref[0], k0_ref[0]], axis=0).astype(bf16)
    vv = jnp.concatenate([v2_ref[0], v1_ref[0], v0_ref[0]], axis=0).astype(bf16)
    n_keys = kk.shape[0]
    key_chunk = lax.broadcasted_iota(jnp.int32, (1, n_keys), 1) // CHUNK
    valid = (c - WIN_CHUNKS + key_chunk) >= 0
    q = q_ref[0]
    hd = SWA_HEAD_DIM
    for kv in range(SWA_KV_HEADS):
        kh = kk[:, kv * hd:(kv + 1) * hd]
        vh = vv[:, kv * hd:(kv + 1) * hd]
        qh = jnp.concatenate(
            [q[:, (kv * SWA_GROUP + g) * hd:(kv * SWA_GROUP + g + 1) * hd]
             for g in range(SWA_GROUP)], axis=0).astype(bf16)
        logits = lax.dot_general(qh, kh, _NT, preferred_element_type=jnp.float32)
        logits = logits * SWA_SCALE + bias_ref[kv]
        logits = jnp.where(valid, logits, NEG_INF)
        sink = sink_ref[kv]
        m = jnp.maximum(jnp.max(logits, axis=-1, keepdims=True), sink)
        e = jnp.exp(logits - m)
        p = e / (jnp.sum(e, axis=-1, keepdims=True) + jnp.exp(sink - m))
        o = jnp.dot(p.astype(bf16), vh, preferred_element_type=jnp.float32)
        for g in range(SWA_GROUP):
            col = (kv * SWA_GROUP + g) * hd
            o_ref[0, :, col:col + hd] = o[g * CHUNK:(g + 1) * CHUNK].astype(o_ref.dtype)


def swa(proj3, bias, sink_col, cache_k=None, cache_v=None):
    b, t, _ = proj3.shape
    nc = t // CHUNK
    assert t % CHUNK == 0
    qspec = pl.BlockSpec((1, CHUNK, SWA_WIDTH), lambda bi, c: (bi, c, OFF_QSW // SWA_WIDTH))

    def cur(off):
        return pl.BlockSpec((1, CHUNK, SWA_KV_WIDTH), lambda bi, c: (bi, c, off // SWA_KV_WIDTH))

    if cache_k is None:
        def prev(off, back):
            return pl.BlockSpec((1, CHUNK, SWA_KV_WIDTH),
                                lambda bi, c: (bi, jnp.maximum(c - back, 0), off // SWA_KV_WIDTH))
        kspecs = [prev(OFF_KSW, 2), prev(OFF_KSW, 1), cur(OFF_KSW)]
        vspecs = [prev(OFF_VSW, 2), prev(OFF_VSW, 1), cur(OFF_VSW)]
        karrs = [proj3, proj3, proj3]
        varrs = [proj3, proj3, proj3]
        c_off = 0
    else:
        assert nc == 1 and cache_k.shape[1] == WINDOW

        def past(j):
            return pl.BlockSpec((1, CHUNK, SWA_KV_WIDTH), lambda bi, c: (bi, j, 0))
        kspecs = [past(0), past(1), cur(OFF_KSW)]
        vspecs = [past(0), past(1), cur(OFF_VSW)]
        karrs = [cache_k, cache_k, proj3]
        varrs = [cache_v, cache_v, proj3]
        c_off = WIN_CHUNKS
    rows = SWA_GROUP * CHUNK
    n_keys = (WIN_CHUNKS + 1) * CHUNK
    return pl.pallas_call(
        functools.partial(_swa_kernel, c_off=c_off),
        grid=(b, nc),
        in_specs=[qspec] + kspecs + vspecs + [
            pl.BlockSpec((SWA_KV_HEADS, rows, n_keys), lambda bi, c: (0, 0, 0)),
            pl.BlockSpec((SWA_KV_HEADS, rows, 1), lambda bi, c: (0, 0, 0)),
        ],
        out_specs=pl.BlockSpec((1, CHUNK, SWA_WIDTH), lambda bi, c: (bi, c, 0)),
        out_shape=jax.ShapeDtypeStruct((b, t, SWA_WIDTH), jnp.bfloat16),
        compiler_params=_cparams(("parallel", "parallel")),
        name="swa",
    )(proj3, *karrs, *varrs, bias, sink_col)


def _t5_bias(rel, table):
    nb = REL_BUCKETS // 2
    max_exact = nb // 2
    side = jnp.where(rel > 0, nb, 0)
    n = jnp.abs(rel)
    n_f = jnp.maximum(n, max_exact).astype(jnp.float32)
    large = max_exact + (jnp.log(n_f / max_exact) / math.log(REL_MAX_DIST / max_exact)
                         * (nb - max_exact)).astype(jnp.int32)
    large = jnp.minimum(large, nb - 1)
    bucket = side + jnp.where(n < max_exact, n, large)
    return jnp.transpose(table[bucket].astype(jnp.float32), (2, 0, 1))


def _outproj_kernel(a1_ref, a2_ref, w_ref, x_ref, o_ref):
    k1 = a1_ref.shape[1]
    acc = jnp.dot(a1_ref[...], w_ref[0:k1, :], preferred_element_type=jnp.float32)
    acc += jnp.dot(a2_ref[...], w_ref[k1:, :], preferred_element_type=jnp.float32)
    o_ref[...] = x_ref[...] + acc


def outproj(a1, a2, w, x, tm=512, tn=512):
    n, d = x.shape
    k1, k2 = a1.shape[1], a2.shape[1]
    tm = min(tm, n)
    assert n % tm == 0 and d % tn == 0
    return pl.pallas_call(
        _outproj_kernel,
        grid=(n // tm, d // tn),
        in_specs=[
            pl.BlockSpec((tm, k1), lambda i, j: (i, 0)),
            pl.BlockSpec((tm, k2), lambda i, j: (i, 0)),
            pl.BlockSpec((k1 + k2, tn), lambda i, j: (0, j)),
            pl.BlockSpec((tm, tn), lambda i, j: (i, j)),
        ],
        out_specs=pl.BlockSpec((tm, tn), lambda i, j: (i, j)),
        out_shape=jax.ShapeDtypeStruct((n, d), jnp.float32),
        compiler_params=_cparams(("parallel", "arbitrary")),
        name="outproj",
    )(a1, a2, w, x)


def _memattn_kernel(x_ref, nw_ref, wq_ref, mk_ref, mv_ref, wo_ref, o_ref):
    bf16 = jnp.bfloat16
    x = x_ref[0]
    h = _rms(x, nw_ref[...]).astype(bf16)
    q = jnp.dot(h, wq_ref[...], preferred_element_type=jnp.float32)
    mk = mk_ref[0].astype(bf16)
    mv = mv_ref[0].astype(bf16)
    outs = []
    for hh in range(MEM_HEADS):
        sl = slice(hh * MEM_HEAD_DIM, (hh + 1) * MEM_HEAD_DIM)
        logits = lax.dot_general(q[:, sl].astype(bf16), mk[:, sl], _NT,
                                 preferred_element_type=jnp.float32) * MEM_SCALE
        m = jnp.max(logits, axis=-1, keepdims=True)
        e = jnp.exp(logits - m)
        p = e / jnp.sum(e, axis=-1, keepdims=True)
        outs.append(jnp.dot(p.astype(bf16), mv[:, sl], preferred_element_type=jnp.float32))
    o = jnp.concatenate(outs, axis=-1).astype(bf16)
    o_ref[0] = x + jnp.dot(o, wo_ref[...], preferred_element_type=jnp.float32)


def memattn(x3, nw, wq, mk_arr, mv_arr, mk_col, mv_col, wo, tm=512):
    b, t, d = x3.shape
    tm = min(tm, t)
    assert t % tm == 0
    return pl.pallas_call(
        _memattn_kernel,
        grid=(b, t // tm),
        in_specs=[
            pl.BlockSpec((1, tm, d), lambda bi, i: (bi, i, 0)),
            pl.BlockSpec((1, d), lambda bi, i: (0, 0)),
            pl.BlockSpec((d, MEM_INNER), lambda bi, i: (0, 0)),
            pl.BlockSpec((1, N_MEM, MEM_INNER), lambda bi, i: (bi, 0, mk_col)),
            pl.BlockSpec((1, N_MEM, MEM_INNER), lambda bi, i: (bi, 0, mv_col)),
            pl.BlockSpec((MEM_INNER, d), lambda bi, i: (0, 0)),
        ],
        out_specs=pl.BlockSpec((1, tm, d), lambda bi, i: (bi, i, 0)),
        out_shape=jax.ShapeDtypeStruct((b, t, d), jnp.float32),
        compiler_params=_cparams(("parallel", "arbitrary")),
        name="memattn",
    )(x3, nw.reshape(1, d), wq, mk_arr, mv_arr, wo)


def _topk_rows(s, k, payload=None):
    n = s.shape[0]
    row = lax.broadcasted_iota(jnp.int32, s.shape, 0)
    vals, picks = [], []
    for _ in range(k):
        m = jnp.max(s, axis=0, keepdims=True)
        idx = jnp.min(jnp.where(s == m, row, n), axis=0, keepdims=True)
        hit = row == idx
        vals.append(m)
        if payload is None:
            picks.append(idx)
        else:
            picks.append(jnp.sum(jnp.where(hit, payload, 0), axis=0, keepdims=True))
        s = jnp.where(hit, -jnp.inf, s)
    return jnp.concatenate(vals, axis=0), jnp.concatenate(picks, axis=0)


def _peer_retrieve_kernel(q_ref, sk_ref, eidx_ref, gate_ref):
    bf16 = jnp.bfloat16
    k = PEER_TOPK
    eids, gates = [], []
    for h in range(PEER_HEADS):
        tops = []
        for c in range(2):
            col = (h * 2 + c) * PEER_DHALF
            qh = q_ref[:, col:col + PEER_DHALF].astype(bf16)
            st = lax.dot_general(sk_ref[c, h], qh, _NT,
                                 preferred_element_type=jnp.float32)
            tops.append(_topk_rows(st, k))
        (s1, i1), (s2, i2) = tops
        cand = jnp.concatenate([s1[a:a + 1] + s2 for a in range(k)], axis=0)
        cidx = jnp.concatenate([i1[a:a + 1] * PEER_NKEYS + i2 for a in range(k)], axis=0)
        top_s, eid = _topk_rows(cand, k, payload=cidx)
        e = jnp.exp(top_s - top_s[0:1])
        gates.append(e / jnp.sum(e, axis=0, keepdims=True))
        eids.append(eid)
    eidx_ref[...] = jnp.concatenate(eids, axis=0).T
    gate_ref[...] = jnp.concatenate(gates, axis=0).T


def peer_retrieve(qp, subkeys, tn=128):
    n, d = qp.shape
    tn = min(tn, n)
    assert n % tn == 0
    return pl.pallas_call(
        _peer_retrieve_kernel,
        grid=(n // tn,),
        in_specs=[
            pl.BlockSpec((tn, d), lambda i: (i, 0)),
            pl.BlockSpec(subkeys.shape, lambda i: (0, 0, 0, 0)),
        ],
        out_specs=[
            pl.BlockSpec((tn, PEER_PAIRS), lambda i: (i, 0)),
            pl.BlockSpec((tn, PEER_PAIRS), lambda i: (i, 0)),
        ],
        out_shape=[
            jax.ShapeDtypeStruct((n, PEER_PAIRS), jnp.int32),
            jax.ShapeDtypeStruct((n, PEER_PAIRS), jnp.float32),
        ],
        compiler_params=_cparams(("parallel",)),
        name="peer_retrieve",
    )(qp, subkeys)


PEER_SLOTS = 4
PEER_AHEAD = PEER_SLOTS - 1
PEER_LANES = 128
PEER_EROWS = 32
PEER_PITCH = 40


def _peer_main_kernel(eidx_ref, gate_ref, x_ref, nw_ref, fw_ref, uv_hbm, o_ref, h_ref, *scratch):
    bf16 = jnp.bfloat16
    bufs, sem = scratch[:PEER_SLOTS], scratch[PEER_SLOTS]
    tb, d = x_ref.shape
    h_ref[...] = _rms(x_ref[...], nw_ref[...])

    rows = PEER_PAIRS * PEER_EROWS

    def issue(t, slot):
        for p in range(PEER_PAIRS):
            r = pl.multiple_of(eidx_ref[t, p] * PEER_EROWS, PEER_EROWS)
            pltpu.make_async_copy(uv_hbm.at[pl.ds(r, PEER_EROWS)],
                                  bufs[slot].at[pl.ds(p * PEER_PITCH, PEER_EROWS)],
                                  sem.at[slot]).start()

    def wait(slot):
        pltpu.make_async_copy(uv_hbm.at[pl.ds(0, rows)], bufs[slot].at[pl.ds(0, rows)],
                              sem.at[slot]).wait()

    def gathered(slot, j0):
        return jnp.concatenate(
            [bufs[slot][pl.ds(j0 + j, PEER_PAIRS, stride=PEER_PITCH), :]
             for j in range(d // PEER_LANES)], axis=1)

    def compute(t, slot):
        hb = jnp.broadcast_to(h_ref[pl.ds(t, 1), :], (8, d)).astype(bf16)
        act = lax.dot_general(hb, gathered(slot, 0).astype(bf16), _NT,
                              preferred_element_type=jnp.float32)[0:1]
        gelu = 0.5 * act * (1.0 + lax.erf(act * np.float32(math.sqrt(0.5))))
        w = gate_ref[pl.ds(t, 1), :] * gelu
        wb = jnp.broadcast_to(w, (8, PEER_PAIRS)).astype(bf16)
        y = jnp.dot(wb, gathered(slot, d // PEER_LANES).astype(bf16),
                    preferred_element_type=jnp.float32)[0:1]
        o_ref[pl.ds(t, 1), :] = _rms(x_ref[pl.ds(t, 1), :] + y, fw_ref[...])

    for t in range(PEER_AHEAD):
        issue(t, t % PEER_SLOTS)

    def body(i, carry):
        t0 = i * PEER_SLOTS
        for j in range(PEER_SLOTS):
            wait(j)
            compute(t0 + j, j)
            issue(t0 + j + PEER_AHEAD, (j + PEER_AHEAD) % PEER_SLOTS)
        return carry

    n_main = (tb - PEER_AHEAD) // PEER_SLOTS
    lax.fori_loop(0, n_main, body, 0)
    for t in range(n_main * PEER_SLOTS, tb):
        wait(t % PEER_SLOTS)
        compute(t, t % PEER_SLOTS)
        if t + PEER_AHEAD < tb:
            issue(t + PEER_AHEAD, (t + PEER_AHEAD) % PEER_SLOTS)


def make_peer_table(u_tab, v_tab):
    uv = jnp.concatenate([u_tab, v_tab], axis=1)
    return uv.reshape(u_tab.shape[0] * PEER_EROWS, PEER_LANES)


def peer_main(eidx, gate, x, nw, fw, uv_tab, tb=64):
    n, d = x.shape
    tb = min(tb, n)
    assert n % tb == 0 and tb > PEER_AHEAD + PEER_SLOTS
    assert uv_tab.shape[1] == PEER_LANES and 2 * d == PEER_EROWS * PEER_LANES
    return pl.pallas_call(
        _peer_main_kernel,
        grid=(n // tb,),
        in_specs=[
            pl.BlockSpec((tb, PEER_PAIRS), lambda i: (i, 0), memory_space=pltpu.SMEM),
            pl.BlockSpec((tb, PEER_PAIRS), lambda i: (i, 0)),
            pl.BlockSpec((tb, d), lambda i: (i, 0)),
            pl.BlockSpec((1, d), lambda i: (0, 0)),
            pl.BlockSpec((1, d), lambda i: (0, 0)),
            pl.BlockSpec(memory_space=pl.ANY),
        ],
        out_specs=pl.BlockSpec((tb, d), lambda i: (i, 0)),
        out_shape=jax.ShapeDtypeStruct((n, d), jnp.float32),
        scratch_shapes=[
            pltpu.VMEM((tb, d), jnp.float32),
            *[pltpu.VMEM((PEER_PAIRS * PEER_PITCH, PEER_LANES), jnp.float32)
              for _ in range(PEER_SLOTS)],
            pltpu.SemaphoreType.DMA((PEER_SLOTS,)),
        ],
        compiler_params=_cparams(("arbitrary",)),
        name="peer_main",
    )(eidx, gate, x, nw.reshape(1, d), fw.reshape(1, d), uv_tab)


def _trunk(x, mk_arr, mv_arr, mk_col, mv_col, s0, cache_k, cache_v, p):
    b, t, d = x.shape
    n = b * t
    proj = norm_matmul(x.reshape(n, d), p["norm_mix_w"], p["w_in"])
    proj3 = proj.reshape(b, t, IN_WIDTH)
    o_hg, s_fin = hgrn(proj3, s0, p["lb"], p["hg_norm_w"])
    o_sw = swa(proj3, p["bias"], p["sink_col"], cache_k, cache_v)
    x1 = outproj(o_hg.reshape(n, HG_WIDTH), o_sw.reshape(n, SWA_WIDTH), p["w_out"], x.reshape(n, d))
    x2 = memattn(x1.reshape(b, t, d), p["norm_cross_w"], p["mem_wq"], mk_arr, mv_arr,
                 mk_col, mv_col, p["mem_wo"]).reshape(n, d)
    qp = norm_matmul(x2, p["norm_ffn_w"], p["peer_wq"])
    eidx, gate = peer_retrieve(qp, p["peer_subkeys"])
    y = peer_main(eidx, gate, x2, p["norm_ffn_w"], p["final_norm_w"], p["peer_uv"])
    keep = min(WINDOW, t) if cache_k is None else t
    k_rows = proj3[:, t - keep:, OFF_KSW:OFF_KSW + SWA_KV_WIDTH]
    v_rows = proj3[:, t - keep:, OFF_VSW:OFF_VSW + SWA_KV_WIDTH]
    k_rows = k_rows.reshape(b, keep, SWA_KV_HEADS, SWA_HEAD_DIM)
    v_rows = v_rows.reshape(b, keep, SWA_KV_HEADS, SWA_HEAD_DIM)
    return y.reshape(b, t, d), s_fin, k_rows, v_rows


def kernel(x_prompt, x_sample, mem_prompt, state_hgrn, cache_swa_k, cache_swa_v, cache_mem_k, cache_mem_v, rel_bias, hg_lb_logits, norm_mix_w, w_in, hg_norm_w, swa_sinks, w_out, norm_mem_w, norm_cross_w, mem_wq, mem_wk, mem_wv, mem_wo, norm_ffn_w, peer_wq, peer_subkeys, peer_u, peer_v, final_norm_w):
    bf16 = jnp.bfloat16
    depth = w_in.shape[0]
    assert depth == 1
    l = 0
    lb_all = jnp.cumsum(jax.nn.softmax(hg_lb_logits.astype(jnp.float32), axis=0), axis=0)
    n_keys = (WIN_CHUNKS + 1) * CHUNK
    rel = jnp.arange(n_keys)[None, :] - (WIN_CHUNKS * CHUNK + jnp.arange(CHUNK))[:, None]
    bias = _t5_bias(rel, rel_bias).reshape(SWA_KV_HEADS, SWA_GROUP * CHUNK, n_keys)
    sink_col = jnp.broadcast_to(
        swa_sinks[l].astype(jnp.float32).reshape(SWA_KV_HEADS, SWA_GROUP, 1),
        (SWA_KV_HEADS, SWA_GROUP, CHUNK)).reshape(SWA_KV_HEADS, SWA_GROUP * CHUNK, 1)
    p = {
        "norm_mix_w": norm_mix_w[l], "w_in": w_in[l].astype(bf16), "lb": lb_all[l],
        "hg_norm_w": hg_norm_w[l], "bias": bias, "sink_col": sink_col,
        "w_out": w_out[l].astype(bf16), "norm_cross_w": norm_cross_w[l],
        "mem_wq": mem_wq[l].astype(bf16), "mem_wo": mem_wo[l].astype(bf16),
        "norm_ffn_w": norm_ffn_w[l], "peer_wq": peer_wq[l].astype(bf16),
        "peer_subkeys": peer_subkeys[l].astype(bf16), "peer_uv": make_peer_table(peer_u[l], peer_v[l]),
        "final_norm_w": final_norm_w,
    }
    bp, tp, d = x_prompt.shape
    bs, ts, _ = x_sample.shape

    wkv = jnp.concatenate([mem_wk[l], mem_wv[l]], axis=1).astype(bf16)
    kv = norm_matmul(mem_prompt.reshape(bp * N_MEM, d), norm_mem_w[l], wkv)
    kv3 = kv.reshape(bp, N_MEM, 2 * MEM_INNER)
    mk = kv3[:, :, :MEM_INNER].reshape(bp, N_MEM, MEM_HEADS, MEM_HEAD_DIM)
    mv = kv3[:, :, MEM_INNER:].reshape(bp, N_MEM, MEM_HEADS, MEM_HEAD_DIM)

    s0 = jnp.zeros((bp, HG_HEADS, HG_DK, HG_DV), jnp.float32)
    yp, sp, kp, vp = _trunk(x_prompt, kv3, kv3, 0, 1, s0, None, None, p)

    cmk = cache_mem_k[l].reshape(bs, N_MEM, MEM_INNER)
    cmv = cache_mem_v[l].reshape(bs, N_MEM, MEM_INNER)
    ck = cache_swa_k[l].reshape(bs, -1, SWA_KV_WIDTH)
    cv = cache_swa_v[l].reshape(bs, -1, SWA_KV_WIDTH)
    ys, ss, ks_new, vs_new = _trunk(x_sample, cmk, cmv, 0, 0, state_hgrn[l], ck, cv, p)

    return (yp, ys, sp[None], kp[None], vp[None], mk[None], mv[None],
            ss[None], ks_new[None], vs_new[None])
```

```python
import functools
import math

import numpy as np
import jax
import jax.numpy as jnp
from jax import lax
from jax.experimental import pallas as pl
from jax.experimental.pallas import tpu as pltpu

EPS = 1e-6
NEG_INF = -1e30
CHUNK = 64

HG_HEADS = 8
HG_DK = 128
HG_DV = 128
HG_WIDTH = HG_HEADS * HG_DV
HG_SUB = 16
HG_CHUNK = 64

SWA_HEADS = 16
SWA_KV_HEADS = 4
SWA_GROUP = SWA_HEADS // SWA_KV_HEADS
SWA_HEAD_DIM = 64
SWA_WIDTH = SWA_HEADS * SWA_HEAD_DIM
SWA_KV_WIDTH = SWA_KV_HEADS * SWA_HEAD_DIM
SWA_SCALE = SWA_HEAD_DIM ** -0.5
WINDOW = 128
WIN_CHUNKS = WINDOW // CHUNK
REL_BUCKETS = 32
REL_MAX_DIST = 128

N_MEM = 256
MEM_HEADS = 4
MEM_HEAD_DIM = 128
MEM_INNER = MEM_HEADS * MEM_HEAD_DIM
MEM_SCALE = MEM_HEAD_DIM ** -0.5

PEER_HEADS = 8
PEER_NKEYS = 128
PEER_DHALF = 128
PEER_TOPK = 16
PEER_PAIRS = PEER_HEADS * PEER_TOPK

OFF_Q, OFF_F, OFF_I, OFF_G = 0, 1024, 2048, 3072
OFF_QSW, OFF_KSW, OFF_VSW = 4096, 5120, 5376
IN_WIDTH = 5632

VMEM_LIMIT_BYTES = 56 * 1024 * 1024

_NT = (((1,), (1,)), ((), ()))
_TN = (((0,), (0,)), ((), ()))


def _cparams(sem):
    return pltpu.CompilerParams(dimension_semantics=sem, vmem_limit_bytes=VMEM_LIMIT_BYTES)


def _rms(x, w):
    return x * lax.rsqrt(jnp.mean(x * x, axis=-1, keepdims=True) + EPS) * w


def _norm_matmul_kernel(x_ref, nw_ref, w_ref, o_ref, h_ref):
    @pl.when(pl.program_id(1) == 0)
    def _():
        h_ref[...] = _rms(x_ref[...], nw_ref[...]).astype(h_ref.dtype)

    o_ref[...] = jnp.dot(h_ref[...], w_ref[...], preferred_element_type=jnp.float32)


def norm_matmul(x, nw, w, tm=512, tn=512):
    n, d = x.shape
    m = w.shape[1]
    tm = min(tm, n)
    tn = min(tn, m)
    assert n % tm == 0 and m % tn == 0
    return pl.pallas_call(
        _norm_matmul_kernel,
        grid=(n // tm, m // tn),
        in_specs=[
            pl.BlockSpec((tm, d), lambda i, j: (i, 0)),
            pl.BlockSpec((1, d), lambda i, j: (0, 0)),
            pl.BlockSpec((d, tn), lambda i, j: (0, j)),
        ],
        out_specs=pl.BlockSpec((tm, tn), lambda i, j: (i, j)),
        out_shape=jax.ShapeDtypeStruct((n, m), jnp.float32),
        scratch_shapes=[pltpu.VMEM((tm, d), jnp.bfloat16)],
        compiler_params=_cparams(("parallel", "arbitrary")),
        name="norm_matmul",
    )(x, nw.reshape(1, d), w)


def _cumsum_rows(x):
    n = x.shape[0]
    row = lax.broadcasted_iota(jnp.int32, x.shape, 0)
    s = 1
    while s < n:
        x = x + jnp.where(row >= s, pltpu.roll(x, s, axis=0), 0.0)
        s *= 2
    return x


def _bcast_rows(x, idxs, g):
    return jnp.concatenate(
        [jnp.broadcast_to(x[i:i + 1, :], (g, x.shape[1])) for i in idxs], axis=0)


def _hgrn_chunk(q, fpre, v, lb, st):
    c = q.shape[0]
    f = lb + (1.0 - lb) * jax.nn.sigmoid(fpre)
    k = 1.0 - f
    lf = jnp.log(f)
    bc = _cumsum_rows(lf)
    be = bc - lf
    bf16 = jnp.bfloat16
    ti = lax.broadcasted_iota(jnp.int32, (c, c), 0)
    si = lax.broadcasted_iota(jnp.int32, (c, c), 1)

    ref0 = _bcast_rows(be, range(0, c, HG_SUB), HG_SUB)
    qd = (q * jnp.exp(bc - ref0)).astype(bf16)
    kd = (k * jnp.exp(ref0 - bc)).astype(bf16)
    a = lax.dot_general(qd, kd, _NT, preferred_element_type=jnp.float32)
    attn = jnp.where((ti // HG_SUB == si // HG_SUB) & (si <= ti), a, 0.0)
    g = 2 * HG_SUB
    while g <= c:
        half = g // 2
        ref = _bcast_rows(bc, range(half - 1, c, g), g)
        ql = (q * jnp.exp(jnp.minimum(bc - ref, 0.0))).astype(bf16)
        kl = (k * jnp.exp(jnp.minimum(ref - bc, 0.0))).astype(bf16)
        a = lax.dot_general(ql, kl, _NT, preferred_element_type=jnp.float32)
        m = (ti // g == si // g) & (ti % g >= half) & (si % g < half)
        attn = jnp.where(m, a, attn)
        g *= 2

    vb = v.astype(bf16)
    q_in = (q * jnp.exp(bc)).astype(bf16)
    o = (lax.dot_general(q_in, st.astype(bf16), _NT, preferred_element_type=jnp.float32)
         + jnp.dot(attn.astype(bf16), vb, preferred_element_type=jnp.float32))
    b_last = bc[c - 1:c, :]
    k_out = (k * jnp.exp(b_last - bc)).astype(bf16)
    st_new = st * jnp.exp(b_last) + lax.dot_general(vb, k_out, _TN,
                                                    preferred_element_type=jnp.float32)
    return o, st_new


def _hgrn_kernel(q_ref, f_ref, i_ref, g_ref, s0_ref, lb_ref, nw_ref, o_ref, sfin_ref, st_ref):
    ci = pl.program_id(2)

    @pl.when(ci == 0)
    def _():
        st_ref[...] = s0_ref[0, 0].T

    ct = q_ref.shape[1]
    lb = lb_ref[...]
    nw = nw_ref[...]
    for j in range(ct // HG_CHUNK):
        sl = pl.ds(j * HG_CHUNK, HG_CHUNK)
        o, st_new = _hgrn_chunk(q_ref[0, sl, :], f_ref[0, sl, :], i_ref[0, sl, :], lb, st_ref[...])
        st_ref[...] = st_new
        o = _rms(o, nw)
        o_ref[0, sl, :] = (o * jax.nn.silu(g_ref[0, sl, :])).astype(o_ref.dtype)

    @pl.when(ci == pl.num_programs(2) - 1)
    def _():
        sfin_ref[0, 0] = st_ref[...].T


def hgrn(proj3, s0, lb, nw):
    b, t, _ = proj3.shape
    ct = min(256, t)
    assert t % ct == 0 and ct % HG_CHUNK == 0

    def col(off):
        return pl.BlockSpec((1, ct, HG_DK), lambda bi, h, c: (bi, c, off // HG_DK + h))

    return pl.pallas_call(
        _hgrn_kernel,
        grid=(b, HG_HEADS, t // ct),
        in_specs=[
            col(OFF_Q), col(OFF_F), col(OFF_I), col(OFF_G),
            pl.BlockSpec((1, 1, HG_DK, HG_DV), lambda bi, h, c: (bi, h, 0, 0)),
            pl.BlockSpec((1, HG_DK), lambda bi, h, c: (0, h)),
            pl.BlockSpec((1, HG_DV), lambda bi, h, c: (0, 0)),
        ],
        out_specs=[
            pl.BlockSpec((1, ct, HG_DV), lambda bi, h, c: (bi, c, h)),
            pl.BlockSpec((1, 1, HG_DK, HG_DV), lambda bi, h, c: (bi, h, 0, 0)),
        ],
        out_shape=[
            jax.ShapeDtypeStruct((b, t, HG_WIDTH), jnp.bfloat16),
            jax.ShapeDtypeStruct((b, HG_HEADS, HG_DK, HG_DV), jnp.float32),
        ],
        scratch_shapes=[pltpu.VMEM((HG_DV, HG_DK), jnp.float32)],
        compiler_params=_cparams(("parallel", "parallel", "arbitrary")),
        name="hgrn",
    )(proj3, proj3, proj3, proj3, s0, lb.reshape(1, -1), nw.reshape(1, -1))


def _swa_kernel(q_ref, k2_ref, k1_ref, k0_ref, v2_ref, v1_ref, v0_ref, bias_ref, sink_ref,
                o_ref, *, c_off):
    bf16 = jnp.bfloat16
    c = pl.program_id(1) + c_off
    kk = jnp.concatenate([k2_ref[0], k1_ref[0], k0_ref[0]], axis=0).astype(bf16)
    vv = jnp.concatenate([v2_ref[0], v1_ref[0], v0_ref[0]], axis=0).astype(bf16)
    n_keys = kk.shape[0]
    key_chunk = lax.broadcasted_iota(jnp.int32, (1, n_keys), 1) // CHUNK
    valid = (c - WIN_CHUNKS + key_chunk) >= 0
    q = q_ref[0]
    hd = SWA_HEAD_DIM
    for kv in range(SWA_KV_HEADS):
        kh = kk[:, kv * hd:(kv + 1) * hd]
        vh = vv[:, kv * hd:(kv + 1) * hd]
        qh = jnp.concatenate(
            [q[:, (kv * SWA_GROUP + g) * hd:(kv * SWA_GROUP + g + 1) * hd]
             for g in range(SWA_GROUP)], axis=0).astype(bf16)
        logits = lax.dot_general(qh, kh, _NT, preferred_element_type=jnp.float32)
        logits = logits * SWA_SCALE + bias_ref[kv]
        logits = jnp.where(valid, logits, NEG_INF)
        sink = sink_ref[kv]
        m = jnp.maximum(jnp.max(logits, axis=-1, keepdims=True), sink)
        e = jnp.exp(logits - m)
        p = e / (jnp.sum(e, axis=-1, keepdims=True) + jnp.exp(sink - m))
        o = jnp.dot(p.astype(bf16), vh, preferred_element_type=jnp.float32)
        for g in range(SWA_GROUP):
            col = (kv * SWA_GROUP + g) * hd
            o_ref[0, :, col:col + hd] = o[g * CHUNK:(g + 1) * CHUNK].astype(o_ref.dtype)


def swa(proj3, bias, sink_col, cache_k=None, cache_v=None):
    b, t, _ = proj3.shape
    nc = t // CHUNK
    assert t % CHUNK == 0
    qspec = pl.BlockSpec((1, CHUNK, SWA_WIDTH), lambda bi, c: (bi, c, OFF_QSW // SWA_WIDTH))

    def cur(off):
        return pl.BlockSpec((1, CHUNK, SWA_KV_WIDTH), lambda bi, c: (bi, c, off // SWA_KV_WIDTH))

    if cache_k is None:
        def prev(off, back):
            return pl.BlockSpec((1, CHUNK, SWA_KV_WIDTH),
                                lambda bi, c: (bi, jnp.maximum(c - back, 0), off // SWA_KV_WIDTH))
        kspecs = [prev(OFF_KSW, 2), prev(OFF_KSW, 1), cur(OFF_KSW)]
        vspecs = [prev(OFF_VSW, 2), prev(OFF_VSW, 1), cur(OFF_VSW)]
        karrs = [proj3, proj3, proj3]
        varrs = [proj3, proj3, proj3]
        c_off = 0
    else:
        assert nc == 1 and cache_k.shape[1] == WINDOW

        def past(j):
            return pl.BlockSpec((1, CHUNK, SWA_KV_WIDTH), lambda bi, c: (bi, j, 0))
        kspecs = [past(0), past(1), cur(OFF_KSW)]
        vspecs = [past(0), past(1), cur(OFF_VSW)]
        karrs = [cache_k, cache_k, proj3]
        varrs = [cache_v, cache_v, proj3]
        c_off = WIN_CHUNKS
    rows = SWA_GROUP * CHUNK
    n_keys = (WIN_CHUNKS + 1) * CHUNK
    return pl.pallas_call(
        functools.partial(_swa_kernel, c_off=c_off),
        grid=(b, nc),
        in_specs=[qspec] + kspecs + vspecs + [
            pl.BlockSpec((SWA_KV_HEADS, rows, n_keys), lambda bi, c: (0, 0, 0)),
            pl.BlockSpec((SWA_KV_HEADS, rows, 1), lambda bi, c: (0, 0, 0)),
        ],
        out_specs=pl.BlockSpec((1, CHUNK, SWA_WIDTH), lambda bi, c: (bi, c, 0)),
        out_shape=jax.ShapeDtypeStruct((b, t, SWA_WIDTH), jnp.bfloat16),
        compiler_params=_cparams(("parallel", "parallel")),
        name="swa",
    )(proj3, *karrs, *varrs, bias, sink_col)


def _t5_bias(rel, table):
    nb = REL_BUCKETS // 2
    max_exact = nb // 2
    side = jnp.where(rel > 0, nb, 0)
    n = jnp.abs(rel)
    n_f = jnp.maximum(n, max_exact).astype(jnp.float32)
    large = max_exact + (jnp.log(n_f / max_exact) / math.log(REL_MAX_DIST / max_exact)
                         * (nb - max_exact)).astype(jnp.int32)
    large = jnp.minimum(large, nb - 1)
    bucket = side + jnp.where(n < max_exact, n, large)
    return jnp.transpose(table[bucket].astype(jnp.float32), (2, 0, 1))


def _outproj_kernel(a1_ref, a2_ref, w_ref, x_ref, o_ref):
    k1 = a1_ref.shape[1]
    acc = jnp.dot(a1_ref[...], w_ref[0:k1, :], preferred_element_type=jnp.float32)
    acc += jnp.dot(a2_ref[...], w_ref[k1:, :], preferred_element_type=jnp.float32)
    o_ref[...] = x_ref[...] + acc


def outproj(a1, a2, w, x, tm=512, tn=512):
    n, d = x.shape
    k1, k2 = a1.shape[1], a2.shape[1]
    tm = min(tm, n)
    assert n % tm == 0 and d % tn == 0
    return pl.pallas_call(
        _outproj_kernel,
        grid=(n // tm, d // tn),
        in_specs=[
            pl.BlockSpec((tm, k1), lambda i, j: (i, 0)),
            pl.BlockSpec((tm, k2), lambda i, j: (i, 0)),
            pl.BlockSpec((k1 + k2, tn), lambda i, j: (0, j)),
            pl.BlockSpec((tm, tn), lambda i, j: (i, j)),
        ],
        out_specs=pl.BlockSpec((tm, tn), lambda i, j: (i, j)),
        out_shape=jax.ShapeDtypeStruct((n, d), jnp.float32),
        compiler_params=_cparams(("parallel", "arbitrary")),
        name="outproj",
    )(a1, a2, w, x)


def _memattn_kernel(x_ref, nw_ref, wq_ref, mk_ref, mv_ref, wo_ref, o_ref):
    bf16 = jnp.bfloat16
    x = x_ref[0]
    h = _rms(x, nw_ref[...]).astype(bf16)
    q = jnp.dot(h, wq_ref[...], preferred_element_type=jnp.float32)
    mk = mk_ref[0].astype(bf16)
    mv = mv_ref[0].astype(bf16)
    outs = []
    for hh in range(MEM_HEADS):
        sl = slice(hh * MEM_HEAD_DIM, (hh + 1) * MEM_HEAD_DIM)
        logits = lax.dot_general(q[:, sl].astype(bf16), mk[:, sl], _NT,
                                 preferred_element_type=jnp.float32) * MEM_SCALE
        m = jnp.max(logits, axis=-1, keepdims=True)
        e = jnp.exp(logits - m)
        p = e / jnp.sum(e, axis=-1, keepdims=True)
        outs.append(jnp.dot(p.astype(bf16), mv[:, sl], preferred_element_type=jnp.float32))
    o = jnp.concatenate(outs, axis=-1).astype(bf16)
    o_ref[0] = x + jnp.dot(o, wo_ref[...], preferred_element_type=jnp.float32)


def memattn(x3, nw, wq, mk_arr, mv_arr, mk_col, mv_col, wo, tm=512):
    b, t, d = x3.shape
    tm = min(tm, t)
    assert t % tm == 0
    return pl.pallas_call(
        _memattn_kernel,
        grid=(b, t // tm),
        in_specs=[
            pl.BlockSpec((1, tm, d), lambda bi, i: (bi, i, 0)),
            pl.BlockSpec((1, d), lambda bi, i: (0, 0)),
            pl.BlockSpec((d, MEM_INNER), lambda bi, i: (0, 0)),
            pl.BlockSpec((1, N_MEM, MEM_INNER), lambda bi, i: (bi, 0, mk_col)),
            pl.BlockSpec((1, N_MEM, MEM_INNER), lambda bi, i: (bi, 0, mv_col)),
            pl.BlockSpec((MEM_INNER, d), lambda bi, i: (0, 0)),
        ],
        out_specs=pl.BlockSpec((1, tm, d), lambda bi, i: (bi, i, 0)),
        out_shape=jax.ShapeDtypeStruct((b, t, d), jnp.float32),
        compiler_params=_cparams(("parallel", "arbitrary")),
        name="memattn",
    )(x3, nw.reshape(1, d), wq, mk_arr, mv_arr, wo)


def _topk_rows(s, k, payload=None):
    n = s.shape[0]
    row = lax.broadcasted_iota(jnp.int32, s.shape, 0)
    vals, picks = [], []
    for _ in range(k):
        m = jnp.max(s, axis=0, keepdims=True)
        idx = jnp.min(jnp.where(s == m, row, n), axis=0, keepdims=True)
        hit = row == idx
        vals.append(m)
        if payload is None:
            picks.append(idx)
        else:
            picks.append(jnp.sum(jnp.where(hit, payload, 0), axis=0, keepdims=True))
        s = jnp.where(hit, -jnp.inf, s)
    return jnp.concatenate(vals, axis=0), jnp.concatenate(picks, axis=0)


def _peer_retrieve_kernel(q_ref, sk_ref, eidx_ref, gate_ref):
    bf16 = jnp.bfloat16
    k = PEER_TOPK
    eids, gates = [], []
    for h in range(PEER_HEADS):
        tops = []
        for c in range(2):
            col = (h * 2 + c) * PEER_DHALF
            qh = q_ref[:, col:col + PEER_DHALF].astype(bf16)
            st = lax.dot_general(sk_ref[c, h], qh, _NT,
                                 preferred_element_type=jnp.float32)
            tops.append(_topk_rows(st, k))
        (s1, i1), (s2, i2) = tops
        cand = jnp.concatenate([s1[a:a + 1] + s2 for a in range(k)], axis=0)
        cidx = jnp.concatenate([i1[a:a + 1] * PEER_NKEYS + i2 for a in range(k)], axis=0)
        top_s, eid = _topk_rows(cand, k, payload=cidx)
        e = jnp.exp(top_s - top_s[0:1])
        gates.append(e / jnp.sum(e, axis=0, keepdims=True))
        eids.append(eid)
    eidx_ref[...] = jnp.concatenate(eids, axis=0).T
    gate_ref[...] = jnp.concatenate(gates, axis=0).T


def peer_retrieve(qp, subkeys, tn=128):
    n, d = qp.shape
    tn = min(tn, n)
    assert n % tn == 0
    return pl.pallas_call(
        _peer_retrieve_kernel,
        grid=(n // tn,),
        in_specs=[
            pl.BlockSpec((tn, d), lambda i: (i, 0)),
            pl.BlockSpec(subkeys.shape, lambda i: (0, 0, 0, 0)),
        ],
        out_specs=[
            pl.BlockSpec((tn, PEER_PAIRS), lambda i: (i, 0)),
            pl.BlockSpec((tn, PEER_PAIRS), lambda i: (i, 0)),
        ],
        out_shape=[
            jax.ShapeDtypeStruct((n, PEER_PAIRS), jnp.int32),
            jax.ShapeDtypeStruct((n, PEER_PAIRS), jnp.float32),
        ],
        compiler_params=_cparams(("parallel",)),
        name="peer_retrieve",
    )(qp, subkeys)


PEER_SLOTS = 4
PEER_AHEAD = PEER_SLOTS - 1
PEER_LANES = 128
PEER_EROWS = 16
PEER_PITCH = 24


def _even_odd_chunks(a):
    nc = a.shape[1] // PEER_LANES
    pick = lambda c: a[:, c * PEER_LANES:(c + 1) * PEER_LANES]
    return jnp.concatenate([pick(c) for c in range(0, nc, 2)] + [pick(c) for c in range(1, nc, 2)],
                           axis=1)


def _interleave_chunks(a):
    nc = a.shape[1] // PEER_LANES
    half = nc // 2
    pick = lambda c: a[:, c * PEER_LANES:(c + 1) * PEER_LANES]
    return jnp.concatenate([pick(c // 2 + (half if c % 2 else 0)) for c in range(nc)], axis=1)


def _peer_main_kernel(eidx_ref, gate_ref, x_ref, nw_ref, fw_ref, uv_hbm, o_ref, h_ref, y_ref,
                      dup_ref, *scratch):
    bf16 = jnp.bfloat16
    f32 = jnp.float32
    bufs, sem = scratch[:PEER_SLOTS], scratch[PEER_SLOTS]
    tb, d = x_ref.shape
    half = d // 2
    n_rows = PEER_EROWS // 2
    lanes2 = 2 * PEER_PAIRS
    h_ref[...] = _even_odd_chunks(_rms(x_ref[...], nw_ref[...])).astype(bf16).astype(f32)
    dup_ref[...] = (lax.broadcasted_iota(jnp.int32, (PEER_PAIRS, lanes2), 1) // 2
                    == lax.broadcasted_iota(jnp.int32, (PEER_PAIRS, lanes2), 0)).astype(bf16)

    rows = PEER_PAIRS * PEER_EROWS

    def issue(t, slot):
        for p in range(PEER_PAIRS):
            r = pl.multiple_of(eidx_ref[t, p] * PEER_EROWS, PEER_EROWS)
            pltpu.make_async_copy(uv_hbm.at[pl.ds(r, PEER_EROWS)],
                                  bufs[slot].at[pl.ds(p * PEER_PITCH, PEER_EROWS)],
                                  sem.at[slot]).start()

    def wait(slot):
        pltpu.make_async_copy(uv_hbm.at[pl.ds(0, rows)], bufs[slot].at[pl.ds(0, rows)],
                              sem.at[slot]).wait()

    def words(slot, m):
        return bufs[slot][pl.ds(m, PEER_PAIRS, stride=PEER_PITCH), :]

    sub_w = lax.broadcasted_iota(jnp.int32, (8, lanes2), 0)
    par_w = lax.broadcasted_iota(jnp.int32, (8, lanes2), 1) % 2

    def compute(t, slot):
        hrow = h_ref[pl.ds(t, 1), :]
        acc = None
        for m in range(n_rows):
            wd = words(slot, m)
            lo = lax.bitcast_convert_type(wd << 16, f32)
            hi = lax.bitcast_convert_type(wd & jnp.int32(-65536), f32)
            term = (lo * hrow[:, m * PEER_LANES:(m + 1) * PEER_LANES]
                    + hi * hrow[:, half + m * PEER_LANES:half + (m + 1) * PEER_LANES])
            acc = term if acc is None else acc + term
        act = jnp.sum(acc.T, axis=0, keepdims=True)
        gelu = 0.5 * act * (1.0 + lax.erf(act * np.float32(math.sqrt(0.5))))
        w = gate_ref[pl.ds(t, 1), :] * gelu
        wb = jnp.broadcast_to(w, (8, PEER_PAIRS)).astype(bf16)
        wd2 = jnp.dot(wb, dup_ref[...], preferred_element_type=f32)
        w2 = jnp.where(par_w == sub_w, wd2, 0.0).astype(bf16)
        wv = jnp.concatenate([pltpu.bitcast(words(slot, n_rows + m), bf16)
                              for m in range(n_rows)], axis=1)
        y2 = jnp.dot(w2, wv, preferred_element_type=f32)
        y_ref[pl.ds(t, 1), :] = jnp.concatenate([y2[0:1], y2[1:2]], axis=1)

    for t in range(PEER_AHEAD):
        issue(t, t % PEER_SLOTS)

    def body(i, carry):
        t0 = i * PEER_SLOTS
        for j in range(PEER_SLOTS):
            wait(j)
            compute(t0 + j, j)
            issue(t0 + j + PEER_AHEAD, (j + PEER_AHEAD) % PEER_SLOTS)
        return carry

    n_main = (tb - PEER_AHEAD) // PEER_SLOTS
    lax.fori_loop(0, n_main, body, 0)
    for t in range(n_main * PEER_SLOTS, tb):
        wait(t % PEER_SLOTS)
        compute(t, t % PEER_SLOTS)
        if t + PEER_AHEAD < tb:
            issue(t + PEER_AHEAD, (t + PEER_AHEAD) % PEER_SLOTS)

    o_ref[...] = _rms(x_ref[...] + _interleave_chunks(y_ref[...]), fw_ref[...])


def make_peer_table(u_tab, v_tab):
    e = u_tab.shape[0]
    uv = jnp.concatenate([u_tab, v_tab], axis=1).astype(jnp.bfloat16).astype(jnp.float32)
    bits = lax.bitcast_convert_type(uv, jnp.uint32).reshape(e, PEER_EROWS, 2, PEER_LANES)
    word = (bits[:, :, 0, :] >> 16) | (bits[:, :, 1, :] & np.uint32(0xFFFF0000))
    return lax.bitcast_convert_type(word, jnp.int32).reshape(e * PEER_EROWS, PEER_LANES)


def peer_main(eidx, gate, x, nw, fw, uv_tab, tb=64):
    n, d = x.shape
    tb = min(tb, n)
    assert n % tb == 0 and tb > PEER_AHEAD + PEER_SLOTS
    assert uv_tab.shape[1] == PEER_LANES and d == PEER_EROWS * PEER_LANES
    return pl.pallas_call(
        _peer_main_kernel,
        grid=(n // tb,),
        in_specs=[
            pl.BlockSpec((tb, PEER_PAIRS), lambda i: (i, 0), memory_space=pltpu.SMEM),
            pl.BlockSpec((tb, PEER_PAIRS), lambda i: (i, 0)),
            pl.BlockSpec((tb, d), lambda i: (i, 0)),
            pl.BlockSpec((1, d), lambda i: (0, 0)),
            pl.BlockSpec((1, d), lambda i: (0, 0)),
            pl.BlockSpec(memory_space=pl.ANY),
        ],
        out_specs=pl.BlockSpec((tb, d), lambda i: (i, 0)),
        out_shape=jax.ShapeDtypeStruct((n, d), jnp.float32),
        scratch_shapes=[
            pltpu.VMEM((tb, d), jnp.float32),
            pltpu.VMEM((tb, d), jnp.float32),
            pltpu.VMEM((PEER_PAIRS, 2 * PEER_PAIRS), jnp.bfloat16),
            *[pltpu.VMEM((PEER_PAIRS * PEER_PITCH, PEER_LANES), jnp.int32)
              for _ in range(PEER_SLOTS)],
            pltpu.SemaphoreType.DMA((PEER_SLOTS,)),
        ],
        compiler_params=_cparams(("arbitrary",)),
        name="peer_main",
    )(eidx, gate, x, nw.reshape(1, d), fw.reshape(1, d), uv_tab)


def _trunk(x, mk_arr, mv_arr, mk_col, mv_col, s0, cache_k, cache_v, p):
    b, t, d = x.shape
    n = b * t
    proj = norm_matmul(x.reshape(n, d), p["norm_mix_w"], p["w_in"])
    proj3 = proj.reshape(b, t, IN_WIDTH)
    o_hg, s_fin = hgrn(proj3, s0, p["lb"], p["hg_norm_w"])
    o_sw = swa(proj3, p["bias"], p["sink_col"], cache_k, cache_v)
    x1 = outproj(o_hg.reshape(n, HG_WIDTH), o_sw.reshape(n, SWA_WIDTH), p["w_out"], x.reshape(n, d))
    x2 = memattn(x1.reshape(b, t, d), p["norm_cross_w"], p["mem_wq"], mk_arr, mv_arr,
                 mk_col, mv_col, p["mem_wo"]).reshape(n, d)
    qp = norm_matmul(x2, p["norm_ffn_w"], p["peer_wq"])
    eidx, gate = peer_retrieve(qp, p["peer_subkeys"])
    y = peer_main(eidx, gate, x2, p["norm_ffn_w"], p["final_norm_w"], p["peer_uv"])
    keep = min(WINDOW, t) if cache_k is None else t
    k_rows = proj3[:, t - keep:, OFF_KSW:OFF_KSW + SWA_KV_WIDTH]
    v_rows = proj3[:, t - keep:, OFF_VSW:OFF_VSW + SWA_KV_WIDTH]
    k_rows = k_rows.reshape(b, keep, SWA_KV_HEADS, SWA_HEAD_DIM)
    v_rows = v_rows.reshape(b, keep, SWA_KV_HEADS, SWA_HEAD_DIM)
    return y.reshape(b, t, d), s_fin, k_rows, v_rows


def kernel(x_prompt, x_sample, mem_prompt, state_hgrn, cache_swa_k, cache_swa_v, cache_mem_k, cache_mem_v, rel_bias, hg_lb_logits, norm_mix_w, w_in, hg_norm_w, swa_sinks, w_out, norm_mem_w, norm_cross_w, mem_wq, mem_wk, mem_wv, mem_wo, norm_ffn_w, peer_wq, peer_subkeys, peer_u, peer_v, final_norm_w):
    bf16 = jnp.bfloat16
    depth = w_in.shape[0]
    assert depth == 1
    l = 0
    lb_all = jnp.cumsum(jax.nn.softmax(hg_lb_logits.astype(jnp.float32), axis=0), axis=0)
    n_keys = (WIN_CHUNKS + 1) * CHUNK
    rel = jnp.arange(n_keys)[None, :] - (WIN_CHUNKS * CHUNK + jnp.arange(CHUNK))[:, None]
    bias = _t5_bias(rel, rel_bias).reshape(SWA_KV_HEADS, SWA_GROUP * CHUNK, n_keys)
    sink_col = jnp.broadcast_to(
        swa_sinks[l].astype(jnp.float32).reshape(SWA_KV_HEADS, SWA_GROUP, 1),
        (SWA_KV_HEADS, SWA_GROUP, CHUNK)).reshape(SWA_KV_HEADS, SWA_GROUP * CHUNK, 1)
    p = {
        "norm_mix_w": norm_mix_w[l], "w_in": w_in[l].astype(bf16), "lb": lb_all[l],
        "hg_norm_w": hg_norm_w[l], "bias": bias, "sink_col": sink_col,
        "w_out": w_out[l].astype(bf16), "norm_cross_w": norm_cross_w[l],
        "mem_wq": mem_wq[l].astype(bf16), "mem_wo": mem_wo[l].astype(bf16),
        "norm_ffn_w": norm_ffn_w[l], "peer_wq": peer_wq[l].astype(bf16),
        "peer_subkeys": peer_subkeys[l].astype(bf16), "peer_uv": make_peer_table(peer_u[l], peer_v[l]),
        "final_norm_w": final_norm_w,
    }
    bp, tp, d = x_prompt.shape
    bs, ts, _ = x_sample.shape

    wkv = jnp.concatenate([mem_wk[l], mem_wv[l]], axis=1).astype(bf16)
    kv = norm_matmul(mem_prompt.reshape(bp * N_MEM, d), norm_mem_w[l], wkv)
    kv3 = kv.reshape(bp, N_MEM, 2 * MEM_INNER)
    mk = kv3[:, :, :MEM_INNER].reshape(bp, N_MEM, MEM_HEADS, MEM_HEAD_DIM)
    mv = kv3[:, :, MEM_INNER:].reshape(bp, N_MEM, MEM_HEADS, MEM_HEAD_DIM)

    s0 = jnp.zeros((bp, HG_HEADS, HG_DK, HG_DV), jnp.float32)
    yp, sp, kp, vp = _trunk(x_prompt, kv3, kv3, 0, 1, s0, None, None, p)

    cmk = cache_mem_k[l].reshape(bs, N_MEM, MEM_INNER)
    cmv = cache_mem_v[l].reshape(bs, N_MEM, MEM_INNER)
    ck = cache_swa_k[l].reshape(bs, -1, SWA_KV_WIDTH)
    cv = cache_swa_v[l].reshape(bs, -1, SWA_KV_WIDTH)
    ys, ss, ks_new, vs_new = _trunk(x_sample, cmk, cmv, 0, 0, state_hgrn[l], ck, cv, p)

    return (yp, ys, sp[None], kp[None], vp[None], mk[None], mv[None],
            ss[None], ks_new[None], vs_new[None])
```

```python
import functools
import math

import numpy as np
import jax
import jax.numpy as jnp
from jax import lax
from jax.experimental import pallas as pl
from jax.experimental.pallas import tpu as pltpu

EPS = 1e-6
NEG_INF = -1e30
CHUNK = 64

HG_HEADS = 8
HG_DK = 128
HG_DV = 128
HG_WIDTH = HG_HEADS * HG_DV
HG_SUB = 16
HG_CHUNK = 64

SWA_HEADS = 16
SWA_KV_HEADS = 4
SWA_GROUP = SWA_HEADS // SWA_KV_HEADS
SWA_HEAD_DIM = 64
SWA_WIDTH = SWA_HEADS * SWA_HEAD_DIM
SWA_KV_WIDTH = SWA_KV_HEADS * SWA_HEAD_DIM
SWA_SCALE = SWA_HEAD_DIM ** -0.5
WINDOW = 128
WIN_CHUNKS = WINDOW // CHUNK
REL_BUCKETS = 32
REL_MAX_DIST = 128

N_MEM = 256
MEM_HEADS = 4
MEM_HEAD_DIM = 128
MEM_INNER = MEM_HEADS * MEM_HEAD_DIM
MEM_SCALE = MEM_HEAD_DIM ** -0.5

PEER_HEADS = 8
PEER_NKEYS = 128
PEER_DHALF = 128
PEER_TOPK = 16
PEER_PAIRS = PEER_HEADS * PEER_TOPK

OFF_Q, OFF_F, OFF_I, OFF_G = 0, 1024, 2048, 3072
OFF_QSW, OFF_KSW, OFF_VSW = 4096, 5120, 5376
IN_WIDTH = 5632

VMEM_LIMIT_BYTES = 56 * 1024 * 1024

_NT = (((1,), (1,)), ((), ()))
_TN = (((0,), (0,)), ((), ()))


def _cparams(sem):
    return pltpu.CompilerParams(dimension_semantics=sem, vmem_limit_bytes=VMEM_LIMIT_BYTES)


def _rms(x, w):
    return x * lax.rsqrt(jnp.mean(x * x, axis=-1, keepdims=True) + EPS) * w


def _norm_matmul_kernel(x_ref, nw_ref, w_ref, o_ref, h_ref):
    @pl.when(pl.program_id(1) == 0)
    def _():
        h_ref[...] = _rms(x_ref[...], nw_ref[...]).astype(h_ref.dtype)

    o_ref[...] = jnp.dot(h_ref[...], w_ref[...], preferred_element_type=jnp.float32)


def norm_matmul(x, nw, w, tm=512, tn=512):
    n, d = x.shape
    m = w.shape[1]
    tm = min(tm, n)
    tn = min(tn, m)
    assert n % tm == 0 and m % tn == 0
    return pl.pallas_call(
        _norm_matmul_kernel,
        grid=(n // tm, m // tn),
        in_specs=[
            pl.BlockSpec((tm, d), lambda i, j: (i, 0)),
            pl.BlockSpec((1, d), lambda i, j: (0, 0)),
            pl.BlockSpec((d, tn), lambda i, j: (0, j)),
        ],
        out_specs=pl.BlockSpec((tm, tn), lambda i, j: (i, j)),
        out_shape=jax.ShapeDtypeStruct((n, m), jnp.float32),
        scratch_shapes=[pltpu.VMEM((tm, d), jnp.bfloat16)],
        compiler_params=_cparams(("parallel", "arbitrary")),
        name="norm_matmul",
    )(x, nw.reshape(1, d), w)


def _cumsum_rows(x):
    n = x.shape[0]
    row = lax.broadcasted_iota(jnp.int32, x.shape, 0)
    s = 1
    while s < n:
        x = x + jnp.where(row >= s, pltpu.roll(x, s, axis=0), 0.0)
        s *= 2
    return x


def _bcast_rows(x, idxs, g):
    return jnp.concatenate(
        [jnp.broadcast_to(x[i:i + 1, :], (g, x.shape[1])) for i in idxs], axis=0)


def _hgrn_chunk(q, fpre, v, lb, st):
    c = q.shape[0]
    f = lb + (1.0 - lb) * jax.nn.sigmoid(fpre)
    k = 1.0 - f
    lf = jnp.log(f)
    bc = _cumsum_rows(lf)
    be = bc - lf
    bf16 = jnp.bfloat16
    ti = lax.broadcasted_iota(jnp.int32, (c, c), 0)
    si = lax.broadcasted_iota(jnp.int32, (c, c), 1)

    ref0 = _bcast_rows(be, range(0, c, HG_SUB), HG_SUB)
    qd = (q * jnp.exp(bc - ref0)).astype(bf16)
    kd = (k * jnp.exp(ref0 - bc)).astype(bf16)
    a = lax.dot_general(qd, kd, _NT, preferred_element_type=jnp.float32)
    attn = jnp.where((ti // HG_SUB == si // HG_SUB) & (si <= ti), a, 0.0)
    g = 2 * HG_SUB
    while g <= c:
        half = g // 2
        ref = _bcast_rows(bc, range(half - 1, c, g), g)
        ql = (q * jnp.exp(jnp.minimum(bc - ref, 0.0))).astype(bf16)
        kl = (k * jnp.exp(jnp.minimum(ref - bc, 0.0))).astype(bf16)
        a = lax.dot_general(ql, kl, _NT, preferred_element_type=jnp.float32)
        m = (ti // g == si // g) & (ti % g >= half) & (si % g < half)
        attn = jnp.where(m, a, attn)
        g *= 2

    vb = v.astype(bf16)
    q_in = (q * jnp.exp(bc)).astype(bf16)
    o = (lax.dot_general(q_in, st.astype(bf16), _NT, preferred_element_type=jnp.float32)
         + jnp.dot(attn.astype(bf16), vb, preferred_element_type=jnp.float32))
    b_last = bc[c - 1:c, :]
    k_out = (k * jnp.exp(b_last - bc)).astype(bf16)
    st_new = st * jnp.exp(b_last) + lax.dot_general(vb, k_out, _TN,
                                                    preferred_element_type=jnp.float32)
    return o, st_new


def _hgrn_kernel(q_ref, f_ref, i_ref, g_ref, s0_ref, lb_ref, nw_ref, o_ref, sfin_ref, st_ref):
    ci = pl.program_id(2)

    @pl.when(ci == 0)
    def _():
        st_ref[...] = s0_ref[0, 0].T

    ct = q_ref.shape[1]
    lb = lb_ref[...]
    nw = nw_ref[...]
    for j in range(ct // HG_CHUNK):
        sl = pl.ds(j * HG_CHUNK, HG_CHUNK)
        o, st_new = _hgrn_chunk(q_ref[0, sl, :], f_ref[0, sl, :], i_ref[0, sl, :], lb, st_ref[...])
        st_ref[...] = st_new
        o = _rms(o, nw)
        o_ref[0, sl, :] = (o * jax.nn.silu(g_ref[0, sl, :])).astype(o_ref.dtype)

    @pl.when(ci == pl.num_programs(2) - 1)
    def _():
        sfin_ref[0, 0] = st_ref[...].T


def hgrn(proj3, s0, lb, nw):
    b, t, _ = proj3.shape
    ct = min(256, t)
    assert t % ct == 0 and ct % HG_CHUNK == 0

    def col(off):
        return pl.BlockSpec((1, ct, HG_DK), lambda bi, h, c: (bi, c, off // HG_DK + h))

    return pl.pallas_call(
        _hgrn_kernel,
        grid=(b, HG_HEADS, t // ct),
        in_specs=[
            col(OFF_Q), col(OFF_F), col(OFF_I), col(OFF_G),
            pl.BlockSpec((1, 1, HG_DK, HG_DV), lambda bi, h, c: (bi, h, 0, 0)),
            pl.BlockSpec((1, HG_DK), lambda bi, h, c: (0, h)),
            pl.BlockSpec((1, HG_DV), lambda bi, h, c: (0, 0)),
        ],
        out_specs=[
            pl.BlockSpec((1, ct, HG_DV), lambda bi, h, c: (bi, c, h)),
            pl.BlockSpec((1, 1, HG_DK, HG_DV), lambda bi, h, c: (bi, h, 0, 0)),
        ],
        out_shape=[
            jax.ShapeDtypeStruct((b, t, HG_WIDTH), jnp.bfloat16),
            jax.ShapeDtypeStruct((b, HG_HEADS, HG_DK, HG_DV), jnp.float32),
        ],
        scratch_shapes=[pltpu.VMEM((HG_DV, HG_DK), jnp.float32)],
        compiler_params=_cparams(("parallel", "parallel", "arbitrary")),
        name="hgrn",
    )(proj3, proj3, proj3, proj3, s0, lb.reshape(1, -1), nw.reshape(1, -1))


def _swa_kernel(q_ref, k2_ref, k1_ref, k0_ref, v2_ref, v1_ref, v0_ref, bias_ref, sink_ref,
                o_ref, *, c_off):
    bf16 = jnp.bfloat16
    c = pl.program_id(1) + c_off
    kk = jnp.concatenate([k2_ref[0], k1_ref[0], k0_ref[0]], axis=0).astype(bf16)
    vv = jnp.concatenate([v2_ref[0], v1_ref[0], v0_ref[0]], axis=0).astype(bf16)
    n_keys = kk.shape[0]
    key_chunk = lax.broadcasted_iota(jnp.int32, (1, n_keys), 1) // CHUNK
    valid = (c - WIN_CHUNKS + key_chunk) >= 0
    q = q_ref[0]
    hd = SWA_HEAD_DIM
    for kv in range(SWA_KV_HEADS):
        kh = kk[:, kv * hd:(kv + 1) * hd]
        vh = vv[:, kv * hd:(kv + 1) * hd]
        qh = jnp.concatenate(
            [q[:, (kv * SWA_GROUP + g) * hd:(kv * SWA_GROUP + g + 1) * hd]
             for g in range(SWA_GROUP)], axis=0).astype(bf16)
        logits = lax.dot_general(qh, kh, _NT, preferred_element_type=jnp.float32)
        logits = logits * SWA_SCALE + bias_ref[kv]
        logits = jnp.where(valid, logits, NEG_INF)
        sink = sink_ref[kv]
        m = jnp.maximum(jnp.max(logits, axis=-1, keepdims=True), sink)
        e = jnp.exp(logits - m)
        p = e / (jnp.sum(e, axis=-1, keepdims=True) + jnp.exp(sink - m))
        o = jnp.dot(p.astype(bf16), vh, preferred_element_type=jnp.float32)
        for g in range(SWA_GROUP):
            col = (kv * SWA_GROUP + g) * hd
            o_ref[0, :, col:col + hd] = o[g * CHUNK:(g + 1) * CHUNK].astype(o_ref.dtype)


def swa(proj3, bias, sink_col, cache_k=None, cache_v=None):
    b, t, _ = proj3.shape
    nc = t // CHUNK
    assert t % CHUNK == 0
    qspec = pl.BlockSpec((1, CHUNK, SWA_WIDTH), lambda bi, c: (bi, c, OFF_QSW // SWA_WIDTH))

    def cur(off):
        return pl.BlockSpec((1, CHUNK, SWA_KV_WIDTH), lambda bi, c: (bi, c, off // SWA_KV_WIDTH))

    if cache_k is None:
        def prev(off, back):
            return pl.BlockSpec((1, CHUNK, SWA_KV_WIDTH),
                                lambda bi, c: (bi, jnp.maximum(c - back, 0), off // SWA_KV_WIDTH))
        kspecs = [prev(OFF_KSW, 2), prev(OFF_KSW, 1), cur(OFF_KSW)]
        vspecs = [prev(OFF_VSW, 2), prev(OFF_VSW, 1), cur(OFF_VSW)]
        karrs = [proj3, proj3, proj3]
        varrs = [proj3, proj3, proj3]
        c_off = 0
    else:
        assert nc == 1 and cache_k.shape[1] == WINDOW

        def past(j):
            return pl.BlockSpec((1, CHUNK, SWA_KV_WIDTH), lambda bi, c: (bi, j, 0))
        kspecs = [past(0), past(1), cur(OFF_KSW)]
        vspecs = [past(0), past(1), cur(OFF_VSW)]
        karrs = [cache_k, cache_k, proj3]
        varrs = [cache_v, cache_v, proj3]
        c_off = WIN_CHUNKS
    rows = SWA_GROUP * CHUNK
    n_keys = (WIN_CHUNKS + 1) * CHUNK
    return pl.pallas_call(
        functools.partial(_swa_kernel, c_off=c_off),
        grid=(b, nc),
        in_specs=[qspec] + kspecs + vspecs + [
            pl.BlockSpec((SWA_KV_HEADS, rows, n_keys), lambda bi, c: (0, 0, 0)),
            pl.BlockSpec((SWA_KV_HEADS, rows, 1), lambda bi, c: (0, 0, 0)),
        ],
        out_specs=pl.BlockSpec((1, CHUNK, SWA_WIDTH), lambda bi, c: (bi, c, 0)),
        out_shape=jax.ShapeDtypeStruct((b, t, SWA_WIDTH), jnp.bfloat16),
        compiler_params=_cparams(("parallel", "parallel")),
        name="swa",
    )(proj3, *karrs, *varrs, bias, sink_col)


def _t5_bias(rel, table):
    nb = REL_BUCKETS // 2
    max_exact = nb // 2
    side = jnp.where(rel > 0, nb, 0)
    n = jnp.abs(rel)
    n_f = jnp.maximum(n, max_exact).astype(jnp.float32)
    large = max_exact + (jnp.log(n_f / max_exact) / math.log(REL_MAX_DIST / max_exact)
                         * (nb - max_exact)).astype(jnp.int32)
    large = jnp.minimum(large, nb - 1)
    bucket = side + jnp.where(n < max_exact, n, large)
    return jnp.transpose(table[bucket].astype(jnp.float32), (2, 0, 1))


def _outproj_kernel(a1_ref, a2_ref, w_ref, x_ref, o_ref):
    k1 = a1_ref.shape[1]
    acc = jnp.dot(a1_ref[...], w_ref[0:k1, :], preferred_element_type=jnp.float32)
    acc += jnp.dot(a2_ref[...], w_ref[k1:, :], preferred_element_type=jnp.float32)
    o_ref[...] = x_ref[...] + acc


def outproj(a1, a2, w, x, tm=512, tn=512):
    n, d = x.shape
    k1, k2 = a1.shape[1], a2.shape[1]
    tm = min(tm, n)
    assert n % tm == 0 and d % tn == 0
    return pl.pallas_call(
        _outproj_kernel,
        grid=(n // tm, d // tn),
        in_specs=[
            pl.BlockSpec((tm, k1), lambda i, j: (i, 0)),
            pl.BlockSpec((tm, k2), lambda i, j: (i, 0)),
            pl.BlockSpec((k1 + k2, tn), lambda i, j: (0, j)),
            pl.BlockSpec((tm, tn), lambda i, j: (i, j)),
        ],
        out_specs=pl.BlockSpec((tm, tn), lambda i, j: (i, j)),
        out_shape=jax.ShapeDtypeStruct((n, d), jnp.float32),
        compiler_params=_cparams(("parallel", "arbitrary")),
        name="outproj",
    )(a1, a2, w, x)


def _memattn_kernel(x_ref, nw_ref, wq_ref, mk_ref, mv_ref, wo_ref, o_ref):
    bf16 = jnp.bfloat16
    x = x_ref[0]
    h = _rms(x, nw_ref[...]).astype(bf16)
    q = jnp.dot(h, wq_ref[...], preferred_element_type=jnp.float32)
    mk = mk_ref[0].astype(bf16)
    mv = mv_ref[0].astype(bf16)
    outs = []
    for hh in range(MEM_HEADS):
        sl = slice(hh * MEM_HEAD_DIM, (hh + 1) * MEM_HEAD_DIM)
        logits = lax.dot_general(q[:, sl].astype(bf16), mk[:, sl], _NT,
                                 preferred_element_type=jnp.float32) * MEM_SCALE
        m = jnp.max(logits, axis=-1, keepdims=True)
        e = jnp.exp(logits - m)
        p = e / jnp.sum(e, axis=-1, keepdims=True)
        outs.append(jnp.dot(p.astype(bf16), mv[:, sl], preferred_element_type=jnp.float32))
    o = jnp.concatenate(outs, axis=-1).astype(bf16)
    o_ref[0] = x + jnp.dot(o, wo_ref[...], preferred_element_type=jnp.float32)


def memattn(x3, nw, wq, mk_arr, mv_arr, mk_col, mv_col, wo, tm=512):
    b, t, d = x3.shape
    tm = min(tm, t)
    assert t % tm == 0
    return pl.pallas_call(
        _memattn_kernel,
        grid=(b, t // tm),
        in_specs=[
            pl.BlockSpec((1, tm, d), lambda bi, i: (bi, i, 0)),
            pl.BlockSpec((1, d), lambda bi, i: (0, 0)),
            pl.BlockSpec((d, MEM_INNER), lambda bi, i: (0, 0)),
            pl.BlockSpec((1, N_MEM, MEM_INNER), lambda bi, i: (bi, 0, mk_col)),
            pl.BlockSpec((1, N_MEM, MEM_INNER), lambda bi, i: (bi, 0, mv_col)),
            pl.BlockSpec((MEM_INNER, d), lambda bi, i: (0, 0)),
        ],
        out_specs=pl.BlockSpec((1, tm, d), lambda bi, i: (bi, i, 0)),
        out_shape=jax.ShapeDtypeStruct((b, t, d), jnp.float32),
        compiler_params=_cparams(("parallel", "arbitrary")),
        name="memattn",
    )(x3, nw.reshape(1, d), wq, mk_arr, mv_arr, wo)


def _topk_rows(s, k, payload=None):
    n = s.shape[0]
    row = lax.broadcasted_iota(jnp.int32, s.shape, 0)
    vals, picks = [], []
    for _ in range(k):
        m = jnp.max(s, axis=0, keepdims=True)
        idx = jnp.min(jnp.where(s == m, row, n), axis=0, keepdims=True)
        hit = row == idx
        vals.append(m)
        if payload is None:
            picks.append(idx)
        else:
            picks.append(jnp.sum(jnp.where(hit, payload, 0), axis=0, keepdims=True))
        s = jnp.where(hit, -jnp.inf, s)
    return jnp.concatenate(vals, axis=0), jnp.concatenate(picks, axis=0)


def _peer_retrieve_kernel(q_ref, sk_ref, eidx_ref, gate_ref):
    bf16 = jnp.bfloat16
    k = PEER_TOPK
    eids, gates = [], []
    for h in range(PEER_HEADS):
        tops = []
        for c in range(2):
            col = (h * 2 + c) * PEER_DHALF
            qh = q_ref[:, col:col + PEER_DHALF].astype(bf16)
            st = lax.dot_general(sk_ref[c, h], qh, _NT,
                                 preferred_element_type=jnp.float32)
            tops.append(_topk_rows(st, k))
        (s1, i1), (s2, i2) = tops
        cand = jnp.concatenate([s1[a:a + 1] + s2 for a in range(k)], axis=0)
        cidx = jnp.concatenate([i1[a:a + 1] * PEER_NKEYS + i2 for a in range(k)], axis=0)
        top_s, eid = _topk_rows(cand, k, payload=cidx)
        e = jnp.exp(top_s - top_s[0:1])
        gates.append(e / jnp.sum(e, axis=0, keepdims=True))
        eids.append(eid)
    eidx_ref[...] = jnp.concatenate(eids, axis=0).T
    gate_ref[...] = jnp.concatenate(gates, axis=0).T


def peer_retrieve(qp, subkeys, tn=128):
    n, d = qp.shape
    tn = min(tn, n)
    assert n % tn == 0
    return pl.pallas_call(
        _peer_retrieve_kernel,
        grid=(n // tn,),
        in_specs=[
            pl.BlockSpec((tn, d), lambda i: (i, 0)),
            pl.BlockSpec(subkeys.shape, lambda i: (0, 0, 0, 0)),
        ],
        out_specs=[
            pl.BlockSpec((tn, PEER_PAIRS), lambda i: (i, 0)),
            pl.BlockSpec((tn, PEER_PAIRS), lambda i: (i, 0)),
        ],
        out_shape=[
            jax.ShapeDtypeStruct((n, PEER_PAIRS), jnp.int32),
            jax.ShapeDtypeStruct((n, PEER_PAIRS), jnp.float32),
        ],
        compiler_params=_cparams(("parallel",)),
        name="peer_retrieve",
    )(qp, subkeys)


PEER_SLOTS = 8
PEER_AHEAD = PEER_SLOTS - 2
PEER_LANES = 128
PEER_EROWS = 16
PEER_PITCH = 24


def _even_odd_chunks(a):
    nc = a.shape[1] // PEER_LANES
    pick = lambda c: a[:, c * PEER_LANES:(c + 1) * PEER_LANES]
    return jnp.concatenate([pick(c) for c in range(0, nc, 2)] + [pick(c) for c in range(1, nc, 2)],
                           axis=1)


def _interleave_chunks(a):
    nc = a.shape[1] // PEER_LANES
    half = nc // 2
    pick = lambda c: a[:, c * PEER_LANES:(c + 1) * PEER_LANES]
    return jnp.concatenate([pick(c // 2 + (half if c % 2 else 0)) for c in range(nc)], axis=1)


def _peer_main_kernel(eidx_ref, gate_ref, x_ref, nw_ref, fw_ref, uv_hbm, o_ref, h_ref, y_ref,
                      dup_ref, w2_ref, *scratch):
    bf16 = jnp.bfloat16
    f32 = jnp.float32
    bufs, sem = scratch[:PEER_SLOTS], scratch[PEER_SLOTS]
    tb, d = x_ref.shape
    half = d // 2
    n_rows = PEER_EROWS // 2
    lanes2 = 2 * PEER_PAIRS
    h_ref[...] = _even_odd_chunks(_rms(x_ref[...], nw_ref[...])).astype(bf16).astype(f32)
    dup_ref[...] = (lax.broadcasted_iota(jnp.int32, (PEER_PAIRS, lanes2), 1) // 2
                    == lax.broadcasted_iota(jnp.int32, (PEER_PAIRS, lanes2), 0)).astype(bf16)

    rows = PEER_PAIRS * PEER_EROWS

    def issue(t, slot, part=0, parts=1):
        n_p = PEER_PAIRS // parts
        for p in range(part * n_p, (part + 1) * n_p):
            r = pl.multiple_of(eidx_ref[t, p] * PEER_EROWS, PEER_EROWS)
            pltpu.make_async_copy(uv_hbm.at[pl.ds(r, PEER_EROWS)],
                                  bufs[slot].at[pl.ds(p * PEER_PITCH, PEER_EROWS)],
                                  sem.at[slot]).start()

    def wait(slot):
        pltpu.make_async_copy(uv_hbm.at[pl.ds(0, rows)], bufs[slot].at[pl.ds(0, rows)],
                              sem.at[slot]).wait()

    def words(slot, m):
        return bufs[slot][pl.ds(m, PEER_PAIRS, stride=PEER_PITCH), :]

    sub_w = lax.broadcasted_iota(jnp.int32, (8, lanes2), 0)
    par_w = lax.broadcasted_iota(jnp.int32, (8, lanes2), 1) % 2

    def u_side(t, slot, between=None):
        hrow = h_ref[pl.ds(t, 1), :]
        acc = None
        for m in range(n_rows):
            if between is not None:
                between(m)
            wd = words(slot, m)
            lo = lax.bitcast_convert_type(wd << 16, f32)
            hi = lax.bitcast_convert_type(wd & jnp.int32(-65536), f32)
            term = (lo * hrow[:, m * PEER_LANES:(m + 1) * PEER_LANES]
                    + hi * hrow[:, half + m * PEER_LANES:half + (m + 1) * PEER_LANES])
            acc = term if acc is None else acc + term
        act = jnp.sum(acc.T, axis=0, keepdims=True)
        gelu = 0.5 * act * (1.0 + lax.erf(act * np.float32(math.sqrt(0.5))))
        w = gate_ref[pl.ds(t, 1), :] * gelu
        wb = jnp.broadcast_to(w, (8, PEER_PAIRS)).astype(bf16)
        wd2 = jnp.dot(wb, dup_ref[...], preferred_element_type=f32)
        w2_ref[slot] = jnp.where(par_w == sub_w, wd2, 0.0)

    def v_side(t, slot):
        wv = jnp.concatenate([pltpu.bitcast(words(slot, n_rows + m), bf16)
                              for m in range(n_rows)], axis=1)
        y2 = jnp.dot(w2_ref[slot].astype(bf16), wv, preferred_element_type=f32)
        y_ref[pl.ds(t, 1), :] = jnp.concatenate([y2[0:1], y2[1:2]], axis=1)

    def step(t, j, do_issue=True):
        wait(j % PEER_SLOTS)
        spread = (lambda m: issue(t + PEER_AHEAD, (j + PEER_AHEAD) % PEER_SLOTS, m, n_rows)
                  ) if do_issue else None
        u_side(t, j % PEER_SLOTS, spread)
        v_side(t - 1, (j - 1) % PEER_SLOTS)

    for t in range(PEER_AHEAD):
        issue(t, t)
    wait(0)
    u_side(0, 0)
    issue(PEER_AHEAD, PEER_AHEAD)

    def body(i, carry):
        t0 = 1 + i * PEER_SLOTS
        for j in range(PEER_SLOTS):
            step(t0 + j, 1 + j)
        return carry

    n_main = (tb - PEER_AHEAD - 1) // PEER_SLOTS
    lax.fori_loop(0, n_main, body, 0)
    for t in range(1 + n_main * PEER_SLOTS, tb):
        step(t, t, do_issue=t + PEER_AHEAD < tb)
    v_side(tb - 1, (tb - 1) % PEER_SLOTS)

    o_ref[...] = _rms(x_ref[...] + _interleave_chunks(y_ref[...]), fw_ref[...])


def make_peer_table(u_tab, v_tab):
    e = u_tab.shape[0]

    def pack(tab):
        bits = lax.bitcast_convert_type(tab.astype(jnp.bfloat16).astype(jnp.float32), jnp.uint32)
        bits = bits.reshape(e, PEER_EROWS // 2, 2 * PEER_LANES)
        return (bits[:, :, :PEER_LANES] >> 16) | (bits[:, :, PEER_LANES:] & np.uint32(0xFFFF0000))

    word = jnp.concatenate([pack(u_tab), pack(v_tab)], axis=1)
    return lax.bitcast_convert_type(word, jnp.int32).reshape(e * PEER_EROWS, PEER_LANES)


def peer_main(eidx, gate, x, nw, fw, uv_tab, tb=128):
    n, d = x.shape
    tb = min(tb, n)
    assert n % tb == 0 and tb > PEER_AHEAD + PEER_SLOTS + 1
    assert uv_tab.shape[1] == PEER_LANES and d == PEER_EROWS * PEER_LANES
    return pl.pallas_call(
        _peer_main_kernel,
        grid=(n // tb,),
        in_specs=[
            pl.BlockSpec((tb, PEER_PAIRS), lambda i: (i, 0), memory_space=pltpu.SMEM),
            pl.BlockSpec((tb, PEER_PAIRS), lambda i: (i, 0)),
            pl.BlockSpec((tb, d), lambda i: (i, 0)),
            pl.BlockSpec((1, d), lambda i: (0, 0)),
            pl.BlockSpec((1, d), lambda i: (0, 0)),
            pl.BlockSpec(memory_space=pl.ANY),
        ],
        out_specs=pl.BlockSpec((tb, d), lambda i: (i, 0)),
        out_shape=jax.ShapeDtypeStruct((n, d), jnp.float32),
        scratch_shapes=[
            pltpu.VMEM((tb, d), jnp.float32),
            pltpu.VMEM((tb, d), jnp.float32),
            pltpu.VMEM((PEER_PAIRS, 2 * PEER_PAIRS), jnp.bfloat16),
            pltpu.VMEM((PEER_SLOTS, 8, 2 * PEER_PAIRS), jnp.float32),
            *[pltpu.VMEM((PEER_PAIRS * PEER_PITCH, PEER_LANES), jnp.int32)
              for _ in range(PEER_SLOTS)],
            pltpu.SemaphoreType.DMA((PEER_SLOTS,)),
        ],
        compiler_params=_cparams(("arbitrary",)),
        name="peer_main",
    )(eidx, gate, x, nw.reshape(1, d), fw.reshape(1, d), uv_tab)


def _trunk(x, mk_arr, mv_arr, mk_col, mv_col, s0, cache_k, cache_v, p):
    b, t, d = x.shape
    n = b * t
    proj = norm_matmul(x.reshape(n, d), p["norm_mix_w"], p["w_in"])
    proj3 = proj.reshape(b, t, IN_WIDTH)
    o_hg, s_fin = hgrn(proj3, s0, p["lb"], p["hg_norm_w"])
    o_sw = swa(proj3, p["bias"], p["sink_col"], cache_k, cache_v)
    x1 = outproj(o_hg.reshape(n, HG_WIDTH), o_sw.reshape(n, SWA_WIDTH), p["w_out"], x.reshape(n, d))
    x2 = memattn(x1.reshape(b, t, d), p["norm_cross_w"], p["mem_wq"], mk_arr, mv_arr,
                 mk_col, mv_col, p["mem_wo"]).reshape(n, d)
    qp = norm_matmul(x2, p["norm_ffn_w"], p["peer_wq"])
    eidx, gate = peer_retrieve(qp, p["peer_subkeys"])
    y = peer_main(eidx, gate, x2, p["norm_ffn_w"], p["final_norm_w"], p["peer_uv"])
    keep = min(WINDOW, t) if cache_k is None else t
    k_rows = proj3[:, t - keep:, OFF_KSW:OFF_KSW + SWA_KV_WIDTH]
    v_rows = proj3[:, t - keep:, OFF_VSW:OFF_VSW + SWA_KV_WIDTH]
    k_rows = k_rows.reshape(b, keep, SWA_KV_HEADS, SWA_HEAD_DIM)
    v_rows = v_rows.reshape(b, keep, SWA_KV_HEADS, SWA_HEAD_DIM)
    return y.reshape(b, t, d), s_fin, k_rows, v_rows


def kernel(x_prompt, x_sample, mem_prompt, state_hgrn, cache_swa_k, cache_swa_v, cache_mem_k, cache_mem_v, rel_bias, hg_lb_logits, norm_mix_w, w_in, hg_norm_w, swa_sinks, w_out, norm_mem_w, norm_cross_w, mem_wq, mem_wk, mem_wv, mem_wo, norm_ffn_w, peer_wq, peer_subkeys, peer_u, peer_v, final_norm_w):
    bf16 = jnp.bfloat16
    depth = w_in.shape[0]
    assert depth == 1
    l = 0
    lb_all = jnp.cumsum(jax.nn.softmax(hg_lb_logits.astype(jnp.float32), axis=0), axis=0)
    n_keys = (WIN_CHUNKS + 1) * CHUNK
    rel = jnp.arange(n_keys)[None, :] - (WIN_CHUNKS * CHUNK + jnp.arange(CHUNK))[:, None]
    bias = _t5_bias(rel, rel_bias).reshape(SWA_KV_HEADS, SWA_GROUP * CHUNK, n_keys)
    sink_col = jnp.broadcast_to(
        swa_sinks[l].astype(jnp.float32).reshape(SWA_KV_HEADS, SWA_GROUP, 1),
        (SWA_KV_HEADS, SWA_GROUP, CHUNK)).reshape(SWA_KV_HEADS, SWA_GROUP * CHUNK, 1)
    p = {
        "norm_mix_w": norm_mix_w[l], "w_in": w_in[l].astype(bf16), "lb": lb_all[l],
        "hg_norm_w": hg_norm_w[l], "bias": bias, "sink_col": sink_col,
        "w_out": w_out[l].astype(bf16), "norm_cross_w": norm_cross_w[l],
        "mem_wq": mem_wq[l].astype(bf16), "mem_wo": mem_wo[l].astype(bf16),
        "norm_ffn_w": norm_ffn_w[l], "peer_wq": peer_wq[l].astype(bf16),
        "peer_subkeys": peer_subkeys[l].astype(bf16), "peer_uv": make_peer_table(peer_u[l], peer_v[l]),
        "final_norm_w": final_norm_w,
    }
    bp, tp, d = x_prompt.shape
    bs, ts, _ = x_sample.shape

    wkv = jnp.concatenate([mem_wk[l], mem_wv[l]], axis=1).astype(bf16)
    kv = norm_matmul(mem_prompt.reshape(bp * N_MEM, d), norm_mem_w[l], wkv)
    kv3 = kv.reshape(bp, N_MEM, 2 * MEM_INNER)
    mk = kv3[:, :, :MEM_INNER].reshape(bp, N_MEM, MEM_HEADS, MEM_HEAD_DIM)
    mv = kv3[:, :, MEM_INNER:].reshape(bp, N_MEM, MEM_HEADS, MEM_HEAD_DIM)

    s0 = jnp.zeros((bp, HG_HEADS, HG_DK, HG_DV), jnp.float32)
    yp, sp, kp, vp = _trunk(x_prompt, kv3, kv3, 0, 1, s0, None, None, p)

    cmk = cache_mem_k[l].reshape(bs, N_MEM, MEM_INNER)
    cmv = cache_mem_v[l].reshape(bs, N_MEM, MEM_INNER)
    ck = cache_swa_k[l].reshape(bs, -1, SWA_KV_WIDTH)
    cv = cache_swa_v[l].reshape(bs, -1, SWA_KV_WIDTH)
    ys, ss, ks_new, vs_new = _trunk(x_sample, cmk, cmv, 0, 0, state_hgrn[l], ck, cv, p)

    return (yp, ys, sp[None], kp[None], vp[None], mk[None], mv[None],
            ss[None], ks_new[None], vs_new[None])
```

```python
import functools
import math

import numpy as np
import jax
import jax.numpy as jnp
from jax import lax
from jax.experimental import pallas as pl
from jax.experimental.pallas import tpu as pltpu

EPS = 1e-6
NEG_INF = -1e30
CHUNK = 64

HG_HEADS = 8
HG_DK = 128
HG_DV = 128
HG_WIDTH = HG_HEADS * HG_DV
HG_SUB = 16
HG_CHUNK = 64

SWA_HEADS = 16
SWA_KV_HEADS = 4
SWA_GROUP = SWA_HEADS // SWA_KV_HEADS
SWA_HEAD_DIM = 64
SWA_WIDTH = SWA_HEADS * SWA_HEAD_DIM
SWA_KV_WIDTH = SWA_KV_HEADS * SWA_HEAD_DIM
SWA_SCALE = SWA_HEAD_DIM ** -0.5
WINDOW = 128
WIN_CHUNKS = WINDOW // CHUNK
REL_BUCKETS = 32
REL_MAX_DIST = 128

N_MEM = 256
MEM_HEADS = 4
MEM_HEAD_DIM = 128
MEM_INNER = MEM_HEADS * MEM_HEAD_DIM
MEM_SCALE = MEM_HEAD_DIM ** -0.5

PEER_HEADS = 8
PEER_NKEYS = 128
PEER_DHALF = 128
PEER_TOPK = 16
PEER_PAIRS = PEER_HEADS * PEER_TOPK

OFF_Q, OFF_F, OFF_I, OFF_G = 0, 1024, 2048, 3072
OFF_QSW, OFF_KSW, OFF_VSW = 4096, 5120, 5376
IN_WIDTH = 5632

VMEM_LIMIT_BYTES = 56 * 1024 * 1024

_NT = (((1,), (1,)), ((), ()))
_TN = (((0,), (0,)), ((), ()))


def _cparams(sem):
    return pltpu.CompilerParams(dimension_semantics=sem, vmem_limit_bytes=VMEM_LIMIT_BYTES)


def _rms(x, w):
    return x * lax.rsqrt(jnp.mean(x * x, axis=-1, keepdims=True) + EPS) * w


def _norm_matmul_kernel(x_ref, nw_ref, w_ref, o_ref, h_ref):
    @pl.when(pl.program_id(1) == 0)
    def _():
        h_ref[...] = _rms(x_ref[...], nw_ref[...]).astype(h_ref.dtype)

    o_ref[...] = jnp.dot(h_ref[...], w_ref[...], preferred_element_type=jnp.float32)


def norm_matmul(x, nw, w, tm=512, tn=512):
    n, d = x.shape
    m = w.shape[1]
    tm = min(tm, n)
    tn = min(tn, m)
    assert n % tm == 0 and m % tn == 0
    return pl.pallas_call(
        _norm_matmul_kernel,
        grid=(n // tm, m // tn),
        in_specs=[
            pl.BlockSpec((tm, d), lambda i, j: (i, 0)),
            pl.BlockSpec((1, d), lambda i, j: (0, 0)),
            pl.BlockSpec((d, tn), lambda i, j: (0, j)),
        ],
        out_specs=pl.BlockSpec((tm, tn), lambda i, j: (i, j)),
        out_shape=jax.ShapeDtypeStruct((n, m), jnp.float32),
        scratch_shapes=[pltpu.VMEM((tm, d), jnp.bfloat16)],
        compiler_params=_cparams(("parallel", "arbitrary")),
        name="norm_matmul",
    )(x, nw.reshape(1, d), w)


def _cumsum_rows(x):
    n = x.shape[0]
    row = lax.broadcasted_iota(jnp.int32, x.shape, 0)
    s = 1
    while s < n:
        x = x + jnp.where(row >= s, pltpu.roll(x, s, axis=0), 0.0)
        s *= 2
    return x


def _bcast_rows(x, idxs, g):
    return jnp.concatenate(
        [jnp.broadcast_to(x[i:i + 1, :], (g, x.shape[1])) for i in idxs], axis=0)


def _hgrn_chunk(q, fpre, v, lb, st):
    c = q.shape[0]
    f = lb + (1.0 - lb) * jax.nn.sigmoid(fpre)
    k = 1.0 - f
    lf = jnp.log(f)
    bc = _cumsum_rows(lf)
    be = bc - lf
    bf16 = jnp.bfloat16
    ti = lax.broadcasted_iota(jnp.int32, (c, c), 0)
    si = lax.broadcasted_iota(jnp.int32, (c, c), 1)

    ref0 = _bcast_rows(be, range(0, c, HG_SUB), HG_SUB)
    qd = (q * jnp.exp(bc - ref0)).astype(bf16)
    kd = (k * jnp.exp(ref0 - bc)).astype(bf16)
    a = lax.dot_general(qd, kd, _NT, preferred_element_type=jnp.float32)
    attn = jnp.where((ti // HG_SUB == si // HG_SUB) & (si <= ti), a, 0.0)
    g = 2 * HG_SUB
    while g <= c:
        half = g // 2
        ref = _bcast_rows(bc, range(half - 1, c, g), g)
        ql = (q * jnp.exp(jnp.minimum(bc - ref, 0.0))).astype(bf16)
        kl = (k * jnp.exp(jnp.minimum(ref - bc, 0.0))).astype(bf16)
        a = lax.dot_general(ql, kl, _NT, preferred_element_type=jnp.float32)
        m = (ti // g == si // g) & (ti % g >= half) & (si % g < half)
        attn = jnp.where(m, a, attn)
        g *= 2

    vb = v.astype(bf16)
    q_in = (q * jnp.exp(bc)).astype(bf16)
    o = (lax.dot_general(q_in, st.astype(bf16), _NT, preferred_element_type=jnp.float32)
         + jnp.dot(attn.astype(bf16), vb, preferred_element_type=jnp.float32))
    b_last = bc[c - 1:c, :]
    k_out = (k * jnp.exp(b_last - bc)).astype(bf16)
    st_new = st * jnp.exp(b_last) + lax.dot_general(vb, k_out, _TN,
                                                    preferred_element_type=jnp.float32)
    return o, st_new


def _hgrn_kernel(q_ref, f_ref, i_ref, g_ref, s0_ref, lb_ref, nw_ref, o_ref, sfin_ref, st_ref):
    ci = pl.program_id(2)

    @pl.when(ci == 0)
    def _():
        st_ref[...] = s0_ref[0, 0].T

    ct = q_ref.shape[1]
    lb = lb_ref[...]
    nw = nw_ref[...]
    for j in range(ct // HG_CHUNK):
        sl = pl.ds(j * HG_CHUNK, HG_CHUNK)
        o, st_new = _hgrn_chunk(q_ref[0, sl, :], f_ref[0, sl, :], i_ref[0, sl, :], lb, st_ref[...])
        st_ref[...] = st_new
        o = _rms(o, nw)
        o_ref[0, sl, :] = (o * jax.nn.silu(g_ref[0, sl, :])).astype(o_ref.dtype)

    @pl.when(ci == pl.num_programs(2) - 1)
    def _():
        sfin_ref[0, 0] = st_ref[...].T


def hgrn(proj3, s0, lb, nw):
    b, t, _ = proj3.shape
    ct = min(256, t)
    assert t % ct == 0 and ct % HG_CHUNK == 0

    def col(off):
        return pl.BlockSpec((1, ct, HG_DK), lambda bi, h, c: (bi, c, off // HG_DK + h))

    return pl.pallas_call(
        _hgrn_kernel,
        grid=(b, HG_HEADS, t // ct),
        in_specs=[
            col(OFF_Q), col(OFF_F), col(OFF_I), col(OFF_G),
            pl.BlockSpec((1, 1, HG_DK, HG_DV), lambda bi, h, c: (bi, h, 0, 0)),
            pl.BlockSpec((1, HG_DK), lambda bi, h, c: (0, h)),
            pl.BlockSpec((1, HG_DV), lambda bi, h, c: (0, 0)),
        ],
        out_specs=[
            pl.BlockSpec((1, ct, HG_DV), lambda bi, h, c: (bi, c, h)),
            pl.BlockSpec((1, 1, HG_DK, HG_DV), lambda bi, h, c: (bi, h, 0, 0)),
        ],
        out_shape=[
            jax.ShapeDtypeStruct((b, t, HG_WIDTH), jnp.bfloat16),
            jax.ShapeDtypeStruct((b, HG_HEADS, HG_DK, HG_DV), jnp.float32),
        ],
        scratch_shapes=[pltpu.VMEM((HG_DV, HG_DK), jnp.float32)],
        compiler_params=_cparams(("parallel", "parallel", "arbitrary")),
        name="hgrn",
    )(proj3, proj3, proj3, proj3, s0, lb.reshape(1, -1), nw.reshape(1, -1))


def _swa_kernel(q_ref, k2_ref, k1_ref, k0_ref, v2_ref, v1_ref, v0_ref, bias_ref, sink_ref,
                o_ref, *, c_off):
    bf16 = jnp.bfloat16
    c = pl.program_id(1) + c_off
    kk = jnp.concatenate([k2_ref[0], k1_ref[0], k0_ref[0]], axis=0).astype(bf16)
    vv = jnp.concatenate([v2_ref[0], v1_ref[0], v0_ref[0]], axis=0).astype(bf16)
    n_keys = kk.shape[0]
    key_chunk = lax.broadcasted_iota(jnp.int32, (1, n_keys), 1) // CHUNK
    valid = (c - WIN_CHUNKS + key_chunk) >= 0
    q = q_ref[0]
    hd = SWA_HEAD_DIM
    for kv in range(SWA_KV_HEADS):
        kh = kk[:, kv * hd:(kv + 1) * hd]
        vh = vv[:, kv * hd:(kv + 1) * hd]
        qh = jnp.concatenate(
            [q[:, (kv * SWA_GROUP + g) * hd:(kv * SWA_GROUP + g + 1) * hd]
             for g in range(SWA_GROUP)], axis=0).astype(bf16)
        logits = lax.dot_general(qh, kh, _NT, preferred_element_type=jnp.float32)
        logits = logits * SWA_SCALE + bias_ref[kv]
        logits = jnp.where(valid, logits, NEG_INF)
        sink = sink_ref[kv]
        m = jnp.maximum(jnp.max(logits, axis=-1, keepdims=True), sink)
        e = jnp.exp(logits - m)
        p = e / (jnp.sum(e, axis=-1, keepdims=True) + jnp.exp(sink - m))
        o = jnp.dot(p.astype(bf16), vh, preferred_element_type=jnp.float32)
        for g in range(SWA_GROUP):
            col = (kv * SWA_GROUP + g) * hd
            o_ref[0, :, col:col + hd] = o[g * CHUNK:(g + 1) * CHUNK].astype(o_ref.dtype)


def swa(proj3, bias, sink_col, cache_k=None, cache_v=None):
    b, t, _ = proj3.shape
    nc = t // CHUNK
    assert t % CHUNK == 0
    qspec = pl.BlockSpec((1, CHUNK, SWA_WIDTH), lambda bi, c: (bi, c, OFF_QSW // SWA_WIDTH))

    def cur(off):
        return pl.BlockSpec((1, CHUNK, SWA_KV_WIDTH), lambda bi, c: (bi, c, off // SWA_KV_WIDTH))

    if cache_k is None:
        def prev(off, back):
            return pl.BlockSpec((1, CHUNK, SWA_KV_WIDTH),
                                lambda bi, c: (bi, jnp.maximum(c - back, 0), off // SWA_KV_WIDTH))
        kspecs = [prev(OFF_KSW, 2), prev(OFF_KSW, 1), cur(OFF_KSW)]
        vspecs = [prev(OFF_VSW, 2), prev(OFF_VSW, 1), cur(OFF_VSW)]
        karrs = [proj3, proj3, proj3]
        varrs = [proj3, proj3, proj3]
        c_off = 0
    else:
        assert nc == 1 and cache_k.shape[1] == WINDOW

        def past(j):
            return pl.BlockSpec((1, CHUNK, SWA_KV_WIDTH), lambda bi, c: (bi, j, 0))
        kspecs = [past(0), past(1), cur(OFF_KSW)]
        vspecs = [past(0), past(1), cur(OFF_VSW)]
        karrs = [cache_k, cache_k, proj3]
        varrs = [cache_v, cache_v, proj3]
        c_off = WIN_CHUNKS
    rows = SWA_GROUP * CHUNK
    n_keys = (WIN_CHUNKS + 1) * CHUNK
    return pl.pallas_call(
        functools.partial(_swa_kernel, c_off=c_off),
        grid=(b, nc),
        in_specs=[qspec] + kspecs + vspecs + [
            pl.BlockSpec((SWA_KV_HEADS, rows, n_keys), lambda bi, c: (0, 0, 0)),
            pl.BlockSpec((SWA_KV_HEADS, rows, 1), lambda bi, c: (0, 0, 0)),
        ],
        out_specs=pl.BlockSpec((1, CHUNK, SWA_WIDTH), lambda bi, c: (bi, c, 0)),
        out_shape=jax.ShapeDtypeStruct((b, t, SWA_WIDTH), jnp.bfloat16),
        compiler_params=_cparams(("parallel", "parallel")),
        name="swa",
    )(proj3, *karrs, *varrs, bias, sink_col)


def _t5_bias(rel, table):
    nb = REL_BUCKETS // 2
    max_exact = nb // 2
    side = jnp.where(rel > 0, nb, 0)
    n = jnp.abs(rel)
    n_f = jnp.maximum(n, max_exact).astype(jnp.float32)
    large = max_exact + (jnp.log(n_f / max_exact) / math.log(REL_MAX_DIST / max_exact)
                         * (nb - max_exact)).astype(jnp.int32)
    large = jnp.minimum(large, nb - 1)
    bucket = side + jnp.where(n < max_exact, n, large)
    return jnp.transpose(table[bucket].astype(jnp.float32), (2, 0, 1))


def _outproj_kernel(a1_ref, a2_ref, w_ref, x_ref, o_ref):
    k1 = a1_ref.shape[1]
    acc = jnp.dot(a1_ref[...], w_ref[0:k1, :], preferred_element_type=jnp.float32)
    acc += jnp.dot(a2_ref[...], w_ref[k1:, :], preferred_element_type=jnp.float32)
    o_ref[...] = x_ref[...] + acc


def outproj(a1, a2, w, x, tm=512, tn=512):
    n, d = x.shape
    k1, k2 = a1.shape[1], a2.shape[1]
    tm = min(tm, n)
    assert n % tm == 0 and d % tn == 0
    return pl.pallas_call(
        _outproj_kernel,
        grid=(n // tm, d // tn),
        in_specs=[
            pl.BlockSpec((tm, k1), lambda i, j: (i, 0)),
            pl.BlockSpec((tm, k2), lambda i, j: (i, 0)),
            pl.BlockSpec((k1 + k2, tn), lambda i, j: (0, j)),
            pl.BlockSpec((tm, tn), lambda i, j: (i, j)),
        ],
        out_specs=pl.BlockSpec((tm, tn), lambda i, j: (i, j)),
        out_shape=jax.ShapeDtypeStruct((n, d), jnp.float32),
        compiler_params=_cparams(("parallel", "arbitrary")),
        name="outproj",
    )(a1, a2, w, x)


def _memattn_kernel(x_ref, nw_ref, wq_ref, mk_ref, mv_ref, wo_ref, o_ref):
    bf16 = jnp.bfloat16
    x = x_ref[0]
    h = _rms(x, nw_ref[...]).astype(bf16)
    q = jnp.dot(h, wq_ref[...], preferred_element_type=jnp.float32)
    mk = mk_ref[0].astype(bf16)
    mv = mv_ref[0].astype(bf16)
    outs = []
    for hh in range(MEM_HEADS):
        sl = slice(hh * MEM_HEAD_DIM, (hh + 1) * MEM_HEAD_DIM)
        logits = lax.dot_general(q[:, sl].astype(bf16), mk[:, sl], _NT,
                                 preferred_element_type=jnp.float32) * MEM_SCALE
        m = jnp.max(logits, axis=-1, keepdims=True)
        e = jnp.exp(logits - m)
        p = e / jnp.sum(e, axis=-1, keepdims=True)
        outs.append(jnp.dot(p.astype(bf16), mv[:, sl], preferred_element_type=jnp.float32))
    o = jnp.concatenate(outs, axis=-1).astype(bf16)
    o_ref[0] = x + jnp.dot(o, wo_ref[...], preferred_element_type=jnp.float32)


def memattn(x3, nw, wq, mk_arr, mv_arr, mk_col, mv_col, wo, tm=512):
    b, t, d = x3.shape
    tm = min(tm, t)
    assert t % tm == 0
    return pl.pallas_call(
        _memattn_kernel,
        grid=(b, t // tm),
        in_specs=[
            pl.BlockSpec((1, tm, d), lambda bi, i: (bi, i, 0)),
            pl.BlockSpec((1, d), lambda bi, i: (0, 0)),
            pl.BlockSpec((d, MEM_INNER), lambda bi, i: (0, 0)),
            pl.BlockSpec((1, N_MEM, MEM_INNER), lambda bi, i: (bi, 0, mk_col)),
            pl.BlockSpec((1, N_MEM, MEM_INNER), lambda bi, i: (bi, 0, mv_col)),
            pl.BlockSpec((MEM_INNER, d), lambda bi, i: (0, 0)),
        ],
        out_specs=pl.BlockSpec((1, tm, d), lambda bi, i: (bi, i, 0)),
        out_shape=jax.ShapeDtypeStruct((b, t, d), jnp.float32),
        compiler_params=_cparams(("parallel", "arbitrary")),
        name="memattn",
    )(x3, nw.reshape(1, d), wq, mk_arr, mv_arr, wo)


def _topk_rows(s, k, payload=None):
    n = s.shape[0]
    row = lax.broadcasted_iota(jnp.int32, s.shape, 0)
    vals, picks = [], []
    for _ in range(k):
        m = jnp.max(s, axis=0, keepdims=True)
        idx = jnp.min(jnp.where(s == m, row, n), axis=0, keepdims=True)
        hit = row == idx
        vals.append(m)
        if payload is None:
            picks.append(idx)
        else:
            picks.append(jnp.sum(jnp.where(hit, payload, 0), axis=0, keepdims=True))
        s = jnp.where(hit, -jnp.inf, s)
    return jnp.concatenate(vals, axis=0), jnp.concatenate(picks, axis=0)


def _peer_retrieve_kernel(q_ref, sk_ref, eidx_ref, gate_ref):
    bf16 = jnp.bfloat16
    k = PEER_TOPK
    eids, gates = [], []
    for h in range(PEER_HEADS):
        tops = []
        for c in range(2):
            col = (h * 2 + c) * PEER_DHALF
            qh = q_ref[:, col:col + PEER_DHALF].astype(bf16)
            st = lax.dot_general(sk_ref[c, h], qh, _NT,
                                 preferred_element_type=jnp.float32)
            tops.append(_topk_rows(st, k))
        (s1, i1), (s2, i2) = tops
        hk = k // 2
        cand = jnp.concatenate(
            [s1[0:1] + s2] + [s1[a:a + 1] + s2[0:hk] for a in range(1, hk)] + [s1[hk:k] + s2[0:1]],
            axis=0)
        cidx = jnp.concatenate(
            [i1[0:1] * PEER_NKEYS + i2] + [i1[a:a + 1] * PEER_NKEYS + i2[0:hk] for a in range(1, hk)]
            + [i1[hk:k] * PEER_NKEYS + i2[0:1]], axis=0)
        top_s, eid = _topk_rows(cand, k, payload=cidx)
        e = jnp.exp(top_s - top_s[0:1])
        gates.append(e / jnp.sum(e, axis=0, keepdims=True))
        eids.append(eid)
    eidx_ref[...] = jnp.concatenate(eids, axis=0).T
    gate_ref[...] = jnp.concatenate(gates, axis=0).T


def peer_retrieve(qp, subkeys, tn=128):
    n, d = qp.shape
    tn = min(tn, n)
    assert n % tn == 0
    return pl.pallas_call(
        _peer_retrieve_kernel,
        grid=(n // tn,),
        in_specs=[
            pl.BlockSpec((tn, d), lambda i: (i, 0)),
            pl.BlockSpec(subkeys.shape, lambda i: (0, 0, 0, 0)),
        ],
        out_specs=[
            pl.BlockSpec((tn, PEER_PAIRS), lambda i: (i, 0)),
            pl.BlockSpec((tn, PEER_PAIRS), lambda i: (i, 0)),
        ],
        out_shape=[
            jax.ShapeDtypeStruct((n, PEER_PAIRS), jnp.int32),
            jax.ShapeDtypeStruct((n, PEER_PAIRS), jnp.float32),
        ],
        compiler_params=_cparams(("parallel",)),
        name="peer_retrieve",
    )(qp, subkeys)


PEER_SLOTS = 8
PEER_AHEAD = PEER_SLOTS - 2
PEER_DMA_QUEUES = 2
PEER_LANES = 128
PEER_EROWS = 16
PEER_PITCH = 24


def _even_odd_chunks(a):
    nc = a.shape[1] // PEER_LANES
    pick = lambda c: a[:, c * PEER_LANES:(c + 1) * PEER_LANES]
    return jnp.concatenate([pick(c) for c in range(0, nc, 2)] + [pick(c) for c in range(1, nc, 2)],
                           axis=1)


def _interleave_chunks(a):
    nc = a.shape[1] // PEER_LANES
    half = nc // 2
    pick = lambda c: a[:, c * PEER_LANES:(c + 1) * PEER_LANES]
    return jnp.concatenate([pick(c // 2 + (half if c % 2 else 0)) for c in range(nc)], axis=1)


def _peer_main_kernel(eidx_ref, gate_ref, x_ref, nw_ref, fw_ref, uv_hbm, o_ref, h_ref, y_ref,
                      dup_ref, w2_ref, *scratch):
    bf16 = jnp.bfloat16
    f32 = jnp.float32
    bufs, sem = scratch[:PEER_SLOTS], scratch[PEER_SLOTS]
    tb, d = x_ref.shape
    half = d // 2
    n_rows = PEER_EROWS // 2
    lanes2 = 2 * PEER_PAIRS
    h_ref[...] = _even_odd_chunks(_rms(x_ref[...], nw_ref[...])).astype(bf16).astype(f32)
    dup_ref[...] = (lax.broadcasted_iota(jnp.int32, (PEER_PAIRS, lanes2), 1) // 2
                    == lax.broadcasted_iota(jnp.int32, (PEER_PAIRS, lanes2), 0)).astype(bf16)

    rows = PEER_PAIRS * PEER_EROWS

    def issue(t, slot, part=0, parts=1):
        n_p = PEER_PAIRS // parts
        for p in range(part * n_p, (part + 1) * n_p):
            r = pl.multiple_of(eidx_ref[t, p] * PEER_EROWS, PEER_EROWS)
            pltpu.make_async_copy(uv_hbm.at[pl.ds(r, PEER_EROWS)],
                                  bufs[slot].at[pl.ds(p * PEER_PITCH, PEER_EROWS)],
                                  sem.at[slot]).start(priority=p % PEER_DMA_QUEUES)

    def wait(slot):
        pltpu.make_async_copy(uv_hbm.at[pl.ds(0, rows)], bufs[slot].at[pl.ds(0, rows)],
                              sem.at[slot]).wait()

    def words(slot, m):
        return bufs[slot][pl.ds(m, PEER_PAIRS, stride=PEER_PITCH), :]

    sub_w = lax.broadcasted_iota(jnp.int32, (8, lanes2), 0)
    par_w = lax.broadcasted_iota(jnp.int32, (8, lanes2), 1) % 2

    def u_side(t, slot, between=None):
        hrow = h_ref[pl.ds(t, 1), :]
        acc = None
        for m in range(n_rows):
            if between is not None:
                between(m)
            wd = words(slot, m)
            lo = lax.bitcast_convert_type(wd << 16, f32)
            hi = lax.bitcast_convert_type(wd & jnp.int32(-65536), f32)
            term = (lo * hrow[:, m * PEER_LANES:(m + 1) * PEER_LANES]
                    + hi * hrow[:, half + m * PEER_LANES:half + (m + 1) * PEER_LANES])
            acc = term if acc is None else acc + term
        act = jnp.sum(acc.T, axis=0, keepdims=True)
        gelu = 0.5 * act * (1.0 + lax.erf(act * np.float32(math.sqrt(0.5))))
        w = gate_ref[pl.ds(t, 1), :] * gelu
        wb = jnp.broadcast_to(w, (8, PEER_PAIRS)).astype(bf16)
        wd2 = jnp.dot(wb, dup_ref[...], preferred_element_type=f32)
        w2_ref[slot] = jnp.where(par_w == sub_w, wd2, 0.0)

    def v_side(t, slot):
        wv = jnp.concatenate([pltpu.bitcast(words(slot, n_rows + m), bf16)
                              for m in range(n_rows)], axis=1)
        y2 = jnp.dot(w2_ref[slot].astype(bf16), wv, preferred_element_type=f32)
        y_ref[pl.ds(t, 1), :] = jnp.concatenate([y2[0:1], y2[1:2]], axis=1)

    def step(t, j, do_issue=True):
        wait(j % PEER_SLOTS)
        spread = (lambda m: issue(t + PEER_AHEAD, (j + PEER_AHEAD) % PEER_SLOTS, m, n_rows)
                  ) if do_issue else None
        u_side(t, j % PEER_SLOTS, spread)
        v_side(t - 1, (j - 1) % PEER_SLOTS)

    for t in range(PEER_AHEAD):
        issue(t, t)
    wait(0)
    u_side(0, 0)
    issue(PEER_AHEAD, PEER_AHEAD)

    def body(i, carry):
        t0 = 1 + i * PEER_SLOTS
        for j in range(PEER_SLOTS):
            step(t0 + j, 1 + j)
        return carry

    n_main = (tb - PEER_AHEAD - 1) // PEER_SLOTS
    lax.fori_loop(0, n_main, body, 0)
    for t in range(1 + n_main * PEER_SLOTS, tb):
        step(t, t, do_issue=t + PEER_AHEAD < tb)
    v_side(tb - 1, (tb - 1) % PEER_SLOTS)

    o_ref[...] = _rms(x_ref[...] + _interleave_chunks(y_ref[...]), fw_ref[...])


def make_peer_table(u_tab, v_tab):
    e = u_tab.shape[0]

    def pack(tab):
        bits = lax.bitcast_convert_type(tab.astype(jnp.bfloat16).astype(jnp.float32), jnp.uint32)
        bits = bits.reshape(e, PEER_EROWS // 2, 2 * PEER_LANES)
        return (bits[:, :, :PEER_LANES] >> 16) | (bits[:, :, PEER_LANES:] & np.uint32(0xFFFF0000))

    word = jnp.concatenate([pack(u_tab), pack(v_tab)], axis=1)
    return lax.bitcast_convert_type(word, jnp.int32).reshape(e * PEER_EROWS, PEER_LANES)


def peer_main(eidx, gate, x, nw, fw, uv_tab, tb=128):
    n, d = x.shape
    tb = min(tb, n)
    assert n % tb == 0 and tb > PEER_AHEAD + PEER_SLOTS + 1
    assert uv_tab.shape[1] == PEER_LANES and d == PEER_EROWS * PEER_LANES
    return pl.pallas_call(
        _peer_main_kernel,
        grid=(n // tb,),
        in_specs=[
            pl.BlockSpec((tb, PEER_PAIRS), lambda i: (i, 0), memory_space=pltpu.SMEM),
            pl.BlockSpec((tb, PEER_PAIRS), lambda i: (i, 0)),
            pl.BlockSpec((tb, d), lambda i: (i, 0)),
            pl.BlockSpec((1, d), lambda i: (0, 0)),
            pl.BlockSpec((1, d), lambda i: (0, 0)),
            pl.BlockSpec(memory_space=pl.ANY),
        ],
        out_specs=pl.BlockSpec((tb, d), lambda i: (i, 0)),
        out_shape=jax.ShapeDtypeStruct((n, d), jnp.float32),
        scratch_shapes=[
            pltpu.VMEM((tb, d), jnp.float32),
            pltpu.VMEM((tb, d), jnp.float32),
            pltpu.VMEM((PEER_PAIRS, 2 * PEER_PAIRS), jnp.bfloat16),
            pltpu.VMEM((PEER_SLOTS, 8, 2 * PEER_PAIRS), jnp.float32),
            *[pltpu.VMEM((PEER_PAIRS * PEER_PITCH, PEER_LANES), jnp.int32)
              for _ in range(PEER_SLOTS)],
            pltpu.SemaphoreType.DMA((PEER_SLOTS,)),
        ],
        compiler_params=_cparams(("arbitrary",)),
        name="peer_main",
    )(eidx, gate, x, nw.reshape(1, d), fw.reshape(1, d), uv_tab)


def _trunk(x, mk_arr, mv_arr, mk_col, mv_col, s0, cache_k, cache_v, p):
    b, t, d = x.shape
    n = b * t
    proj = norm_matmul(x.reshape(n, d), p["norm_mix_w"], p["w_in"])
    proj3 = proj.reshape(b, t, IN_WIDTH)
    o_hg, s_fin = hgrn(proj3, s0, p["lb"], p["hg_norm_w"])
    o_sw = swa(proj3, p["bias"], p["sink_col"], cache_k, cache_v)
    x1 = outproj(o_hg.reshape(n, HG_WIDTH), o_sw.reshape(n, SWA_WIDTH), p["w_out"], x.reshape(n, d))
    x2 = memattn(x1.reshape(b, t, d), p["norm_cross_w"], p["mem_wq"], mk_arr, mv_arr,
                 mk_col, mv_col, p["mem_wo"]).reshape(n, d)
    qp = norm_matmul(x2, p["norm_ffn_w"], p["peer_wq"])
    eidx, gate = peer_retrieve(qp, p["peer_subkeys"])
    y = peer_main(eidx, gate, x2, p["norm_ffn_w"], p["final_norm_w"], p["peer_uv"])
    keep = min(WINDOW, t) if cache_k is None else t
    k_rows = proj3[:, t - keep:, OFF_KSW:OFF_KSW + SWA_KV_WIDTH]
    v_rows = proj3[:, t - keep:, OFF_VSW:OFF_VSW + SWA_KV_WIDTH]
    k_rows = k_rows.reshape(b, keep, SWA_KV_HEADS, SWA_HEAD_DIM)
    v_rows = v_rows.reshape(b, keep, SWA_KV_HEADS, SWA_HEAD_DIM)
    return y.reshape(b, t, d), s_fin, k_rows, v_rows


def kernel(x_prompt, x_sample, mem_prompt, state_hgrn, cache_swa_k, cache_swa_v, cache_mem_k, cache_mem_v, rel_bias, hg_lb_logits, norm_mix_w, w_in, hg_norm_w, swa_sinks, w_out, norm_mem_w, norm_cross_w, mem_wq, mem_wk, mem_wv, mem_wo, norm_ffn_w, peer_wq, peer_subkeys, peer_u, peer_v, final_norm_w):
    bf16 = jnp.bfloat16
    depth = w_in.shape[0]
    assert depth == 1
    l = 0
    lb_all = jnp.cumsum(jax.nn.softmax(hg_lb_logits.astype(jnp.float32), axis=0), axis=0)
    n_keys = (WIN_CHUNKS + 1) * CHUNK
    rel = jnp.arange(n_keys)[None, :] - (WIN_CHUNKS * CHUNK + jnp.arange(CHUNK))[:, None]
    bias = _t5_bias(rel, rel_bias).reshape(SWA_KV_HEADS, SWA_GROUP * CHUNK, n_keys)
    sink_col = jnp.broadcast_to(
        swa_sinks[l].astype(jnp.float32).reshape(SWA_KV_HEADS, SWA_GROUP, 1),
        (SWA_KV_HEADS, SWA_GROUP, CHUNK)).reshape(SWA_KV_HEADS, SWA_GROUP * CHUNK, 1)
    p = {
        "norm_mix_w": norm_mix_w[l], "w_in": w_in[l].astype(bf16), "lb": lb_all[l],
        "hg_norm_w": hg_norm_w[l], "bias": bias, "sink_col": sink_col,
        "w_out": w_out[l].astype(bf16), "norm_cross_w": norm_cross_w[l],
        "mem_wq": mem_wq[l].astype(bf16), "mem_wo": mem_wo[l].astype(bf16),
        "norm_ffn_w": norm_ffn_w[l], "peer_wq": peer_wq[l].astype(bf16),
        "peer_subkeys": peer_subkeys[l].astype(bf16), "peer_uv": make_peer_table(peer_u[l], peer_v[l]),
        "final_norm_w": final_norm_w,
    }
    bp, tp, d = x_prompt.shape
    bs, ts, _ = x_sample.shape

    wkv = jnp.concatenate([mem_wk[l], mem_wv[l]], axis=1).astype(bf16)
    kv = norm_matmul(mem_prompt.reshape(bp * N_MEM, d), norm_mem_w[l], wkv)
    kv3 = kv.reshape(bp, N_MEM, 2 * MEM_INNER)
    mk = kv3[:, :, :MEM_INNER].reshape(bp, N_MEM, MEM_HEADS, MEM_HEAD_DIM)
    mv = kv3[:, :, MEM_INNER:].reshape(bp, N_MEM, MEM_HEADS, MEM_HEAD_DIM)

    s0 = jnp.zeros((bp, HG_HEADS, HG_DK, HG_DV), jnp.float32)
    yp, sp, kp, vp = _trunk(x_prompt, kv3, kv3, 0, 1, s0, None, None, p)

    cmk = cache_mem_k[l].reshape(bs, N_MEM, MEM_INNER)
    cmv = cache_mem_v[l].reshape(bs, N_MEM, MEM_INNER)
    ck = cache_swa_k[l].reshape(bs, -1, SWA_KV_WIDTH)
    cv = cache_swa_v[l].reshape(bs, -1, SWA_KV_WIDTH)
    ys, ss, ks_new, vs_new = _trunk(x_sample, cmk, cmv, 0, 0, state_hgrn[l], ck, cv, p)

    return (yp, ys, sp[None], kp[None], vp[None], mk[None], mv[None],
            ss[None], ks_new[None], vs_new[None])
```

```python
import functools
import math

import numpy as np
import jax
import jax.numpy as jnp
from jax import lax
from jax.experimental import pallas as pl
from jax.experimental.pallas import tpu as pltpu

EPS = 1e-6
NEG_INF = -1e30
CHUNK = 64

HG_HEADS = 8
HG_DK = 128
HG_DV = 128
HG_WIDTH = HG_HEADS * HG_DV
HG_SUB = 16
HG_CHUNK = 64

SWA_HEADS = 16
SWA_KV_HEADS = 4
SWA_GROUP = SWA_HEADS // SWA_KV_HEADS
SWA_HEAD_DIM = 64
SWA_WIDTH = SWA_HEADS * SWA_HEAD_DIM
SWA_KV_WIDTH = SWA_KV_HEADS * SWA_HEAD_DIM
SWA_SCALE = SWA_HEAD_DIM ** -0.5
WINDOW = 128
WIN_CHUNKS = WINDOW // CHUNK
REL_BUCKETS = 32
REL_MAX_DIST = 128

N_MEM = 256
MEM_HEADS = 4
MEM_HEAD_DIM = 128
MEM_INNER = MEM_HEADS * MEM_HEAD_DIM
MEM_SCALE = MEM_HEAD_DIM ** -0.5

PEER_HEADS = 8
PEER_NKEYS = 128
PEER_DHALF = 128
PEER_TOPK = 16
PEER_PAIRS = PEER_HEADS * PEER_TOPK

OFF_Q, OFF_F, OFF_I, OFF_G = 0, 1024, 2048, 3072
OFF_QSW, OFF_KSW, OFF_VSW = 4096, 5120, 5376
IN_WIDTH = 5632

VMEM_LIMIT_BYTES = 56 * 1024 * 1024

_NT = (((1,), (1,)), ((), ()))
_TN = (((0,), (0,)), ((), ()))


def _cparams(sem):
    return pltpu.CompilerParams(dimension_semantics=sem, vmem_limit_bytes=VMEM_LIMIT_BYTES)


def _rms(x, w):
    return x * lax.rsqrt(jnp.mean(x * x, axis=-1, keepdims=True) + EPS) * w


def _norm_matmul_kernel(x_ref, nw_ref, w_ref, o_ref, h_ref):
    @pl.when(pl.program_id(1) == 0)
    def _():
        h_ref[...] = _rms(x_ref[...], nw_ref[...]).astype(h_ref.dtype)

    o_ref[...] = jnp.dot(h_ref[...], w_ref[...], preferred_element_type=jnp.float32)


def norm_matmul(x, nw, w, tm=512, tn=512):
    n, d = x.shape
    m = w.shape[1]
    tm = min(tm, n)
    tn = min(tn, m)
    assert n % tm == 0 and m % tn == 0
    return pl.pallas_call(
        _norm_matmul_kernel,
        grid=(n // tm, m // tn),
        in_specs=[
            pl.BlockSpec((tm, d), lambda i, j: (i, 0)),
            pl.BlockSpec((1, d), lambda i, j: (0, 0)),
            pl.BlockSpec((d, tn), lambda i, j: (0, j)),
        ],
        out_specs=pl.BlockSpec((tm, tn), lambda i, j: (i, j)),
        out_shape=jax.ShapeDtypeStruct((n, m), jnp.float32),
        scratch_shapes=[pltpu.VMEM((tm, d), jnp.bfloat16)],
        compiler_params=_cparams(("parallel", "arbitrary")),
        name="norm_matmul",
    )(x, nw.reshape(1, d), w)


def _cumsum_rows(x):
    n = x.shape[0]
    row = lax.broadcasted_iota(jnp.int32, x.shape, 0)
    s = 1
    while s < n:
        x = x + jnp.where(row >= s, pltpu.roll(x, s, axis=0), 0.0)
        s *= 2
    return x


def _bcast_rows(x, idxs, g):
    return jnp.concatenate(
        [jnp.broadcast_to(x[i:i + 1, :], (g, x.shape[1])) for i in idxs], axis=0)


def _hgrn_chunk(q, fpre, v, lb, st):
    c = q.shape[0]
    f = lb + (1.0 - lb) * jax.nn.sigmoid(fpre)
    k = 1.0 - f
    lf = jnp.log(f)
    bc = _cumsum_rows(lf)
    be = bc - lf
    bf16 = jnp.bfloat16
    ti = lax.broadcasted_iota(jnp.int32, (c, c), 0)
    si = lax.broadcasted_iota(jnp.int32, (c, c), 1)

    ref0 = _bcast_rows(be, range(0, c, HG_SUB), HG_SUB)
    qd = (q * jnp.exp(bc - ref0)).astype(bf16)
    kd = (k * jnp.exp(ref0 - bc)).astype(bf16)
    a = lax.dot_general(qd, kd, _NT, preferred_element_type=jnp.float32)
    attn = jnp.where((ti // HG_SUB == si // HG_SUB) & (si <= ti), a, 0.0)
    g = 2 * HG_SUB
    while g <= c:
        half = g // 2
        ref = _bcast_rows(bc, range(half - 1, c, g), g)
        ql = (q * jnp.exp(jnp.minimum(bc - ref, 0.0))).astype(bf16)
        kl = (k * jnp.exp(jnp.minimum(ref - bc, 0.0))).astype(bf16)
        a = lax.dot_general(ql, kl, _NT, preferred_element_type=jnp.float32)
        m = (ti // g == si // g) & (ti % g >= half) & (si % g < half)
        attn = jnp.where(m, a, attn)
        g *= 2

    vb = v.astype(bf16)
    q_in = (q * jnp.exp(bc)).astype(bf16)
    o = (lax.dot_general(q_in, st.astype(bf16), _NT, preferred_element_type=jnp.float32)
         + jnp.dot(attn.astype(bf16), vb, preferred_element_type=jnp.float32))
    b_last = bc[c - 1:c, :]
    k_out = (k * jnp.exp(b_last - bc)).astype(bf16)
    st_new = st * jnp.exp(b_last) + lax.dot_general(vb, k_out, _TN,
                                                    preferred_element_type=jnp.float32)
    return o, st_new


def _hgrn_kernel(q_ref, f_ref, i_ref, g_ref, s0_ref, lb_ref, nw_ref, o_ref, sfin_ref, st_ref):
    ci = pl.program_id(2)

    @pl.when(ci == 0)
    def _():
        st_ref[...] = s0_ref[0, 0].T

    ct = q_ref.shape[1]
    lb = lb_ref[...]
    nw = nw_ref[...]
    for j in range(ct // HG_CHUNK):
        sl = pl.ds(j * HG_CHUNK, HG_CHUNK)
        o, st_new = _hgrn_chunk(q_ref[0, sl, :], f_ref[0, sl, :], i_ref[0, sl, :], lb, st_ref[...])
        st_ref[...] = st_new
        o = _rms(o, nw)
        o_ref[0, sl, :] = (o * jax.nn.silu(g_ref[0, sl, :])).astype(o_ref.dtype)

    @pl.when(ci == pl.num_programs(2) - 1)
    def _():
        sfin_ref[0, 0] = st_ref[...].T


def hgrn(proj3, s0, lb, nw):
    b, t, _ = proj3.shape
    ct = min(256, t)
    assert t % ct == 0 and ct % HG_CHUNK == 0

    def col(off):
        return pl.BlockSpec((1, ct, HG_DK), lambda bi, h, c: (bi, c, off // HG_DK + h))

    return pl.pallas_call(
        _hgrn_kernel,
        grid=(b, HG_HEADS, t // ct),
        in_specs=[
            col(OFF_Q), col(OFF_F), col(OFF_I), col(OFF_G),
            pl.BlockSpec((1, 1, HG_DK, HG_DV), lambda bi, h, c: (bi, h, 0, 0)),
            pl.BlockSpec((1, HG_DK), lambda bi, h, c: (0, h)),
            pl.BlockSpec((1, HG_DV), lambda bi, h, c: (0, 0)),
        ],
        out_specs=[
            pl.BlockSpec((1, ct, HG_DV), lambda bi, h, c: (bi, c, h)),
            pl.BlockSpec((1, 1, HG_DK, HG_DV), lambda bi, h, c: (bi, h, 0, 0)),
        ],
        out_shape=[
            jax.ShapeDtypeStruct((b, t, HG_WIDTH), jnp.bfloat16),
            jax.ShapeDtypeStruct((b, HG_HEADS, HG_DK, HG_DV), jnp.float32),
        ],
        scratch_shapes=[pltpu.VMEM((HG_DV, HG_DK), jnp.float32)],
        compiler_params=_cparams(("parallel", "parallel", "arbitrary")),
        name="hgrn",
    )(proj3, proj3, proj3, proj3, s0, lb.reshape(1, -1), nw.reshape(1, -1))


def _swa_kernel(q_ref, k2_ref, k1_ref, k0_ref, v2_ref, v1_ref, v0_ref, bias_ref, sink_ref,
                o_ref, *, c_off):
    bf16 = jnp.bfloat16
    c = pl.program_id(1) + c_off
    hd = SWA_HEAD_DIM
    pad = jnp.zeros_like(k0_ref[0])
    kk = jnp.concatenate([k2_ref[0], k1_ref[0], k0_ref[0], pad], axis=0)
    vv = jnp.concatenate([v2_ref[0], v1_ref[0], v0_ref[0], pad], axis=0).astype(bf16)
    kt = kk.T.astype(bf16)
    n_keys = kk.shape[0]
    key_chunk = lax.broadcasted_iota(jnp.int32, (1, n_keys), 1) // CHUNK
    valid = ((c - WIN_CHUNKS + key_chunk) >= 0) & (key_chunk <= WIN_CHUNKS)
    zk = jnp.zeros((hd, n_keys), bf16)
    lane = lax.broadcasted_iota(jnp.int32, (n_keys, 2 * hd), 1)
    q = q_ref[0].astype(bf16)
    for kv in range(SWA_KV_HEADS):
        ktj = kt[kv * hd:(kv + 1) * hd, :]
        kt_lo = jnp.concatenate([ktj, zk], axis=0)
        kt_hi = jnp.concatenate([zk, ktj], axis=0)
        tile = vv[:, (kv // 2) * 2 * hd:(kv // 2 + 1) * 2 * hd]
        own_lo = kv % 2 == 0
        keep = (lane < hd) if own_lo else (lane >= hd)
        v_own = jnp.where(keep, tile, jnp.zeros_like(tile))
        v_swap = pltpu.roll(v_own.astype(jnp.float32), hd, axis=1).astype(bf16)
        v_lo, v_hi = (v_own, v_swap) if own_lo else (v_swap, v_own)
        c0 = kv * SWA_GROUP * hd
        qs = jnp.concatenate([q[:, c0:c0 + 2 * hd], q[:, c0 + 2 * hd:c0 + 4 * hd]], axis=0)
        probs = []
        for i, ktm in enumerate((kt_lo, kt_hi)):
            logits = jnp.dot(qs, ktm, preferred_element_type=jnp.float32)
            logits = logits * SWA_SCALE + bias_ref[kv, i]
            logits = jnp.where(valid, logits, NEG_INF)
            sink = sink_ref[kv, i]
            m = jnp.maximum(jnp.max(logits, axis=-1, keepdims=True), sink)
            e = jnp.exp(logits - m)
            p = e / (jnp.sum(e, axis=-1, keepdims=True) + jnp.exp(sink - m))
            probs.append(p.astype(bf16))
        o = (jnp.dot(probs[0], v_lo, preferred_element_type=jnp.float32)
             + jnp.dot(probs[1], v_hi, preferred_element_type=jnp.float32))
        o_ref[0, :, c0:c0 + 2 * hd] = o[0:CHUNK].astype(o_ref.dtype)
        o_ref[0, :, c0 + 2 * hd:c0 + 4 * hd] = o[CHUNK:2 * CHUNK].astype(o_ref.dtype)


def swa(proj3, bias, sink_col, cache_k=None, cache_v=None):
    b, t, _ = proj3.shape
    nc = t // CHUNK
    assert t % CHUNK == 0
    qspec = pl.BlockSpec((1, CHUNK, SWA_WIDTH), lambda bi, c: (bi, c, OFF_QSW // SWA_WIDTH))

    def cur(off):
        return pl.BlockSpec((1, CHUNK, SWA_KV_WIDTH), lambda bi, c: (bi, c, off // SWA_KV_WIDTH))

    if cache_k is None:
        def prev(off, back):
            return pl.BlockSpec((1, CHUNK, SWA_KV_WIDTH),
                                lambda bi, c: (bi, jnp.maximum(c - back, 0), off // SWA_KV_WIDTH))
        kspecs = [prev(OFF_KSW, 2), prev(OFF_KSW, 1), cur(OFF_KSW)]
        vspecs = [prev(OFF_VSW, 2), prev(OFF_VSW, 1), cur(OFF_VSW)]
        karrs = [proj3, proj3, proj3]
        varrs = [proj3, proj3, proj3]
        c_off = 0
    else:
        assert nc == 1 and cache_k.shape[1] == WINDOW

        def past(j):
            return pl.BlockSpec((1, CHUNK, SWA_KV_WIDTH), lambda bi, c: (bi, j, 0))
        kspecs = [past(0), past(1), cur(OFF_KSW)]
        vspecs = [past(0), past(1), cur(OFF_VSW)]
        karrs = [cache_k, cache_k, proj3]
        varrs = [cache_v, cache_v, proj3]
        c_off = WIN_CHUNKS
    rows = 2 * CHUNK
    n_keys = (WIN_CHUNKS + 2) * CHUNK
    return pl.pallas_call(
        functools.partial(_swa_kernel, c_off=c_off),
        grid=(b, nc),
        in_specs=[qspec] + kspecs + vspecs + [
            pl.BlockSpec((SWA_KV_HEADS, 2, rows, n_keys), lambda bi, c: (0, 0, 0, 0)),
            pl.BlockSpec((SWA_KV_HEADS, 2, rows, 1), lambda bi, c: (0, 0, 0, 0)),
        ],
        out_specs=pl.BlockSpec((1, CHUNK, SWA_WIDTH), lambda bi, c: (bi, c, 0)),
        out_shape=jax.ShapeDtypeStruct((b, t, SWA_WIDTH), jnp.bfloat16),
        compiler_params=_cparams(("parallel", "parallel")),
        name="swa",
    )(proj3, *karrs, *varrs, bias, sink_col)


def _t5_bias(rel, table):
    nb = REL_BUCKETS // 2
    max_exact = nb // 2
    side = jnp.where(rel > 0, nb, 0)
    n = jnp.abs(rel)
    n_f = jnp.maximum(n, max_exact).astype(jnp.float32)
    large = max_exact + (jnp.log(n_f / max_exact) / math.log(REL_MAX_DIST / max_exact)
                         * (nb - max_exact)).astype(jnp.int32)
    large = jnp.minimum(large, nb - 1)
    bucket = side + jnp.where(n < max_exact, n, large)
    return jnp.transpose(table[bucket].astype(jnp.float32), (2, 0, 1))


def _outproj_kernel(a1_ref, a2_ref, w_ref, x_ref, o_ref):
    k1 = a1_ref.shape[1]
    acc = jnp.dot(a1_ref[...], w_ref[0:k1, :], preferred_element_type=jnp.float32)
    acc += jnp.dot(a2_ref[...], w_ref[k1:, :], preferred_element_type=jnp.float32)
    o_ref[...] = x_ref[...] + acc


def outproj(a1, a2, w, x, tm=512, tn=2048):
    n, d = x.shape
    k1, k2 = a1.shape[1], a2.shape[1]
    tm = min(tm, n)
    assert n % tm == 0 and d % tn == 0
    return pl.pallas_call(
        _outproj_kernel,
        grid=(n // tm, d // tn),
        in_specs=[
            pl.BlockSpec((tm, k1), lambda i, j: (i, 0)),
            pl.BlockSpec((tm, k2), lambda i, j: (i, 0)),
            pl.BlockSpec((k1 + k2, tn), lambda i, j: (0, j)),
            pl.BlockSpec((tm, tn), lambda i, j: (i, j)),
        ],
        out_specs=pl.BlockSpec((tm, tn), lambda i, j: (i, j)),
        out_shape=jax.ShapeDtypeStruct((n, d), jnp.float32),
        compiler_params=_cparams(("parallel", "arbitrary")),
        name="outproj",
    )(a1, a2, w, x)


def _memattn_kernel(x_ref, nw_ref, wq_ref, mk_ref, mv_ref, wo_ref, o_ref):
    bf16 = jnp.bfloat16
    x = x_ref[0]
    h = _rms(x, nw_ref[...]).astype(bf16)
    q = jnp.dot(h, wq_ref[...], preferred_element_type=jnp.float32)
    mk = mk_ref[0].astype(bf16)
    mv = mv_ref[0].astype(bf16)
    outs = []
    for hh in range(MEM_HEADS):
        sl = slice(hh * MEM_HEAD_DIM, (hh + 1) * MEM_HEAD_DIM)
        logits = lax.dot_general(q[:, sl].astype(bf16), mk[:, sl], _NT,
                                 preferred_element_type=jnp.float32) * MEM_SCALE
        m = jnp.max(logits, axis=-1, keepdims=True)
        e = jnp.exp(logits - m)
        p = e / jnp.sum(e, axis=-1, keepdims=True)
        outs.append(jnp.dot(p.astype(bf16), mv[:, sl], preferred_element_type=jnp.float32))
    o = jnp.concatenate(outs, axis=-1).astype(bf16)
    o_ref[0] = x + jnp.dot(o, wo_ref[...], preferred_element_type=jnp.float32)


def memattn(x3, nw, wq, mk_arr, mv_arr, mk_col, mv_col, wo, tm=512):
    b, t, d = x3.shape
    tm = min(tm, t)
    assert t % tm == 0
    return pl.pallas_call(
        _memattn_kernel,
        grid=(b, t // tm),
        in_specs=[
            pl.BlockSpec((1, tm, d), lambda bi, i: (bi, i, 0)),
            pl.BlockSpec((1, d), lambda bi, i: (0, 0)),
            pl.BlockSpec((d, MEM_INNER), lambda bi, i: (0, 0)),
            pl.BlockSpec((1, N_MEM, MEM_INNER), lambda bi, i: (bi, 0, mk_col)),
            pl.BlockSpec((1, N_MEM, MEM_INNER), lambda bi, i: (bi, 0, mv_col)),
            pl.BlockSpec((MEM_INNER, d), lambda bi, i: (0, 0)),
        ],
        out_specs=pl.BlockSpec((1, tm, d), lambda bi, i: (bi, i, 0)),
        out_shape=jax.ShapeDtypeStruct((b, t, d), jnp.float32),
        compiler_params=_cparams(("parallel", "arbitrary")),
        name="memattn",
    )(x3, nw.reshape(1, d), wq, mk_arr, mv_arr, wo)


def _topk_rows(s, k, payload=None):
    n = s.shape[0]
    row = lax.broadcasted_iota(jnp.int32, s.shape, 0)
    vals, picks = [], []
    for _ in range(k):
        m = jnp.max(s, axis=0, keepdims=True)
        idx = jnp.min(jnp.where(s == m, row, n), axis=0, keepdims=True)
        hit = row == idx
        vals.append(m)
        if payload is None:
            picks.append(idx)
        else:
            picks.append(jnp.sum(jnp.where(hit, payload, 0), axis=0, keepdims=True))
        s = jnp.where(hit, -jnp.inf, s)
    return jnp.concatenate(vals, axis=0), jnp.concatenate(picks, axis=0)


def _peer_retrieve_kernel(q_ref, sk_ref, eidx_ref, gate_ref):
    bf16 = jnp.bfloat16
    k = PEER_TOPK
    eids, gates = [], []
    for h in range(PEER_HEADS):
        tops = []
        for c in range(2):
            col = (h * 2 + c) * PEER_DHALF
            qh = q_ref[:, col:col + PEER_DHALF].astype(bf16)
            st = lax.dot_general(sk_ref[c, h], qh, _NT,
                                 preferred_element_type=jnp.float32)
            tops.append(_topk_rows(st, k))
        (s1, i1), (s2, i2) = tops
        hk = k // 2
        cand = jnp.concatenate(
            [s1[0:1] + s2] + [s1[a:a + 1] + s2[0:hk] for a in range(1, hk)] + [s1[hk:k] + s2[0:1]],
            axis=0)
        cidx = jnp.concatenate(
            [i1[0:1] * PEER_NKEYS + i2] + [i1[a:a + 1] * PEER_NKEYS + i2[0:hk] for a in range(1, hk)]
            + [i1[hk:k] * PEER_NKEYS + i2[0:1]], axis=0)
        top_s, eid = _topk_rows(cand, k, payload=cidx)
        e = jnp.exp(top_s - top_s[0:1])
        gates.append(e / jnp.sum(e, axis=0, keepdims=True))
        eids.append(eid)
    eidx_ref[...] = jnp.concatenate(eids, axis=0).T
    gate_ref[...] = jnp.concatenate(gates, axis=0).T


def peer_retrieve(qp, subkeys, tn=128):
    n, d = qp.shape
    tn = min(tn, n)
    assert n % tn == 0
    return pl.pallas_call(
        _peer_retrieve_kernel,
        grid=(n // tn,),
        in_specs=[
            pl.BlockSpec((tn, d), lambda i: (i, 0)),
            pl.BlockSpec(subkeys.shape, lambda i: (0, 0, 0, 0)),
        ],
        out_specs=[
            pl.BlockSpec((tn, PEER_PAIRS), lambda i: (i, 0)),
            pl.BlockSpec((tn, PEER_PAIRS), lambda i: (i, 0)),
        ],
        out_shape=[
            jax.ShapeDtypeStruct((n, PEER_PAIRS), jnp.int32),
            jax.ShapeDtypeStruct((n, PEER_PAIRS), jnp.float32),
        ],
        compiler_params=_cparams(("parallel",)),
        name="peer_retrieve",
    )(qp, subkeys)


PEER_SLOTS = 8
PEER_AHEAD = PEER_SLOTS - 2
PEER_DMA_QUEUES = 2
PEER_LANES = 128
PEER_EROWS = 16
PEER_PITCH = 24


def _even_odd_chunks(a):
    nc = a.shape[1] // PEER_LANES
    pick = lambda c: a[:, c * PEER_LANES:(c + 1) * PEER_LANES]
    return jnp.concatenate([pick(c) for c in range(0, nc, 2)] + [pick(c) for c in range(1, nc, 2)],
                           axis=1)


def _interleave_chunks(a):
    nc = a.shape[1] // PEER_LANES
    half = nc // 2
    pick = lambda c: a[:, c * PEER_LANES:(c + 1) * PEER_LANES]
    return jnp.concatenate([pick(c // 2 + (half if c % 2 else 0)) for c in range(nc)], axis=1)


def _peer_main_kernel(eidx_ref, gate_ref, x_ref, nw_ref, fw_ref, uv_hbm, o_ref, h_ref, y_ref,
                      dup_ref, w2_ref, *scratch):
    bf16 = jnp.bfloat16
    f32 = jnp.float32
    bufs, sem = scratch[:PEER_SLOTS], scratch[PEER_SLOTS]
    tb, d = x_ref.shape
    half = d // 2
    n_rows = PEER_EROWS // 2
    lanes2 = 2 * PEER_PAIRS
    h_ref[...] = _even_odd_chunks(_rms(x_ref[...], nw_ref[...])).astype(bf16).astype(f32)
    dup_ref[...] = (lax.broadcasted_iota(jnp.int32, (PEER_PAIRS, lanes2), 1) // 2
                    == lax.broadcasted_iota(jnp.int32, (PEER_PAIRS, lanes2), 0)).astype(bf16)

    rows = PEER_PAIRS * PEER_EROWS

    def issue(t, slot, part=0, parts=1):
        n_p = PEER_PAIRS // parts
        for p in range(part * n_p, (part + 1) * n_p):
            r = pl.multiple_of(eidx_ref[t, p] * PEER_EROWS, PEER_EROWS)
            pltpu.make_async_copy(uv_hbm.at[pl.ds(r, PEER_EROWS)],
                                  bufs[slot].at[pl.ds(p * PEER_PITCH, PEER_EROWS)],
                                  sem.at[slot]).start(priority=p % PEER_DMA_QUEUES)

    def wait(slot):
        pltpu.make_async_copy(uv_hbm.at[pl.ds(0, rows)], bufs[slot].at[pl.ds(0, rows)],
                              sem.at[slot]).wait()

    def words(slot, m):
        return bufs[slot][pl.ds(m, PEER_PAIRS, stride=PEER_PITCH), :]

    sub_w = lax.broadcasted_iota(jnp.int32, (8, lanes2), 0)
    par_w = lax.broadcasted_iota(jnp.int32, (8, lanes2), 1) % 2

    def u_side(t, slot, between=None):
        hrow = h_ref[pl.ds(t, 1), :]
        acc = None
        for m in range(n_rows):
            if between is not None:
                between(m)
            wd = words(slot, m)
            lo = lax.bitcast_convert_type(wd << 16, f32)
            hi = lax.bitcast_convert_type(wd & jnp.int32(-65536), f32)
            term = (lo * hrow[:, m * PEER_LANES:(m + 1) * PEER_LANES]
                    + hi * hrow[:, half + m * PEER_LANES:half + (m + 1) * PEER_LANES])
            acc = term if acc is None else acc + term
        act = jnp.sum(acc.T, axis=0, keepdims=True)
        gelu = 0.5 * act * (1.0 + lax.erf(act * np.float32(math.sqrt(0.5))))
        w = gate_ref[pl.ds(t, 1), :] * gelu
        wb = jnp.broadcast_to(w, (8, PEER_PAIRS)).astype(bf16)
        wd2 = jnp.dot(wb, dup_ref[...], preferred_element_type=f32)
        w2_ref[slot] = jnp.where(par_w == sub_w, wd2, 0.0)

    def v_side(t, slot):
        wv = jnp.concatenate([pltpu.bitcast(words(slot, n_rows + m), bf16)
                              for m in range(n_rows)], axis=1)
        y2 = jnp.dot(w2_ref[slot].astype(bf16), wv, preferred_element_type=f32)
        y_ref[pl.ds(t, 1), :] = jnp.concatenate([y2[0:1], y2[1:2]], axis=1)

    def step(t, j, do_issue=True):
        wait(j % PEER_SLOTS)
        spread = (lambda m: issue(t + PEER_AHEAD, (j + PEER_AHEAD) % PEER_SLOTS, m, n_rows)
                  ) if do_issue else None
        u_side(t, j % PEER_SLOTS, spread)
        v_side(t - 1, (j - 1) % PEER_SLOTS)

    for t in range(PEER_AHEAD):
        issue(t, t)
    wait(0)
    u_side(0, 0)
    issue(PEER_AHEAD, PEER_AHEAD)

    def body(i, carry):
        t0 = 1 + i * PEER_SLOTS
        for j in range(PEER_SLOTS):
            step(t0 + j, 1 + j)
        return carry

    n_main = (tb - PEER_AHEAD - 1) // PEER_SLOTS
    lax.fori_loop(0, n_main, body, 0)
    for t in range(1 + n_main * PEER_SLOTS, tb):
        step(t, t, do_issue=t + PEER_AHEAD < tb)
    v_side(tb - 1, (tb - 1) % PEER_SLOTS)

    o_ref[...] = _rms(x_ref[...] + _interleave_chunks(y_ref[...]), fw_ref[...])


def make_peer_table(u_tab, v_tab):
    e = u_tab.shape[0]

    def pack(tab):
        bits = lax.bitcast_convert_type(tab.astype(jnp.bfloat16).astype(jnp.float32), jnp.uint32)
        bits = bits.reshape(e, PEER_EROWS // 2, 2 * PEER_LANES)
        return (bits[:, :, :PEER_LANES] >> 16) | (bits[:, :, PEER_LANES:] & np.uint32(0xFFFF0000))

    word = jnp.concatenate([pack(u_tab), pack(v_tab)], axis=1)
    return lax.bitcast_convert_type(word, jnp.int32).reshape(e * PEER_EROWS, PEER_LANES)


def peer_main(eidx, gate, x, nw, fw, uv_tab, tb=128):
    n, d = x.shape
    tb = min(tb, n)
    assert n % tb == 0 and tb > PEER_AHEAD + PEER_SLOTS + 1
    assert uv_tab.shape[1] == PEER_LANES and d == PEER_EROWS * PEER_LANES
    return pl.pallas_call(
        _peer_main_kernel,
        grid=(n // tb,),
        in_specs=[
            pl.BlockSpec((tb, PEER_PAIRS), lambda i: (i, 0), memory_space=pltpu.SMEM),
            pl.BlockSpec((tb, PEER_PAIRS), lambda i: (i, 0)),
            pl.BlockSpec((tb, d), lambda i: (i, 0)),
            pl.BlockSpec((1, d), lambda i: (0, 0)),
            pl.BlockSpec((1, d), lambda i: (0, 0)),
            pl.BlockSpec(memory_space=pl.ANY),
        ],
        out_specs=pl.BlockSpec((tb, d), lambda i: (i, 0)),
        out_shape=jax.ShapeDtypeStruct((n, d), jnp.float32),
        scratch_shapes=[
            pltpu.VMEM((tb, d), jnp.float32),
            pltpu.VMEM((tb, d), jnp.float32),
            pltpu.VMEM((PEER_PAIRS, 2 * PEER_PAIRS), jnp.bfloat16),
            pltpu.VMEM((PEER_SLOTS, 8, 2 * PEER_PAIRS), jnp.float32),
            *[pltpu.VMEM((PEER_PAIRS * PEER_PITCH, PEER_LANES), jnp.int32)
              for _ in range(PEER_SLOTS)],
            pltpu.SemaphoreType.DMA((PEER_SLOTS,)),
        ],
        compiler_params=_cparams(("arbitrary",)),
        name="peer_main",
    )(eidx, gate, x, nw.reshape(1, d), fw.reshape(1, d), uv_tab)


def _trunk(x, mk_arr, mv_arr, mk_col, mv_col, s0, cache_k, cache_v, p):
    b, t, d = x.shape
    n = b * t
    proj = norm_matmul(x.reshape(n, d), p["norm_mix_w"], p["w_in"], tn=IN_WIDTH // 2)
    proj3 = proj.reshape(b, t, IN_WIDTH)
    o_hg, s_fin = hgrn(proj3, s0, p["lb"], p["hg_norm_w"])
    o_sw = swa(proj3, p["bias"], p["sink_col"], cache_k, cache_v)
    x1 = outproj(o_hg.reshape(n, HG_WIDTH), o_sw.reshape(n, SWA_WIDTH), p["w_out"], x.reshape(n, d))
    x2 = memattn(x1.reshape(b, t, d), p["norm_cross_w"], p["mem_wq"], mk_arr, mv_arr,
                 mk_col, mv_col, p["mem_wo"]).reshape(n, d)
    qp = norm_matmul(x2, p["norm_ffn_w"], p["peer_wq"], tn=2048)
    eidx, gate = peer_retrieve(qp, p["peer_subkeys"])
    y = peer_main(eidx, gate, x2, p["norm_ffn_w"], p["final_norm_w"], p["peer_uv"])
    keep = min(WINDOW, t) if cache_k is None else t
    k_rows = proj3[:, t - keep:, OFF_KSW:OFF_KSW + SWA_KV_WIDTH]
    v_rows = proj3[:, t - keep:, OFF_VSW:OFF_VSW + SWA_KV_WIDTH]
    k_rows = k_rows.reshape(b, keep, SWA_KV_HEADS, SWA_HEAD_DIM)
    v_rows = v_rows.reshape(b, keep, SWA_KV_HEADS, SWA_HEAD_DIM)
    return y.reshape(b, t, d), s_fin, k_rows, v_rows


def kernel(x_prompt, x_sample, mem_prompt, state_hgrn, cache_swa_k, cache_swa_v, cache_mem_k, cache_mem_v, rel_bias, hg_lb_logits, norm_mix_w, w_in, hg_norm_w, swa_sinks, w_out, norm_mem_w, norm_cross_w, mem_wq, mem_wk, mem_wv, mem_wo, norm_ffn_w, peer_wq, peer_subkeys, peer_u, peer_v, final_norm_w):
    bf16 = jnp.bfloat16
    depth = w_in.shape[0]
    assert depth == 1
    l = 0
    lb_all = jnp.cumsum(jax.nn.softmax(hg_lb_logits.astype(jnp.float32), axis=0), axis=0)
    n_keys = (WIN_CHUNKS + 1) * CHUNK
    rel = jnp.arange(n_keys)[None, :] - (WIN_CHUNKS * CHUNK + jnp.arange(CHUNK))[:, None]
    bias = _t5_bias(rel, rel_bias).reshape(SWA_KV_HEADS, 2, 2, CHUNK, n_keys)
    bias = bias.transpose(0, 2, 1, 3, 4).reshape(SWA_KV_HEADS, 2, 2 * CHUNK, n_keys)
    bias = jnp.pad(bias, ((0, 0), (0, 0), (0, 0), (0, CHUNK)))
    sink_col = jnp.broadcast_to(
        swa_sinks[l].astype(jnp.float32).reshape(SWA_KV_HEADS, 2, 2, 1),
        (SWA_KV_HEADS, 2, 2, CHUNK)).transpose(0, 2, 1, 3).reshape(SWA_KV_HEADS, 2, 2 * CHUNK, 1)
    p = {
        "norm_mix_w": norm_mix_w[l], "w_in": w_in[l].astype(bf16), "lb": lb_all[l],
        "hg_norm_w": hg_norm_w[l], "bias": bias, "sink_col": sink_col,
        "w_out": w_out[l].astype(bf16), "norm_cross_w": norm_cross_w[l],
        "mem_wq": mem_wq[l].astype(bf16), "mem_wo": mem_wo[l].astype(bf16),
        "norm_ffn_w": norm_ffn_w[l], "peer_wq": peer_wq[l].astype(bf16),
        "peer_subkeys": peer_subkeys[l].astype(bf16), "peer_uv": make_peer_table(peer_u[l], peer_v[l]),
        "final_norm_w": final_norm_w,
    }
    bp, tp, d = x_prompt.shape
    bs, ts, _ = x_sample.shape

    wkv = jnp.concatenate([mem_wk[l], mem_wv[l]], axis=1).astype(bf16)
    kv = norm_matmul(mem_prompt.reshape(bp * N_MEM, d), norm_mem_w[l], wkv)
    kv3 = kv.reshape(bp, N_MEM, 2 * MEM_INNER)
    mk = kv3[:, :, :MEM_INNER].reshape(bp, N_MEM, MEM_HEADS, MEM_HEAD_DIM)
    mv = kv3[:, :, MEM_INNER:].reshape(bp, N_MEM, MEM_HEADS, MEM_HEAD_DIM)

    s0 = jnp.zeros((bp, HG_HEADS, HG_DK, HG_DV), jnp.float32)
    yp, sp, kp, vp = _trunk(x_prompt, kv3, kv3, 0, 1, s0, None, None, p)

    cmk = cache_mem_k[l].reshape(bs, N_MEM, MEM_INNER)
    cmv = cache_mem_v[l].reshape(bs, N_MEM, MEM_INNER)
    ck = cache_swa_k[l].reshape(bs, -1, SWA_KV_WIDTH)
    cv = cache_swa_v[l].reshape(bs, -1, SWA_KV_WIDTH)
    ys, ss, ks_new, vs_new = _trunk(x_sample, cmk, cmv, 0, 0, state_hgrn[l], ck, cv, p)

    return (yp, ys, sp[None], kp[None], vp[None], mk[None], mv[None],
            ss[None], ks_new[None], vs_new[None])
```

```python
import functools
import math

import numpy as np
import jax
import jax.numpy as jnp
from jax import lax
from jax.experimental import pallas as pl
from jax.experimental.pallas import tpu as pltpu

EPS = 1e-6
NEG_INF = -1e30
CHUNK = 64

HG_HEADS = 8
HG_DK = 128
HG_DV = 128
HG_WIDTH = HG_HEADS * HG_DV
HG_SUB = 16
HG_CHUNK = 64

SWA_HEADS = 16
SWA_KV_HEADS = 4
SWA_GROUP = SWA_HEADS // SWA_KV_HEADS
SWA_HEAD_DIM = 64
SWA_WIDTH = SWA_HEADS * SWA_HEAD_DIM
SWA_KV_WIDTH = SWA_KV_HEADS * SWA_HEAD_DIM
SWA_SCALE = SWA_HEAD_DIM ** -0.5
WINDOW = 128
WIN_CHUNKS = WINDOW // CHUNK
REL_BUCKETS = 32
REL_MAX_DIST = 128

N_MEM = 256
MEM_HEADS = 4
MEM_HEAD_DIM = 128
MEM_INNER = MEM_HEADS * MEM_HEAD_DIM
MEM_SCALE = MEM_HEAD_DIM ** -0.5

PEER_HEADS = 8
PEER_NKEYS = 128
PEER_DHALF = 128
PEER_TOPK = 16
PEER_PAIRS = PEER_HEADS * PEER_TOPK

OFF_Q, OFF_F, OFF_I, OFF_G = 0, 1024, 2048, 3072
OFF_QSW, OFF_KSW, OFF_VSW = 4096, 5120, 5376
IN_WIDTH = 5632

VMEM_LIMIT_BYTES = 56 * 1024 * 1024

_NT = (((1,), (1,)), ((), ()))
_TN = (((0,), (0,)), ((), ()))


def _cparams(sem):
    return pltpu.CompilerParams(dimension_semantics=sem, vmem_limit_bytes=VMEM_LIMIT_BYTES)


def _rms(x, w):
    return x * lax.rsqrt(jnp.mean(x * x, axis=-1, keepdims=True) + EPS) * w


def _norm_matmul_kernel(x_ref, nw_ref, w_ref, o_ref, h_ref):
    @pl.when(pl.program_id(1) == 0)
    def _():
        h_ref[...] = _rms(x_ref[...], nw_ref[...]).astype(h_ref.dtype)

    o_ref[...] = jnp.dot(h_ref[...], w_ref[...], preferred_element_type=jnp.float32)


def norm_matmul(x, nw, w, tm=512, tn=512):
    n, d = x.shape
    m = w.shape[1]
    tm = min(tm, n)
    tn = min(tn, m)
    assert n % tm == 0 and m % tn == 0
    return pl.pallas_call(
        _norm_matmul_kernel,
        grid=(n // tm, m // tn),
        in_specs=[
            pl.BlockSpec((tm, d), lambda i, j: (i, 0)),
            pl.BlockSpec((1, d), lambda i, j: (0, 0)),
            pl.BlockSpec((d, tn), lambda i, j: (0, j)),
        ],
        out_specs=pl.BlockSpec((tm, tn), lambda i, j: (i, j)),
        out_shape=jax.ShapeDtypeStruct((n, m), jnp.float32),
        scratch_shapes=[pltpu.VMEM((tm, d), jnp.bfloat16)],
        compiler_params=_cparams(("parallel", "arbitrary")),
        name="norm_matmul",
    )(x, nw.reshape(1, d), w)


def _cumsum_rows(x):
    n = x.shape[0]
    row = lax.broadcasted_iota(jnp.int32, x.shape, 0)
    s = 1
    while s < n:
        x = x + jnp.where(row >= s, pltpu.roll(x, s, axis=0), 0.0)
        s *= 2
    return x


def _bcast_rows(x, idxs, g):
    return jnp.concatenate(
        [jnp.broadcast_to(x[i:i + 1, :], (g, x.shape[1])) for i in idxs], axis=0)


def _hgrn_chunk(q, fpre, v, lb, st):
    c = q.shape[0]
    f = lb + (1.0 - lb) * jax.nn.sigmoid(fpre)
    k = 1.0 - f
    lf = jnp.log(f)
    bc = _cumsum_rows(lf)
    be = bc - lf
    bf16 = jnp.bfloat16
    ti = lax.broadcasted_iota(jnp.int32, (c, c), 0)
    si = lax.broadcasted_iota(jnp.int32, (c, c), 1)

    ref0 = _bcast_rows(be, range(0, c, HG_SUB), HG_SUB)
    qd = (q * jnp.exp(bc - ref0)).astype(bf16)
    kd = (k * jnp.exp(ref0 - bc)).astype(bf16)
    a = lax.dot_general(qd, kd, _NT, preferred_element_type=jnp.float32)
    attn = jnp.where((ti // HG_SUB == si // HG_SUB) & (si <= ti), a, 0.0)
    g = 2 * HG_SUB
    while g <= c:
        half = g // 2
        ref = _bcast_rows(bc, range(half - 1, c, g), g)
        ql = (q * jnp.exp(jnp.minimum(bc - ref, 0.0))).astype(bf16)
        kl = (k * jnp.exp(jnp.minimum(ref - bc, 0.0))).astype(bf16)
        a = lax.dot_general(ql, kl, _NT, preferred_element_type=jnp.float32)
        m = (ti // g == si // g) & (ti % g >= half) & (si % g < half)
        attn = jnp.where(m, a, attn)
        g *= 2

    vb = v.astype(bf16)
    q_in = (q * jnp.exp(bc)).astype(bf16)
    o = (lax.dot_general(q_in, st.astype(bf16), _NT, preferred_element_type=jnp.float32)
         + jnp.dot(attn.astype(bf16), vb, preferred_element_type=jnp.float32))
    b_last = bc[c - 1:c, :]
    k_out = (k * jnp.exp(b_last - bc)).astype(bf16)
    st_new = st * jnp.exp(b_last) + lax.dot_general(vb, k_out, _TN,
                                                    preferred_element_type=jnp.float32)
    return o, st_new


def _hgrn_kernel(q_ref, f_ref, i_ref, g_ref, s0_ref, lb_ref, nw_ref, o_ref, sfin_ref, st_ref):
    ci = pl.program_id(2)

    @pl.when(ci == 0)
    def _():
        st_ref[...] = s0_ref[0, 0].T

    ct = q_ref.shape[1]
    lb = lb_ref[...]
    nw = nw_ref[...]
    for j in range(ct // HG_CHUNK):
        sl = pl.ds(j * HG_CHUNK, HG_CHUNK)
        o, st_new = _hgrn_chunk(q_ref[0, sl, :], f_ref[0, sl, :], i_ref[0, sl, :], lb, st_ref[...])
        st_ref[...] = st_new
        o = _rms(o, nw)
        o_ref[0, sl, :] = (o * jax.nn.silu(g_ref[0, sl, :])).astype(o_ref.dtype)

    @pl.when(ci == pl.num_programs(2) - 1)
    def _():
        sfin_ref[0, 0] = st_ref[...].T


def hgrn(proj3, s0, lb, nw):
    b, t, _ = proj3.shape
    ct = min(256, t)
    assert t % ct == 0 and ct % HG_CHUNK == 0

    def col(off):
        return pl.BlockSpec((1, ct, HG_DK), lambda bi, h, c: (bi, c, off // HG_DK + h))

    return pl.pallas_call(
        _hgrn_kernel,
        grid=(b, HG_HEADS, t // ct),
        in_specs=[
            col(OFF_Q), col(OFF_F), col(OFF_I), col(OFF_G),
            pl.BlockSpec((1, 1, HG_DK, HG_DV), lambda bi, h, c: (bi, h, 0, 0)),
            pl.BlockSpec((1, HG_DK), lambda bi, h, c: (0, h)),
            pl.BlockSpec((1, HG_DV), lambda bi, h, c: (0, 0)),
        ],
        out_specs=[
            pl.BlockSpec((1, ct, HG_DV), lambda bi, h, c: (bi, c, h)),
            pl.BlockSpec((1, 1, HG_DK, HG_DV), lambda bi, h, c: (bi, h, 0, 0)),
        ],
        out_shape=[
            jax.ShapeDtypeStruct((b, t, HG_WIDTH), jnp.bfloat16),
            jax.ShapeDtypeStruct((b, HG_HEADS, HG_DK, HG_DV), jnp.float32),
        ],
        scratch_shapes=[pltpu.VMEM((HG_DV, HG_DK), jnp.float32)],
        compiler_params=_cparams(("parallel", "parallel", "arbitrary")),
        name="hgrn",
    )(proj3, proj3, proj3, proj3, s0, lb.reshape(1, -1), nw.reshape(1, -1))


def _swa_kernel(q_ref, k2_ref, k1_ref, k0_ref, v2_ref, v1_ref, v0_ref, bias_ref, sink_ref,
                o_ref, *, c_off):
    bf16 = jnp.bfloat16
    c = pl.program_id(1) + c_off
    hd = SWA_HEAD_DIM
    pad = jnp.zeros_like(k0_ref[0])
    kk = jnp.concatenate([k2_ref[0], k1_ref[0], k0_ref[0], pad], axis=0)
    vv = jnp.concatenate([v2_ref[0], v1_ref[0], v0_ref[0], pad], axis=0).astype(bf16)
    kt = kk.T.astype(bf16)
    n_keys = kk.shape[0]
    key_chunk = lax.broadcasted_iota(jnp.int32, (1, n_keys), 1) // CHUNK
    valid = ((c - WIN_CHUNKS + key_chunk) >= 0) & (key_chunk <= WIN_CHUNKS)
    zk = jnp.zeros((hd, n_keys), bf16)
    lane = lax.broadcasted_iota(jnp.int32, (n_keys, 2 * hd), 1)
    q = q_ref[0].astype(bf16)
    for kv in range(SWA_KV_HEADS):
        ktj = kt[kv * hd:(kv + 1) * hd, :]
        kt_lo = jnp.concatenate([ktj, zk], axis=0)
        kt_hi = jnp.concatenate([zk, ktj], axis=0)
        tile = vv[:, (kv // 2) * 2 * hd:(kv // 2 + 1) * 2 * hd]
        own_lo = kv % 2 == 0
        keep = (lane < hd) if own_lo else (lane >= hd)
        v_own = jnp.where(keep, tile, jnp.zeros_like(tile))
        v_swap = pltpu.roll(v_own.astype(jnp.float32), hd, axis=1).astype(bf16)
        v_lo, v_hi = (v_own, v_swap) if own_lo else (v_swap, v_own)
        c0 = kv * SWA_GROUP * hd
        qs = jnp.concatenate([q[:, c0:c0 + 2 * hd], q[:, c0 + 2 * hd:c0 + 4 * hd]], axis=0)
        probs = []
        for i, ktm in enumerate((kt_lo, kt_hi)):
            logits = jnp.dot(qs, ktm, preferred_element_type=jnp.float32)
            logits = logits * SWA_SCALE + bias_ref[kv, i]
            logits = jnp.where(valid, logits, NEG_INF)
            sink = sink_ref[kv, i]
            m = jnp.maximum(jnp.max(logits, axis=-1, keepdims=True), sink)
            e = jnp.exp(logits - m)
            p = e / (jnp.sum(e, axis=-1, keepdims=True) + jnp.exp(sink - m))
            probs.append(p.astype(bf16))
        o = (jnp.dot(probs[0], v_lo, preferred_element_type=jnp.float32)
             + jnp.dot(probs[1], v_hi, preferred_element_type=jnp.float32))
        o_ref[0, :, c0:c0 + 2 * hd] = o[0:CHUNK].astype(o_ref.dtype)
        o_ref[0, :, c0 + 2 * hd:c0 + 4 * hd] = o[CHUNK:2 * CHUNK].astype(o_ref.dtype)


def swa(proj3, bias, sink_col, cache_k=None, cache_v=None):
    b, t, _ = proj3.shape
    nc = t // CHUNK
    assert t % CHUNK == 0
    qspec = pl.BlockSpec((1, CHUNK, SWA_WIDTH), lambda bi, c: (bi, c, OFF_QSW // SWA_WIDTH))

    def cur(off):
        return pl.BlockSpec((1, CHUNK, SWA_KV_WIDTH), lambda bi, c: (bi, c, off // SWA_KV_WIDTH))

    if cache_k is None:
        def prev(off, back):
            return pl.BlockSpec((1, CHUNK, SWA_KV_WIDTH),
                                lambda bi, c: (bi, jnp.maximum(c - back, 0), off // SWA_KV_WIDTH))
        kspecs = [prev(OFF_KSW, 2), prev(OFF_KSW, 1), cur(OFF_KSW)]
        vspecs = [prev(OFF_VSW, 2), prev(OFF_VSW, 1), cur(OFF_VSW)]
        karrs = [proj3, proj3, proj3]
        varrs = [proj3, proj3, proj3]
        c_off = 0
    else:
        assert nc == 1 and cache_k.shape[1] == WINDOW

        def past(j):
            return pl.BlockSpec((1, CHUNK, SWA_KV_WIDTH), lambda bi, c: (bi, j, 0))
        kspecs = [past(0), past(1), cur(OFF_KSW)]
        vspecs = [past(0), past(1), cur(OFF_VSW)]
        karrs = [cache_k, cache_k, proj3]
        varrs = [cache_v, cache_v, proj3]
        c_off = WIN_CHUNKS
    rows = 2 * CHUNK
    n_keys = (WIN_CHUNKS + 2) * CHUNK
    return pl.pallas_call(
        functools.partial(_swa_kernel, c_off=c_off),
        grid=(b, nc),
        in_specs=[qspec] + kspecs + vspecs + [
            pl.BlockSpec((SWA_KV_HEADS, 2, rows, n_keys), lambda bi, c: (0, 0, 0, 0)),
            pl.BlockSpec((SWA_KV_HEADS, 2, rows, 1), lambda bi, c: (0, 0, 0, 0)),
        ],
        out_specs=pl.BlockSpec((1, CHUNK, SWA_WIDTH), lambda bi, c: (bi, c, 0)),
        out_shape=jax.ShapeDtypeStruct((b, t, SWA_WIDTH), jnp.bfloat16),
        compiler_params=_cparams(("parallel", "parallel")),
        name="swa",
    )(proj3, *karrs, *varrs, bias, sink_col)


def _t5_bias(rel, table):
    nb = REL_BUCKETS // 2
    max_exact = nb // 2
    side = jnp.where(rel > 0, nb, 0)
    n = jnp.abs(rel)
    n_f = jnp.maximum(n, max_exact).astype(jnp.float32)
    large = max_exact + (jnp.log(n_f / max_exact) / math.log(REL_MAX_DIST / max_exact)
                         * (nb - max_exact)).astype(jnp.int32)
    large = jnp.minimum(large, nb - 1)
    bucket = side + jnp.where(n < max_exact, n, large)
    return jnp.transpose(table[bucket].astype(jnp.float32), (2, 0, 1))


def _outproj_kernel(a1_ref, a2_ref, w_ref, x_ref, o_ref):
    k1 = a1_ref.shape[1]
    acc = jnp.dot(a1_ref[...], w_ref[0:k1, :], preferred_element_type=jnp.float32)
    acc += jnp.dot(a2_ref[...], w_ref[k1:, :], preferred_element_type=jnp.float32)
    o_ref[...] = x_ref[...] + acc


def outproj(a1, a2, w, x, tm=512, tn=2048):
    n, d = x.shape
    k1, k2 = a1.shape[1], a2.shape[1]
    tm = min(tm, n)
    assert n % tm == 0 and d % tn == 0
    return pl.pallas_call(
        _outproj_kernel,
        grid=(n // tm, d // tn),
        in_specs=[
            pl.BlockSpec((tm, k1), lambda i, j: (i, 0)),
            pl.BlockSpec((tm, k2), lambda i, j: (i, 0)),
            pl.BlockSpec((k1 + k2, tn), lambda i, j: (0, j)),
            pl.BlockSpec((tm, tn), lambda i, j: (i, j)),
        ],
        out_specs=pl.BlockSpec((tm, tn), lambda i, j: (i, j)),
        out_shape=jax.ShapeDtypeStruct((n, d), jnp.float32),
        compiler_params=_cparams(("parallel", "arbitrary")),
        name="outproj",
    )(a1, a2, w, x)


def _memattn_kernel(x_ref, nw_ref, wq_ref, mk_ref, mv_ref, wo_ref, o_ref):
    bf16 = jnp.bfloat16
    x = x_ref[0]
    h = _rms(x, nw_ref[...]).astype(bf16)
    q = jnp.dot(h, wq_ref[...], preferred_element_type=jnp.float32)
    mk = mk_ref[0].astype(bf16)
    mv = mv_ref[0].astype(bf16)
    outs = []
    for hh in range(MEM_HEADS):
        sl = slice(hh * MEM_HEAD_DIM, (hh + 1) * MEM_HEAD_DIM)
        logits = lax.dot_general(q[:, sl].astype(bf16), mk[:, sl], _NT,
                                 preferred_element_type=jnp.float32) * MEM_SCALE
        m = jnp.max(logits, axis=-1, keepdims=True)
        e = jnp.exp(logits - m)
        p = e / jnp.sum(e, axis=-1, keepdims=True)
        outs.append(jnp.dot(p.astype(bf16), mv[:, sl], preferred_element_type=jnp.float32))
    o = jnp.concatenate(outs, axis=-1).astype(bf16)
    o_ref[0] = x + jnp.dot(o, wo_ref[...], preferred_element_type=jnp.float32)


def memattn(x3, nw, wq, mk_arr, mv_arr, mk_col, mv_col, wo, tm=512):
    b, t, d = x3.shape
    tm = min(tm, t)
    assert t % tm == 0
    return pl.pallas_call(
        _memattn_kernel,
        grid=(b, t // tm),
        in_specs=[
            pl.BlockSpec((1, tm, d), lambda bi, i: (bi, i, 0)),
            pl.BlockSpec((1, d), lambda bi, i: (0, 0)),
            pl.BlockSpec((d, MEM_INNER), lambda bi, i: (0, 0)),
            pl.BlockSpec((1, N_MEM, MEM_INNER), lambda bi, i: (bi, 0, mk_col)),
            pl.BlockSpec((1, N_MEM, MEM_INNER), lambda bi, i: (bi, 0, mv_col)),
            pl.BlockSpec((MEM_INNER, d), lambda bi, i: (0, 0)),
        ],
        out_specs=pl.BlockSpec((1, tm, d), lambda bi, i: (bi, i, 0)),
        out_shape=jax.ShapeDtypeStruct((b, t, d), jnp.float32),
        compiler_params=_cparams(("parallel", "arbitrary")),
        name="memattn",
    )(x3, nw.reshape(1, d), wq, mk_arr, mv_arr, wo)


def _topk_step(s, row, payload=None):
    n = s.shape[0]
    m = jnp.max(s, axis=0, keepdims=True)
    idx = jnp.min(jnp.where(s == m, row, n), axis=0, keepdims=True)
    hit = row == idx
    pick = idx if payload is None else jnp.sum(jnp.where(hit, payload, 0), axis=0, keepdims=True)
    return m, pick, jnp.where(hit, -jnp.inf, s)


def _head_retrieval_pieces(q_ref, sk_ref, h, eid_out, gate_out, n_pieces):
    bf16 = jnp.bfloat16
    k = PEER_TOPK
    hk = k // 2
    static = isinstance(h, int)
    st = {}

    def reset(s, payload=None):
        st.update(s=s, payload=payload, vals=[], picks=[])

    def score(c):
        col = (h * 2 + c) * PEER_DHALF
        col = col if static else pl.multiple_of(col, PEER_DHALF)
        qh = q_ref[:, pl.ds(col, PEER_DHALF)].astype(bf16)
        reset(lax.dot_general(sk_ref[c, h], qh, _NT, preferred_element_type=jnp.float32))

    def extract(n_it):
        s = st["s"]
        row = lax.broadcasted_iota(jnp.int32, s.shape, 0)
        for _ in range(n_it):
            m, pick, s = _topk_step(s, row, st["payload"])
            st["vals"].append(m)
            st["picks"].append(pick)
        st["s"] = s

    def close(name):
        st[name] = (jnp.concatenate(st["vals"], axis=0), jnp.concatenate(st["picks"], axis=0))

    def candidates():
        (s1, i1), (s2, i2) = st["t0"], st["t1"]
        cand = jnp.concatenate(
            [s1[0:1] + s2] + [s1[a:a + 1] + s2[0:hk] for a in range(1, hk)] + [s1[hk:k] + s2[0:1]],
            axis=0)
        cidx = jnp.concatenate(
            [i1[0:1] * PEER_NKEYS + i2] + [i1[a:a + 1] * PEER_NKEYS + i2[0:hk] for a in range(1, hk)]
            + [i1[hk:k] * PEER_NKEYS + i2[0:1]], axis=0)
        reset(cand, cidx)

    def finish():
        close("t2")
        top_s, eid = st["t2"]
        e = jnp.exp(top_s - top_s[0:1])
        r0 = h * k if static else pl.multiple_of(h * k, k)
        eid_out[pl.ds(r0, k), :] = eid
        gate_out[pl.ds(r0, k), :] = e / jnp.sum(e, axis=0, keepdims=True)

    q4 = [lambda: extract(k // 4)] * 4
    work = ([lambda: score(0)] + q4 + [lambda: (close("t0"), score(1))] + q4
            + [lambda: (close("t1"), candidates())] + q4 + [finish])
    while len(work) > n_pieces:
        f0, f1 = work[0], work[1]
        work[0:2] = [lambda f0=f0, f1=f1: (f0(), f1())]
    return work


def _peer_retrieve_kernel(q_ref, sk_ref, eidx_ref, gate_ref, eid_t, gate_t):
    for h in range(PEER_HEADS):
        for piece in _head_retrieval_pieces(q_ref, sk_ref, h, eid_t, gate_t, 1):
            piece()
    eidx_ref[...] = eid_t[...].T
    gate_ref[...] = gate_t[...].T


def peer_retrieve(qp, subkeys, tn=128):
    n, d = qp.shape
    tn = min(tn, n)
    assert n % tn == 0
    return pl.pallas_call(
        _peer_retrieve_kernel,
        grid=(n // tn,),
        in_specs=[
            pl.BlockSpec((tn, d), lambda i: (i, 0)),
            pl.BlockSpec(subkeys.shape, lambda i: (0, 0, 0, 0)),
        ],
        out_specs=[
            pl.BlockSpec((tn, PEER_PAIRS), lambda i: (i, 0)),
            pl.BlockSpec((tn, PEER_PAIRS), lambda i: (i, 0)),
        ],
        out_shape=[
            jax.ShapeDtypeStruct((n, PEER_PAIRS), jnp.int32),
            jax.ShapeDtypeStruct((n, PEER_PAIRS), jnp.float32),
        ],
        scratch_shapes=[pltpu.VMEM((PEER_PAIRS, tn), jnp.int32),
                        pltpu.VMEM((PEER_PAIRS, tn), jnp.float32)],
        compiler_params=_cparams(("arbitrary",)),
        name="peer_retrieve",
    )(qp, subkeys)


PEER_SLOTS = 8
PEER_AHEAD = PEER_SLOTS - 2
PEER_GROUP = 16
PEER_DMA_QUEUES = 2
PEER_LANES = 128
PEER_EROWS = 16
PEER_PITCH = 24


def _even_odd_chunks(a):
    nc = a.shape[1] // PEER_LANES
    pick = lambda c: a[:, c * PEER_LANES:(c + 1) * PEER_LANES]
    return jnp.concatenate([pick(c) for c in range(0, nc, 2)] + [pick(c) for c in range(1, nc, 2)],
                           axis=1)


def _interleave_chunks(a):
    nc = a.shape[1] // PEER_LANES
    half = nc // 2
    pick = lambda c: a[:, c * PEER_LANES:(c + 1) * PEER_LANES]
    return jnp.concatenate([pick(c // 2 + (half if c % 2 else 0)) for c in range(nc)], axis=1)


def _peer_main_kernel(eidx0_ref, gate0_ref, qn_ref, sk_ref, x_ref, nw_ref, fw_ref, uv_hbm, o_ref,
                      h_ref, y_ref, dup_ref, w2_ref, eidx_ref, gate_ref, eid_t, gate_t,
                      eid_next, gate_next, *scratch):
    bf16 = jnp.bfloat16
    f32 = jnp.float32
    bufs, sem, csem = scratch[:PEER_SLOTS], scratch[PEER_SLOTS], scratch[PEER_SLOTS + 1]
    tb, d = x_ref.shape
    half = d // 2
    n_rows = PEER_EROWS // 2
    lanes2 = 2 * PEER_PAIRS

    @pl.when(pl.program_id(0) == 0)
    def _():
        eid_next[...] = eidx0_ref[...]
        gate_next[...] = gate0_ref[...]

    to_smem = pltpu.make_async_copy(eid_next, eidx_ref, csem)
    to_smem.start()
    gate_ref[...] = gate_next[...]
    h_ref[...] = _even_odd_chunks(_rms(x_ref[...], nw_ref[...])).astype(bf16).astype(f32)
    dup_ref[...] = (lax.broadcasted_iota(jnp.int32, (PEER_PAIRS, lanes2), 1) // 2
                    == lax.broadcasted_iota(jnp.int32, (PEER_PAIRS, lanes2), 0)).astype(bf16)
    to_smem.wait()

    rows = PEER_PAIRS * PEER_EROWS

    def issue(t, slot, part=0, parts=1):
        n_p = PEER_PAIRS // parts
        for p in range(part * n_p, (part + 1) * n_p):
            r = pl.multiple_of(eidx_ref[t, p] * PEER_EROWS, PEER_EROWS)
            pltpu.make_async_copy(uv_hbm.at[pl.ds(r, PEER_EROWS)],
                                  bufs[slot].at[pl.ds(p * PEER_PITCH, PEER_EROWS)],
                                  sem.at[slot]).start(priority=p % PEER_DMA_QUEUES)

    def wait(slot):
        pltpu.make_async_copy(uv_hbm.at[pl.ds(0, rows)], bufs[slot].at[pl.ds(0, rows)],
                              sem.at[slot]).wait()

    def words(slot, m):
        return bufs[slot][pl.ds(m, PEER_PAIRS, stride=PEER_PITCH), :]

    sub_w = lax.broadcasted_iota(jnp.int32, (8, lanes2), 0)
    par_w = lax.broadcasted_iota(jnp.int32, (8, lanes2), 1) % 2

    def u_side(t, slot, between=None):
        hrow = h_ref[pl.ds(t, 1), :]
        acc = None
        for m in range(n_rows):
            if between is not None:
                between(m)
            wd = words(slot, m)
            lo = lax.bitcast_convert_type(wd << 16, f32)
            hi = lax.bitcast_convert_type(wd & jnp.int32(-65536), f32)
            term = (lo * hrow[:, m * PEER_LANES:(m + 1) * PEER_LANES]
                    + hi * hrow[:, half + m * PEER_LANES:half + (m + 1) * PEER_LANES])
            acc = term if acc is None else acc + term
        act = jnp.sum(acc.T, axis=0, keepdims=True)
        gelu = 0.5 * act * (1.0 + lax.erf(act * np.float32(math.sqrt(0.5))))
        w = gate_ref[pl.ds(t, 1), :] * gelu
        wb = jnp.broadcast_to(w, (8, PEER_PAIRS)).astype(bf16)
        wd2 = jnp.dot(wb, dup_ref[...], preferred_element_type=f32)
        w2_ref[slot] = jnp.where(par_w == sub_w, wd2, 0.0)

    def v_side(t, slot):
        wv = jnp.concatenate([pltpu.bitcast(words(slot, n_rows + m), bf16)
                              for m in range(n_rows)], axis=1)
        y2 = jnp.dot(w2_ref[slot].astype(bf16), wv, preferred_element_type=f32)
        y_ref[pl.ds(t, 1), :] = jnp.concatenate([y2[0:1], y2[1:2]], axis=1)

    def step(t, j, do_issue=True):
        wait(j % PEER_SLOTS)
        spread = (lambda m: issue(t + PEER_AHEAD, (j + PEER_AHEAD) % PEER_SLOTS, m, n_rows)
                  ) if do_issue else None
        u_side(t, j % PEER_SLOTS, spread)
        v_side(t - 1, (j - 1) % PEER_SLOTS)

    for t in range(PEER_AHEAD):
        issue(t, t)
    wait(0)
    u_side(0, 0)
    issue(PEER_AHEAD, PEER_AHEAD)

    def body(g, carry):
        t0 = 1 + g * PEER_GROUP
        pieces = _head_retrieval_pieces(qn_ref, sk_ref, g, eid_t, gate_t, PEER_GROUP)
        for j in range(PEER_GROUP):
            step(t0 + j, 1 + j)
            pieces[j]()
        return carry

    n_main = (tb - PEER_AHEAD - 1) // PEER_GROUP
    assert n_main == PEER_HEADS - 1
    lax.fori_loop(0, n_main, body, 0)
    tail = range(1 + n_main * PEER_GROUP, tb)
    pieces = _head_retrieval_pieces(qn_ref, sk_ref, PEER_HEADS - 1, eid_t, gate_t, len(tail))
    for t, piece in zip(tail, pieces):
        step(t, t, do_issue=t + PEER_AHEAD < tb)
        piece()
    v_side(tb - 1, (tb - 1) % PEER_SLOTS)
    eid_next[...] = eid_t[...].T
    gate_next[...] = gate_t[...].T


    o_ref[...] = _rms(x_ref[...] + _interleave_chunks(y_ref[...]), fw_ref[...])


def make_peer_table(u_tab, v_tab):
    e = u_tab.shape[0]

    def pack(tab):
        bits = lax.bitcast_convert_type(tab.astype(jnp.bfloat16).astype(jnp.float32), jnp.uint32)
        bits = bits.reshape(e, PEER_EROWS // 2, 2 * PEER_LANES)
        return (bits[:, :, :PEER_LANES] >> 16) | (bits[:, :, PEER_LANES:] & np.uint32(0xFFFF0000))

    word = jnp.concatenate([pack(u_tab), pack(v_tab)], axis=1)
    return lax.bitcast_convert_type(word, jnp.int32).reshape(e * PEER_EROWS, PEER_LANES)


def peer_main(qp, subkeys, x, nw, fw, uv_tab, tb=128):
    n, d = x.shape
    tb = min(tb, n)
    assert n % tb == 0 and tb > PEER_AHEAD + PEER_SLOTS + 1 and PEER_GROUP % PEER_SLOTS == 0
    assert uv_tab.shape[1] == PEER_LANES and d == PEER_EROWS * PEER_LANES
    nblk = n // tb
    eidx0, gate0 = peer_retrieve(qp[:tb], subkeys, tn=tb)
    first = lambda i: (0, 0)
    return pl.pallas_call(
        _peer_main_kernel,
        grid=(nblk,),
        in_specs=[
            pl.BlockSpec((tb, PEER_PAIRS), first),
            pl.BlockSpec((tb, PEER_PAIRS), first),
            pl.BlockSpec((tb, qp.shape[1]), lambda i: (jnp.minimum(i + 1, nblk - 1), 0)),
            pl.BlockSpec(subkeys.shape, lambda i: (0, 0, 0, 0)),
            pl.BlockSpec((tb, d), lambda i: (i, 0)),
            pl.BlockSpec((1, d), first),
            pl.BlockSpec((1, d), first),
            pl.BlockSpec(memory_space=pl.ANY),
        ],
        out_specs=pl.BlockSpec((tb, d), lambda i: (i, 0)),
        out_shape=jax.ShapeDtypeStruct((n, d), jnp.float32),
        scratch_shapes=[
            pltpu.VMEM((tb, d), jnp.float32),
            pltpu.VMEM((tb, d), jnp.float32),
            pltpu.VMEM((PEER_PAIRS, 2 * PEER_PAIRS), jnp.bfloat16),
            pltpu.VMEM((PEER_SLOTS, 8, 2 * PEER_PAIRS), jnp.float32),
            pltpu.SMEM((tb, PEER_PAIRS), jnp.int32),
            pltpu.VMEM((tb, PEER_PAIRS), jnp.float32),
            pltpu.VMEM((PEER_PAIRS, tb), jnp.int32),
            pltpu.VMEM((PEER_PAIRS, tb), jnp.float32),
            pltpu.VMEM((tb, PEER_PAIRS), jnp.int32),
            pltpu.VMEM((tb, PEER_PAIRS), jnp.float32),
            *[pltpu.VMEM((PEER_PAIRS * PEER_PITCH, PEER_LANES), jnp.int32)
              for _ in range(PEER_SLOTS)],
            pltpu.SemaphoreType.DMA((PEER_SLOTS,)),
            pltpu.SemaphoreType.DMA(()),
        ],
        compiler_params=_cparams(("arbitrary",)),
        name="peer_main",
    )(eidx0, gate0, qp, subkeys, x, nw.reshape(1, d), fw.reshape(1, d), uv_tab)


def _trunk(x, mk_arr, mv_arr, mk_col, mv_col, s0, cache_k, cache_v, p):
    b, t, d = x.shape
    n = b * t
    proj = norm_matmul(x.reshape(n, d), p["norm_mix_w"], p["w_in"], tn=IN_WIDTH // 2)
    proj3 = proj.reshape(b, t, IN_WIDTH)
    o_hg, s_fin = hgrn(proj3, s0, p["lb"], p["hg_norm_w"])
    o_sw = swa(proj3, p["bias"], p["sink_col"], cache_k, cache_v)
    x1 = outproj(o_hg.reshape(n, HG_WIDTH), o_sw.reshape(n, SWA_WIDTH), p["w_out"], x.reshape(n, d))
    x2 = memattn(x1.reshape(b, t, d), p["norm_cross_w"], p["mem_wq"], mk_arr, mv_arr,
                 mk_col, mv_col, p["mem_wo"]).reshape(n, d)
    qp = norm_matmul(x2, p["norm_ffn_w"], p["peer_wq"], tn=2048)
    y = peer_main(qp, p["peer_subkeys"], x2, p["norm_ffn_w"], p["final_norm_w"], p["peer_uv"])
    keep = min(WINDOW, t) if cache_k is None else t
    k_rows = proj3[:, t - keep:, OFF_KSW:OFF_KSW + SWA_KV_WIDTH]
    v_rows = proj3[:, t - keep:, OFF_VSW:OFF_VSW + SWA_KV_WIDTH]
    k_rows = k_rows.reshape(b, keep, SWA_KV_HEADS, SWA_HEAD_DIM)
    v_rows = v_rows.reshape(b, keep, SWA_KV_HEADS, SWA_HEAD_DIM)
    return y.reshape(b, t, d), s_fin, k_rows, v_rows


def kernel(x_prompt, x_sample, mem_prompt, state_hgrn, cache_swa_k, cache_swa_v, cache_mem_k, cache_mem_v, rel_bias, hg_lb_logits, norm_mix_w, w_in, hg_norm_w, swa_sinks, w_out, norm_mem_w, norm_cross_w, mem_wq, mem_wk, mem_wv, mem_wo, norm_ffn_w, peer_wq, peer_subkeys, peer_u, peer_v, final_norm_w):
    bf16 = jnp.bfloat16
    depth = w_in.shape[0]
    assert depth == 1
    l = 0
    lb_all = jnp.cumsum(jax.nn.softmax(hg_lb_logits.astype(jnp.float32), axis=0), axis=0)
    n_keys = (WIN_CHUNKS + 1) * CHUNK
    rel = jnp.arange(n_keys)[None, :] - (WIN_CHUNKS * CHUNK + jnp.arange(CHUNK))[:, None]
    bias = _t5_bias(rel, rel_bias).reshape(SWA_KV_HEADS, 2, 2, CHUNK, n_keys)
    bias = bias.transpose(0, 2, 1, 3, 4).reshape(SWA_KV_HEADS, 2, 2 * CHUNK, n_keys)
    bias = jnp.pad(bias, ((0, 0), (0, 0), (0, 0), (0, CHUNK)))
    sink_col = jnp.broadcast_to(
        swa_sinks[l].astype(jnp.float32).reshape(SWA_KV_HEADS, 2, 2, 1),
        (SWA_KV_HEADS, 2, 2, CHUNK)).transpose(0, 2, 1, 3).reshape(SWA_KV_HEADS, 2, 2 * CHUNK, 1)
    p = {
        "norm_mix_w": norm_mix_w[l], "w_in": w_in[l].astype(bf16), "lb": lb_all[l],
        "hg_norm_w": hg_norm_w[l], "bias": bias, "sink_col": sink_col,
        "w_out": w_out[l].astype(bf16), "norm_cross_w": norm_cross_w[l],
        "mem_wq": mem_wq[l].astype(bf16), "mem_wo": mem_wo[l].astype(bf16),
        "norm_ffn_w": norm_ffn_w[l], "peer_wq": peer_wq[l].astype(bf16),
        "peer_subkeys": peer_subkeys[l].astype(bf16), "peer_uv": make_peer_table(peer_u[l], peer_v[l]),
        "final_norm_w": final_norm_w,
    }
    bp, tp, d = x_prompt.shape
    bs, ts, _ = x_sample.shape

    wkv = jnp.concatenate([mem_wk[l], mem_wv[l]], axis=1).astype(bf16)
    kv = norm_matmul(mem_prompt.reshape(bp * N_MEM, d), norm_mem_w[l], wkv)
    kv3 = kv.reshape(bp, N_MEM, 2 * MEM_INNER)
    mk = kv3[:, :, :MEM_INNER].reshape(bp, N_MEM, MEM_HEADS, MEM_HEAD_DIM)
    mv = kv3[:, :, MEM_INNER:].reshape(bp, N_MEM, MEM_HEADS, MEM_HEAD_DIM)

    s0 = jnp.zeros((bp, HG_HEADS, HG_DK, HG_DV), jnp.float32)
    yp, sp, kp, vp = _trunk(x_prompt, kv3, kv3, 0, 1, s0, None, None, p)

    cmk = cache_mem_k[l].reshape(bs, N_MEM, MEM_INNER)
    cmv = cache_mem_v[l].reshape(bs, N_MEM, MEM_INNER)
    ck = cache_swa_k[l].reshape(bs, -1, SWA_KV_WIDTH)
    cv = cache_swa_v[l].reshape(bs, -1, SWA_KV_WIDTH)
    ys, ss, ks_new, vs_new = _trunk(x_sample, cmk, cmv, 0, 0, state_hgrn[l], ck, cv, p)

    return (yp, ys, sp[None], kp[None], vp[None], mk[None], mv[None],
            ss[None], ks_new[None], vs_new[None])
```

```python
import functools
import math

import numpy as np
import jax
import jax.numpy as jnp
from jax import lax
from jax.experimental import pallas as pl
from jax.experimental.pallas import tpu as pltpu

EPS = 1e-6
NEG_INF = -1e30
CHUNK = 64

HG_HEADS = 8
HG_DK = 128
HG_DV = 128
HG_WIDTH = HG_HEADS * HG_DV
HG_SUB = 16
HG_CHUNK = 64
HG_HPS = 8

SWA_HEADS = 16
SWA_KV_HEADS = 4
SWA_GROUP = SWA_HEADS // SWA_KV_HEADS
SWA_HEAD_DIM = 64
SWA_WIDTH = SWA_HEADS * SWA_HEAD_DIM
SWA_KV_WIDTH = SWA_KV_HEADS * SWA_HEAD_DIM
SWA_SCALE = SWA_HEAD_DIM ** -0.5
WINDOW = 128
WIN_CHUNKS = WINDOW // CHUNK
REL_BUCKETS = 32
REL_MAX_DIST = 128

N_MEM = 256
MEM_HEADS = 4
MEM_HEAD_DIM = 128
MEM_INNER = MEM_HEADS * MEM_HEAD_DIM
MEM_SCALE = MEM_HEAD_DIM ** -0.5

PEER_HEADS = 8
PEER_NKEYS = 128
PEER_DHALF = 128
PEER_TOPK = 16
PEER_PAIRS = PEER_HEADS * PEER_TOPK

OFF_Q, OFF_F, OFF_I, OFF_G = 0, 1024, 2048, 3072
OFF_QSW, OFF_KSW, OFF_VSW = 4096, 5120, 5376
IN_WIDTH = 5632

VMEM_LIMIT_BYTES = 56 * 1024 * 1024

_NT = (((1,), (1,)), ((), ()))
_TN = (((0,), (0,)), ((), ()))


def _cparams(sem):
    return pltpu.CompilerParams(dimension_semantics=sem, vmem_limit_bytes=VMEM_LIMIT_BYTES)


def _rms(x, w):
    return x * lax.rsqrt(jnp.mean(x * x, axis=-1, keepdims=True) + EPS) * w


def _norm_matmul_kernel(x_ref, nw_ref, w_ref, o_ref, h_ref):
    @pl.when(pl.program_id(1) == 0)
    def _():
        h_ref[...] = _rms(x_ref[...], nw_ref[...]).astype(h_ref.dtype)

    o_ref[...] = jnp.dot(h_ref[...], w_ref[...], preferred_element_type=jnp.float32)


def norm_matmul(x, nw, w, tm=512, tn=512):
    n, d = x.shape
    m = w.shape[1]
    tm = min(tm, n)
    tn = min(tn, m)
    assert n % tm == 0 and m % tn == 0
    return pl.pallas_call(
        _norm_matmul_kernel,
        grid=(n // tm, m // tn),
        in_specs=[
            pl.BlockSpec((tm, d), lambda i, j: (i, 0)),
            pl.BlockSpec((1, d), lambda i, j: (0, 0)),
            pl.BlockSpec((d, tn), lambda i, j: (0, j)),
        ],
        out_specs=pl.BlockSpec((tm, tn), lambda i, j: (i, j)),
        out_shape=jax.ShapeDtypeStruct((n, m), jnp.float32),
        scratch_shapes=[pltpu.VMEM((tm, d), jnp.bfloat16)],
        compiler_params=_cparams(("parallel", "arbitrary")),
        name="norm_matmul",
    )(x, nw.reshape(1, d), w)


def _cumsum_rows(x):
    n = x.shape[0]
    row = lax.broadcasted_iota(jnp.int32, x.shape, 0)
    s = 1
    while s < n:
        x = x + jnp.where(row >= s, pltpu.roll(x, s, axis=0), 0.0)
        s *= 2
    return x


def _bcast_rows(x, idxs, g):
    return jnp.concatenate(
        [jnp.broadcast_to(x[i:i + 1, :], (g, x.shape[1])) for i in idxs], axis=0)


def _hgrn_chunk(q, fpre, v, lb, st):
    c = q.shape[0]
    f = lb + (1.0 - lb) * jax.nn.sigmoid(fpre)
    k = 1.0 - f
    lf = jnp.log(f)
    bc = _cumsum_rows(lf)
    be = bc - lf
    bf16 = jnp.bfloat16
    ti = lax.broadcasted_iota(jnp.int32, (c, c), 0)
    si = lax.broadcasted_iota(jnp.int32, (c, c), 1)

    ref0 = _bcast_rows(be, range(0, c, HG_SUB), HG_SUB)
    qd = (q * jnp.exp(bc - ref0)).astype(bf16)
    kd = (k * jnp.exp(ref0 - bc)).astype(bf16)
    a = lax.dot_general(qd, kd, _NT, preferred_element_type=jnp.float32)
    attn = jnp.where((ti // HG_SUB == si // HG_SUB) & (si <= ti), a, 0.0)
    g = 2 * HG_SUB
    while g <= c:
        half = g // 2
        ref = _bcast_rows(bc, range(half - 1, c, g), g)
        ql = (q * jnp.exp(jnp.minimum(bc - ref, 0.0))).astype(bf16)
        kl = (k * jnp.exp(jnp.minimum(ref - bc, 0.0))).astype(bf16)
        a = lax.dot_general(ql, kl, _NT, preferred_element_type=jnp.float32)
        m = (ti // g == si // g) & (ti % g >= half) & (si % g < half)
        attn = jnp.where(m, a, attn)
        g *= 2

    vb = v.astype(bf16)
    q_in = (q * jnp.exp(bc)).astype(bf16)
    o = (lax.dot_general(q_in, st.astype(bf16), _NT, preferred_element_type=jnp.float32)
         + jnp.dot(attn.astype(bf16), vb, preferred_element_type=jnp.float32))
    b_last = bc[c - 1:c, :]
    k_out = (k * jnp.exp(b_last - bc)).astype(bf16)
    st_new = st * jnp.exp(b_last) + lax.dot_general(vb, k_out, _TN,
                                                    preferred_element_type=jnp.float32)
    return o, st_new


def _hgrn_kernel(q_ref, f_ref, i_ref, g_ref, s0_ref, lb_ref, nw_ref, o_ref, sfin_ref, st_ref):
    ci = pl.program_id(2)

    @pl.when(ci == 0)
    def _():
        for hh in range(HG_HPS):
            st_ref[hh] = s0_ref[0, hh].T

    ct = q_ref.shape[1]
    nw = nw_ref[...]
    for j in range(ct // HG_CHUNK):
        sl = pl.ds(j * HG_CHUNK, HG_CHUNK)
        for hh in range(HG_HPS):
            hs = slice(hh * HG_DK, (hh + 1) * HG_DK)
            o, st_new = _hgrn_chunk(q_ref[0, sl, hs], f_ref[0, sl, hs], i_ref[0, sl, hs],
                                    lb_ref[:, hs], st_ref[hh])
            st_ref[hh] = st_new
            o = _rms(o, nw)
            o_ref[0, sl, hs] = (o * jax.nn.silu(g_ref[0, sl, hs])).astype(o_ref.dtype)

    @pl.when(ci == pl.num_programs(2) - 1)
    def _():
        for hh in range(HG_HPS):
            sfin_ref[0, hh] = st_ref[hh].T


def hgrn(proj3, s0, lb, nw):
    b, t, _ = proj3.shape
    ct = min(256, t)
    assert t % ct == 0 and ct % HG_CHUNK == 0 and HG_HEADS % HG_HPS == 0
    w = HG_HPS * HG_DK

    def col(off):
        return pl.BlockSpec((1, ct, w), lambda bi, h, c: (bi, c, off // w + h))

    return pl.pallas_call(
        _hgrn_kernel,
        grid=(b, HG_HEADS // HG_HPS, t // ct),
        in_specs=[
            col(OFF_Q), col(OFF_F), col(OFF_I), col(OFF_G),
            pl.BlockSpec((1, HG_HPS, HG_DK, HG_DV), lambda bi, h, c: (bi, h, 0, 0)),
            pl.BlockSpec((1, w), lambda bi, h, c: (0, h)),
            pl.BlockSpec((1, HG_DV), lambda bi, h, c: (0, 0)),
        ],
        out_specs=[
            pl.BlockSpec((1, ct, w), lambda bi, h, c: (bi, c, h)),
            pl.BlockSpec((1, HG_HPS, HG_DK, HG_DV), lambda bi, h, c: (bi, h, 0, 0)),
        ],
        out_shape=[
            jax.ShapeDtypeStruct((b, t, HG_WIDTH), jnp.bfloat16),
            jax.ShapeDtypeStruct((b, HG_HEADS, HG_DK, HG_DV), jnp.float32),
        ],
        scratch_shapes=[pltpu.VMEM((HG_HPS, HG_DV, HG_DK), jnp.float32)],
        compiler_params=_cparams(("parallel", "parallel", "arbitrary")),
        name="hgrn",
    )(proj3, proj3, proj3, proj3, s0, lb.reshape(1, -1), nw.reshape(1, -1))


def _swa_kernel(q_ref, k2_ref, k1_ref, k0_ref, v2_ref, v1_ref, v0_ref, bias_ref, sink_ref,
                o_ref, *, c_off):
    bf16 = jnp.bfloat16
    c = pl.program_id(1) + c_off
    hd = SWA_HEAD_DIM
    pad = jnp.zeros_like(k0_ref[0])
    kk = jnp.concatenate([k2_ref[0], k1_ref[0], k0_ref[0], pad], axis=0)
    vv = jnp.concatenate([v2_ref[0], v1_ref[0], v0_ref[0], pad], axis=0).astype(bf16)
    kt = kk.T.astype(bf16)
    n_keys = kk.shape[0]
    key_chunk = lax.broadcasted_iota(jnp.int32, (1, n_keys), 1) // CHUNK
    valid = ((c - WIN_CHUNKS + key_chunk) >= 0) & (key_chunk <= WIN_CHUNKS)
    zk = jnp.zeros((hd, n_keys), bf16)
    lane = lax.broadcasted_iota(jnp.int32, (n_keys, 2 * hd), 1)
    q = q_ref[0].astype(bf16)
    for kv in range(SWA_KV_HEADS):
        ktj = kt[kv * hd:(kv + 1) * hd, :]
        kt_lo = jnp.concatenate([ktj, zk], axis=0)
        kt_hi = jnp.concatenate([zk, ktj], axis=0)
        tile = vv[:, (kv // 2) * 2 * hd:(kv // 2 + 1) * 2 * hd]
        own_lo = kv % 2 == 0
        keep = (lane < hd) if own_lo else (lane >= hd)
        v_own = jnp.where(keep, tile, jnp.zeros_like(tile))
        v_swap = pltpu.roll(v_own.astype(jnp.float32), hd, axis=1).astype(bf16)
        v_lo, v_hi = (v_own, v_swap) if own_lo else (v_swap, v_own)
        c0 = kv * SWA_GROUP * hd
        qs = jnp.concatenate([q[:, c0:c0 + 2 * hd], q[:, c0 + 2 * hd:c0 + 4 * hd]], axis=0)
        probs = []
        for i, ktm in enumerate((kt_lo, kt_hi)):
            logits = jnp.dot(qs, ktm, preferred_element_type=jnp.float32)
            logits = logits * SWA_SCALE + bias_ref[kv, i]
            logits = jnp.where(valid, logits, NEG_INF)
            sink = sink_ref[kv, i]
            m = jnp.maximum(jnp.max(logits, axis=-1, keepdims=True), sink)
            e = jnp.exp(logits - m)
            p = e / (jnp.sum(e, axis=-1, keepdims=True) + jnp.exp(sink - m))
            probs.append(p.astype(bf16))
        o = (jnp.dot(probs[0], v_lo, preferred_element_type=jnp.float32)
             + jnp.dot(probs[1], v_hi, preferred_element_type=jnp.float32))
        o_ref[0, :, c0:c0 + 2 * hd] = o[0:CHUNK].astype(o_ref.dtype)
        o_ref[0, :, c0 + 2 * hd:c0 + 4 * hd] = o[CHUNK:2 * CHUNK].astype(o_ref.dtype)


def swa(proj3, bias, sink_col, cache_k=None, cache_v=None):
    b, t, _ = proj3.shape
    nc = t // CHUNK
    assert t % CHUNK == 0
    qspec = pl.BlockSpec((1, CHUNK, SWA_WIDTH), lambda bi, c: (bi, c, OFF_QSW // SWA_WIDTH))

    def cur(off):
        return pl.BlockSpec((1, CHUNK, SWA_KV_WIDTH), lambda bi, c: (bi, c, off // SWA_KV_WIDTH))

    if cache_k is None:
        def prev(off, back):
            return pl.BlockSpec((1, CHUNK, SWA_KV_WIDTH),
                                lambda bi, c: (bi, jnp.maximum(c - back, 0), off // SWA_KV_WIDTH))
        kspecs = [prev(OFF_KSW, 2), prev(OFF_KSW, 1), cur(OFF_KSW)]
        vspecs = [prev(OFF_VSW, 2), prev(OFF_VSW, 1), cur(OFF_VSW)]
        karrs = [proj3, proj3, proj3]
        varrs = [proj3, proj3, proj3]
        c_off = 0
    else:
        assert nc == 1 and cache_k.shape[1] == WINDOW

        def past(j):
            return pl.BlockSpec((1, CHUNK, SWA_KV_WIDTH), lambda bi, c: (bi, j, 0))
        kspecs = [past(0), past(1), cur(OFF_KSW)]
        vspecs = [past(0), past(1), cur(OFF_VSW)]
        karrs = [cache_k, cache_k, proj3]
        varrs = [cache_v, cache_v, proj3]
        c_off = WIN_CHUNKS
    rows = 2 * CHUNK
    n_keys = (WIN_CHUNKS + 2) * CHUNK
    return pl.pallas_call(
        functools.partial(_swa_kernel, c_off=c_off),
        grid=(b, nc),
        in_specs=[qspec] + kspecs + vspecs + [
            pl.BlockSpec((SWA_KV_HEADS, 2, rows, n_keys), lambda bi, c: (0, 0, 0, 0)),
            pl.BlockSpec((SWA_KV_HEADS, 2, rows, 1), lambda bi, c: (0, 0, 0, 0)),
        ],
        out_specs=pl.BlockSpec((1, CHUNK, SWA_WIDTH), lambda bi, c: (bi, c, 0)),
        out_shape=jax.ShapeDtypeStruct((b, t, SWA_WIDTH), jnp.bfloat16),
        compiler_params=_cparams(("parallel", "parallel")),
        name="swa",
    )(proj3, *karrs, *varrs, bias, sink_col)


def _t5_bias(rel, table):
    nb = REL_BUCKETS // 2
    max_exact = nb // 2
    side = jnp.where(rel > 0, nb, 0)
    n = jnp.abs(rel)
    n_f = jnp.maximum(n, max_exact).astype(jnp.float32)
    large = max_exact + (jnp.log(n_f / max_exact) / math.log(REL_MAX_DIST / max_exact)
                         * (nb - max_exact)).astype(jnp.int32)
    large = jnp.minimum(large, nb - 1)
    bucket = side + jnp.where(n < max_exact, n, large)
    return jnp.transpose(table[bucket].astype(jnp.float32), (2, 0, 1))


def _outproj_kernel(a1_ref, a2_ref, w_ref, x_ref, o_ref):
    k1 = a1_ref.shape[1]
    acc = jnp.dot(a1_ref[...], w_ref[0:k1, :], preferred_element_type=jnp.float32)
    acc += jnp.dot(a2_ref[...], w_ref[k1:, :], preferred_element_type=jnp.float32)
    o_ref[...] = x_ref[...] + acc


def outproj(a1, a2, w, x, tm=512, tn=2048):
    n, d = x.shape
    k1, k2 = a1.shape[1], a2.shape[1]
    tm = min(tm, n)
    assert n % tm == 0 and d % tn == 0
    return pl.pallas_call(
        _outproj_kernel,
        grid=(n // tm, d // tn),
        in_specs=[
            pl.BlockSpec((tm, k1), lambda i, j: (i, 0)),
            pl.BlockSpec((tm, k2), lambda i, j: (i, 0)),
            pl.BlockSpec((k1 + k2, tn), lambda i, j: (0, j)),
            pl.BlockSpec((tm, tn), lambda i, j: (i, j)),
        ],
        out_specs=pl.BlockSpec((tm, tn), lambda i, j: (i, j)),
        out_shape=jax.ShapeDtypeStruct((n, d), jnp.float32),
        compiler_params=_cparams(("parallel", "arbitrary")),
        name="outproj",
    )(a1, a2, w, x)


def _memattn_kernel(x_ref, nw_ref, wq_ref, mk_ref, mv_ref, wo_ref, o_ref):
    bf16 = jnp.bfloat16
    x = x_ref[0]
    h = _rms(x, nw_ref[...]).astype(bf16)
    q = jnp.dot(h, wq_ref[...], preferred_element_type=jnp.float32)
    mk = mk_ref[0].astype(bf16)
    mv = mv_ref[0].astype(bf16)
    outs = []
    for hh in range(MEM_HEADS):
        sl = slice(hh * MEM_HEAD_DIM, (hh + 1) * MEM_HEAD_DIM)
        logits = lax.dot_general(q[:, sl].astype(bf16), mk[:, sl], _NT,
                                 preferred_element_type=jnp.float32) * MEM_SCALE
        m = jnp.max(logits, axis=-1, keepdims=True)
        e = jnp.exp(logits - m)
        p = e / jnp.sum(e, axis=-1, keepdims=True)
        outs.append(jnp.dot(p.astype(bf16), mv[:, sl], preferred_element_type=jnp.float32))
    o = jnp.concatenate(outs, axis=-1).astype(bf16)
    o_ref[0] = x + jnp.dot(o, wo_ref[...], preferred_element_type=jnp.float32)


def memattn(x3, nw, wq, mk_arr, mv_arr, mk_col, mv_col, wo, tm=512):
    b, t, d = x3.shape
    tm = min(tm, t)
    assert t % tm == 0
    return pl.pallas_call(
        _memattn_kernel,
        grid=(b, t // tm),
        in_specs=[
            pl.BlockSpec((1, tm, d), lambda bi, i: (bi, i, 0)),
            pl.BlockSpec((1, d), lambda bi, i: (0, 0)),
            pl.BlockSpec((d, MEM_INNER), lambda bi, i: (0, 0)),
            pl.BlockSpec((1, N_MEM, MEM_INNER), lambda bi, i: (bi, 0, mk_col)),
            pl.BlockSpec((1, N_MEM, MEM_INNER), lambda bi, i: (bi, 0, mv_col)),
            pl.BlockSpec((MEM_INNER, d), lambda bi, i: (0, 0)),
        ],
        out_specs=pl.BlockSpec((1, tm, d), lambda bi, i: (bi, i, 0)),
        out_shape=jax.ShapeDtypeStruct((b, t, d), jnp.float32),
        compiler_params=_cparams(("parallel", "arbitrary")),
        name="memattn",
    )(x3, nw.reshape(1, d), wq, mk_arr, mv_arr, wo)


def _topk_step(s, row, payload=None):
    n = s.shape[0]
    m = jnp.max(s, axis=0, keepdims=True)
    idx = jnp.min(jnp.where(s == m, row, n), axis=0, keepdims=True)
    hit = row == idx
    pick = idx if payload is None else jnp.sum(jnp.where(hit, payload, 0), axis=0, keepdims=True)
    return m, pick, jnp.where(hit, -jnp.inf, s)


def _head_retrieval_pieces(q_ref, sk_ref, h, eid_out, gate_out, n_pieces):
    bf16 = jnp.bfloat16
    k = PEER_TOPK
    hk = k // 2
    static = isinstance(h, int)
    st = {}

    def reset(s, payload=None):
        st.update(s=s, payload=payload, vals=[], picks=[])

    def score(c):
        col = (h * 2 + c) * PEER_DHALF
        col = col if static else pl.multiple_of(col, PEER_DHALF)
        qh = q_ref[:, pl.ds(col, PEER_DHALF)].astype(bf16)
        reset(lax.dot_general(sk_ref[c, h], qh, _NT, preferred_element_type=jnp.float32))

    def extract(n_it):
        s = st["s"]
        row = lax.broadcasted_iota(jnp.int32, s.shape, 0)
        for _ in range(n_it):
            m, pick, s = _topk_step(s, row, st["payload"])
            st["vals"].append(m)
            st["picks"].append(pick)
        st["s"] = s

    def close(name):
        st[name] = (jnp.concatenate(st["vals"], axis=0), jnp.concatenate(st["picks"], axis=0))

    def candidates():
        (s1, i1), (s2, i2) = st["t0"], st["t1"]
        cand = jnp.concatenate(
            [s1[0:1] + s2] + [s1[a:a + 1] + s2[0:hk] for a in range(1, hk)] + [s1[hk:k] + s2[0:1]],
            axis=0)
        cidx = jnp.concatenate(
            [i1[0:1] * PEER_NKEYS + i2] + [i1[a:a + 1] * PEER_NKEYS + i2[0:hk] for a in range(1, hk)]
            + [i1[hk:k] * PEER_NKEYS + i2[0:1]], axis=0)
        reset(cand, cidx)

    def finish():
        close("t2")
        top_s, eid = st["t2"]
        e = jnp.exp(top_s - top_s[0:1])
        r0 = h * k if static else pl.multiple_of(h * k, k)
        eid_out[pl.ds(r0, k), :] = eid
        gate_out[pl.ds(r0, k), :] = e / jnp.sum(e, axis=0, keepdims=True)

    q4 = [lambda: extract(k // 4)] * 4
    work = ([lambda: score(0)] + q4 + [lambda: (close("t0"), score(1))] + q4
            + [lambda: (close("t1"), candidates())] + q4 + [finish])
    while len(work) > n_pieces:
        f0, f1 = work[0], work[1]
        work[0:2] = [lambda f0=f0, f1=f1: (f0(), f1())]
    return work


def _peer_retrieve_kernel(q_ref, sk_ref, eidx_ref, gate_ref, eid_t, gate_t):
    for h in range(PEER_HEADS):
        for piece in _head_retrieval_pieces(q_ref, sk_ref, h, eid_t, gate_t, 1):
            piece()
    eidx_ref[...] = eid_t[...].T
    gate_ref[...] = gate_t[...].T


def peer_retrieve(qp, subkeys, tn=128):
    n, d = qp.shape
    tn = min(tn, n)
    assert n % tn == 0
    return pl.pallas_call(
        _peer_retrieve_kernel,
        grid=(n // tn,),
        in_specs=[
            pl.BlockSpec((tn, d), lambda i: (i, 0)),
            pl.BlockSpec(subkeys.shape, lambda i: (0, 0, 0, 0)),
        ],
        out_specs=[
            pl.BlockSpec((tn, PEER_PAIRS), lambda i: (i, 0)),
            pl.BlockSpec((tn, PEER_PAIRS), lambda i: (i, 0)),
        ],
        out_shape=[
            jax.ShapeDtypeStruct((n, PEER_PAIRS), jnp.int32),
            jax.ShapeDtypeStruct((n, PEER_PAIRS), jnp.float32),
        ],
        scratch_shapes=[pltpu.VMEM((PEER_PAIRS, tn), jnp.int32),
                        pltpu.VMEM((PEER_PAIRS, tn), jnp.float32)],
        compiler_params=_cparams(("arbitrary",)),
        name="peer_retrieve",
    )(qp, subkeys)


PEER_SLOTS = 8
PEER_AHEAD = PEER_SLOTS - 2
PEER_GROUP = 16
PEER_DMA_QUEUES = 2
PEER_LANES = 128
PEER_EROWS = 16
PEER_PITCH = 24


def _even_odd_chunks(a):
    nc = a.shape[1] // PEER_LANES
    pick = lambda c: a[:, c * PEER_LANES:(c + 1) * PEER_LANES]
    return jnp.concatenate([pick(c) for c in range(0, nc, 2)] + [pick(c) for c in range(1, nc, 2)],
                           axis=1)


def _interleave_chunks(a):
    nc = a.shape[1] // PEER_LANES
    half = nc // 2
    pick = lambda c: a[:, c * PEER_LANES:(c + 1) * PEER_LANES]
    return jnp.concatenate([pick(c // 2 + (half if c % 2 else 0)) for c in range(nc)], axis=1)


def _peer_main_kernel(eidx0_ref, gate0_ref, qn_ref, sk_ref, x_ref, nw_ref, fw_ref, uv_hbm, o_ref,
                      h_ref, y_ref, dup_ref, w2_ref, eidx_ref, gate_ref, eid_t, gate_t,
                      eid_next, gate_next, *scratch):
    bf16 = jnp.bfloat16
    f32 = jnp.float32
    bufs, sem, csem = scratch[:PEER_SLOTS], scratch[PEER_SLOTS], scratch[PEER_SLOTS + 1]
    tb, d = x_ref.shape
    half = d // 2
    n_rows = PEER_EROWS // 2
    lanes2 = 2 * PEER_PAIRS

    @pl.when(pl.program_id(0) == 0)
    def _():
        eid_next[...] = eidx0_ref[...]
        gate_next[...] = gate0_ref[...]

    to_smem = pltpu.make_async_copy(eid_next, eidx_ref, csem)
    to_smem.start()
    gate_ref[...] = gate_next[...]
    h_ref[...] = _even_odd_chunks(_rms(x_ref[...], nw_ref[...])).astype(bf16).astype(f32)
    dup_ref[...] = (lax.broadcasted_iota(jnp.int32, (PEER_PAIRS, lanes2), 1) // 2
                    == lax.broadcasted_iota(jnp.int32, (PEER_PAIRS, lanes2), 0)).astype(bf16)
    to_smem.wait()

    rows = PEER_PAIRS * PEER_EROWS

    def issue(t, slot, part=0, parts=1):
        n_p = PEER_PAIRS // parts
        for p in range(part * n_p, (part + 1) * n_p):
            r = pl.multiple_of(eidx_ref[t, p] * PEER_EROWS, PEER_EROWS)
            pltpu.make_async_copy(uv_hbm.at[pl.ds(r, PEER_EROWS)],
                                  bufs[slot].at[pl.ds(p * PEER_PITCH, PEER_EROWS)],
                                  sem.at[slot]).start(priority=p % PEER_DMA_QUEUES)

    def wait(slot):
        pltpu.make_async_copy(uv_hbm.at[pl.ds(0, rows)], bufs[slot].at[pl.ds(0, rows)],
                              sem.at[slot]).wait()

    def words(slot, m):
        return bufs[slot][pl.ds(m, PEER_PAIRS, stride=PEER_PITCH), :]

    sub_w = lax.broadcasted_iota(jnp.int32, (8, lanes2), 0)
    par_w = lax.broadcasted_iota(jnp.int32, (8, lanes2), 1) % 2

    def u_side(t, slot, between=None):
        hrow = h_ref[pl.ds(t, 1), :]
        acc = None
        for m in range(n_rows):
            if between is not None:
                between(m)
            wd = words(slot, m)
            lo = lax.bitcast_convert_type(wd << 16, f32)
            hi = lax.bitcast_convert_type(wd & jnp.int32(-65536), f32)
            term = (lo * hrow[:, m * PEER_LANES:(m + 1) * PEER_LANES]
                    + hi * hrow[:, half + m * PEER_LANES:half + (m + 1) * PEER_LANES])
            acc = term if acc is None else acc + term
        act = jnp.sum(acc.T, axis=0, keepdims=True)
        gelu = 0.5 * act * (1.0 + lax.erf(act * np.float32(math.sqrt(0.5))))
        w = gate_ref[pl.ds(t, 1), :] * gelu
        wb = jnp.broadcast_to(w, (8, PEER_PAIRS)).astype(bf16)
        wd2 = jnp.dot(wb, dup_ref[...], preferred_element_type=f32)
        w2_ref[slot] = jnp.where(par_w == sub_w, wd2, 0.0)

    def v_side(t, slot):
        wv = jnp.concatenate([pltpu.bitcast(words(slot, n_rows + m), bf16)
                              for m in range(n_rows)], axis=1)
        y2 = jnp.dot(w2_ref[slot].astype(bf16), wv, preferred_element_type=f32)
        y_ref[pl.ds(t, 1), :] = jnp.concatenate([y2[0:1], y2[1:2]], axis=1)

    def step(t, j, do_issue=True):
        wait(j % PEER_SLOTS)
        spread = (lambda m: issue(t + PEER_AHEAD, (j + PEER_AHEAD) % PEER_SLOTS, m, n_rows)
                  ) if do_issue else None
        u_side(t, j % PEER_SLOTS, spread)
        v_side(t - 1, (j - 1) % PEER_SLOTS)

    for t in range(PEER_AHEAD):
        issue(t, t)
    wait(0)
    u_side(0, 0)
    issue(PEER_AHEAD, PEER_AHEAD)

    def body(g, carry):
        t0 = 1 + g * PEER_GROUP
        pieces = _head_retrieval_pieces(qn_ref, sk_ref, g, eid_t, gate_t, PEER_GROUP)
        for j in range(PEER_GROUP):
            step(t0 + j, 1 + j)
            pieces[j]()
        return carry

    n_main = (tb - PEER_AHEAD - 1) // PEER_GROUP
    assert n_main == PEER_HEADS - 1
    lax.fori_loop(0, n_main, body, 0)
    tail = range(1 + n_main * PEER_GROUP, tb)
    pieces = _head_retrieval_pieces(qn_ref, sk_ref, PEER_HEADS - 1, eid_t, gate_t, len(tail))
    for t, piece in zip(tail, pieces):
        step(t, t, do_issue=t + PEER_AHEAD < tb)
        piece()
    v_side(tb - 1, (tb - 1) % PEER_SLOTS)
    eid_next[...] = eid_t[...].T
    gate_next[...] = gate_t[...].T


    o_ref[...] = _rms(x_ref[...] + _interleave_chunks(y_ref[...]), fw_ref[...])


def make_peer_table(u_tab, v_tab):
    e, d = u_tab.shape
    te = 256
    assert e % te == 0 and d == PEER_EROWS * PEER_LANES
    return pl.pallas_call(
        _pack_table_kernel,
        grid=(e // te,),
        in_specs=[pl.BlockSpec((te, d), lambda i: (i, 0)), pl.BlockSpec((te, d), lambda i: (i, 0))],
        out_specs=pl.BlockSpec((te * PEER_EROWS, PEER_LANES), lambda i: (i, 0)),
        out_shape=jax.ShapeDtypeStruct((e * PEER_EROWS, PEER_LANES), jnp.int32),
        compiler_params=_cparams(("arbitrary",)),
        name="peer_pack",
    )(u_tab, v_tab)


def _pack_table_kernel(u_ref, v_ref, o_ref):
    te = u_ref.shape[0]
    half = PEER_EROWS // 2

    def bf16_bits(x):
        return lax.bitcast_convert_type(x.astype(jnp.bfloat16).astype(jnp.float32), jnp.int32)

    for src, m0 in ((u_ref, 0), (v_ref, half)):
        for m in range(half):
            lo = bf16_bits(src[:, (2 * m) * PEER_LANES:(2 * m + 1) * PEER_LANES])
            hi = bf16_bits(src[:, (2 * m + 1) * PEER_LANES:(2 * m + 2) * PEER_LANES])
            word = lax.shift_right_logical(lo, jnp.int32(16)) | (hi & jnp.int32(-65536))
            o_ref[pl.ds(m0 + m, te, stride=PEER_EROWS), :] = word


def peer_main(qp, subkeys, x, nw, fw, uv_tab, tb=128):
    n, d = x.shape
    tb = min(tb, n)
    assert n % tb == 0 and tb > PEER_AHEAD + PEER_SLOTS + 1 and PEER_GROUP % PEER_SLOTS == 0
    assert uv_tab.shape[1] == PEER_LANES and d == PEER_EROWS * PEER_LANES
    nblk = n // tb
    eidx0, gate0 = peer_retrieve(qp[:tb], subkeys, tn=tb)
    first = lambda i: (0, 0)
    return pl.pallas_call(
        _peer_main_kernel,
        grid=(nblk,),
        in_specs=[
            pl.BlockSpec((tb, PEER_PAIRS), first),
            pl.BlockSpec((tb, PEER_PAIRS), first),
            pl.BlockSpec((tb, qp.shape[1]), lambda i: (jnp.minimum(i + 1, nblk - 1), 0)),
            pl.BlockSpec(subkeys.shape, lambda i: (0, 0, 0, 0)),
            pl.BlockSpec((tb, d), lambda i: (i, 0)),
            pl.BlockSpec((1, d), first),
            pl.BlockSpec((1, d), first),
            pl.BlockSpec(memory_space=pl.ANY),
        ],
        out_specs=pl.BlockSpec((tb, d), lambda i: (i, 0)),
        out_shape=jax.ShapeDtypeStruct((n, d), jnp.float32),
        scratch_shapes=[
            pltpu.VMEM((tb, d), jnp.float32),
            pltpu.VMEM((tb, d), jnp.float32),
            pltpu.VMEM((PEER_PAIRS, 2 * PEER_PAIRS), jnp.bfloat16),
            pltpu.VMEM((PEER_SLOTS, 8, 2 * PEER_PAIRS), jnp.float32),
            pltpu.SMEM((tb, PEER_PAIRS), jnp.int32),
            pltpu.VMEM((tb, PEER_PAIRS), jnp.float32),
            pltpu.VMEM((PEER_PAIRS, tb), jnp.int32),
            pltpu.VMEM((PEER_PAIRS, tb), jnp.float32),
            pltpu.VMEM((tb, PEER_PAIRS), jnp.int32),
            pltpu.VMEM((tb, PEER_PAIRS), jnp.float32),
            *[pltpu.VMEM((PEER_PAIRS * PEER_PITCH, PEER_LANES), jnp.int32)
              for _ in range(PEER_SLOTS)],
            pltpu.SemaphoreType.DMA((PEER_SLOTS,)),
            pltpu.SemaphoreType.DMA(()),
        ],
        compiler_params=_cparams(("arbitrary",)),
        name="peer_main",
    )(eidx0, gate0, qp, subkeys, x, nw.reshape(1, d), fw.reshape(1, d), uv_tab)


def _trunk(x, mk_arr, mv_arr, mk_col, mv_col, s0, cache_k, cache_v, p):
    b, t, d = x.shape
    n = b * t
    proj = norm_matmul(x.reshape(n, d), p["norm_mix_w"], p["w_in"], tn=IN_WIDTH // 2)
    proj3 = proj.reshape(b, t, IN_WIDTH)
    o_hg, s_fin = hgrn(proj3, s0, p["lb"], p["hg_norm_w"])
    o_sw = swa(proj3, p["bias"], p["sink_col"], cache_k, cache_v)
    x1 = outproj(o_hg.reshape(n, HG_WIDTH), o_sw.reshape(n, SWA_WIDTH), p["w_out"], x.reshape(n, d))
    x2 = memattn(x1.reshape(b, t, d), p["norm_cross_w"], p["mem_wq"], mk_arr, mv_arr,
                 mk_col, mv_col, p["mem_wo"]).reshape(n, d)
    qp = norm_matmul(x2, p["norm_ffn_w"], p["peer_wq"], tn=2048)
    y = peer_main(qp, p["peer_subkeys"], x2, p["norm_ffn_w"], p["final_norm_w"], p["peer_uv"])
    keep = min(WINDOW, t) if cache_k is None else t
    k_rows = proj3[:, t - keep:, OFF_KSW:OFF_KSW + SWA_KV_WIDTH]
    v_rows = proj3[:, t - keep:, OFF_VSW:OFF_VSW + SWA_KV_WIDTH]
    k_rows = k_rows.reshape(b, keep, SWA_KV_HEADS, SWA_HEAD_DIM)
    v_rows = v_rows.reshape(b, keep, SWA_KV_HEADS, SWA_HEAD_DIM)
    return y.reshape(b, t, d), s_fin, k_rows, v_rows


def kernel(x_prompt, x_sample, mem_prompt, state_hgrn, cache_swa_k, cache_swa_v, cache_mem_k, cache_mem_v, rel_bias, hg_lb_logits, norm_mix_w, w_in, hg_norm_w, swa_sinks, w_out, norm_mem_w, norm_cross_w, mem_wq, mem_wk, mem_wv, mem_wo, norm_ffn_w, peer_wq, peer_subkeys, peer_u, peer_v, final_norm_w):
    bf16 = jnp.bfloat16
    depth = w_in.shape[0]
    assert depth == 1
    l = 0
    lb_all = jnp.cumsum(jax.nn.softmax(hg_lb_logits.astype(jnp.float32), axis=0), axis=0)
    n_keys = (WIN_CHUNKS + 1) * CHUNK
    rel = jnp.arange(n_keys)[None, :] - (WIN_CHUNKS * CHUNK + jnp.arange(CHUNK))[:, None]
    bias = _t5_bias(rel, rel_bias).reshape(SWA_KV_HEADS, 2, 2, CHUNK, n_keys)
    bias = bias.transpose(0, 2, 1, 3, 4).reshape(SWA_KV_HEADS, 2, 2 * CHUNK, n_keys)
    bias = jnp.pad(bias, ((0, 0), (0, 0), (0, 0), (0, CHUNK)))
    sink_col = jnp.broadcast_to(
        swa_sinks[l].astype(jnp.float32).reshape(SWA_KV_HEADS, 2, 2, 1),
        (SWA_KV_HEADS, 2, 2, CHUNK)).transpose(0, 2, 1, 3).reshape(SWA_KV_HEADS, 2, 2 * CHUNK, 1)
    p = {
        "norm_mix_w": norm_mix_w[l], "w_in": w_in[l].astype(bf16), "lb": lb_all[l],
        "hg_norm_w": hg_norm_w[l], "bias": bias, "sink_col": sink_col,
        "w_out": w_out[l].astype(bf16), "norm_cross_w": norm_cross_w[l],
        "mem_wq": mem_wq[l].astype(bf16), "mem_wo": mem_wo[l].astype(bf16),
        "norm_ffn_w": norm_ffn_w[l], "peer_wq": peer_wq[l].astype(bf16),
        "peer_subkeys": peer_subkeys[l].astype(bf16), "peer_uv": make_peer_table(peer_u[l], peer_v[l]),
        "final_norm_w": final_norm_w,
    }
    bp, tp, d = x_prompt.shape
    bs, ts, _ = x_sample.shape

    wkv = jnp.concatenate([mem_wk[l], mem_wv[l]], axis=1).astype(bf16)
    kv = norm_matmul(mem_prompt.reshape(bp * N_MEM, d), norm_mem_w[l], wkv)
    kv3 = kv.reshape(bp, N_MEM, 2 * MEM_INNER)
    mk = kv3[:, :, :MEM_INNER].reshape(bp, N_MEM, MEM_HEADS, MEM_HEAD_DIM)
    mv = kv3[:, :, MEM_INNER:].reshape(bp, N_MEM, MEM_HEADS, MEM_HEAD_DIM)

    s0 = jnp.zeros((bp, HG_HEADS, HG_DK, HG_DV), jnp.float32)
    yp, sp, kp, vp = _trunk(x_prompt, kv3, kv3, 0, 1, s0, None, None, p)

    cmk = cache_mem_k[l].reshape(bs, N_MEM, MEM_INNER)
    cmv = cache_mem_v[l].reshape(bs, N_MEM, MEM_INNER)
    ck = cache_swa_k[l].reshape(bs, -1, SWA_KV_WIDTH)
    cv = cache_swa_v[l].reshape(bs, -1, SWA_KV_WIDTH)
    ys, ss, ks_new, vs_new = _trunk(x_sample, cmk, cmv, 0, 0, state_hgrn[l], ck, cv, p)

    return (yp, ys, sp[None], kp[None], vp[None], mk[None], mv[None],
            ss[None], ks_new[None], vs_new[None])
```

```python
import functools
import math

import numpy as np
import jax
import jax.numpy as jnp
from jax import lax
from jax.experimental import pallas as pl
from jax.experimental.pallas import tpu as pltpu

EPS = 1e-6
NEG_INF = -1e30
CHUNK = 64

HG_HEADS = 8
HG_DK = 128
HG_DV = 128
HG_WIDTH = HG_HEADS * HG_DV
HG_SUB = 16
HG_CHUNK = 64
HG_HPS = 8

SWA_HEADS = 16
SWA_KV_HEADS = 4
SWA_GROUP = SWA_HEADS // SWA_KV_HEADS
SWA_HEAD_DIM = 64
SWA_WIDTH = SWA_HEADS * SWA_HEAD_DIM
SWA_KV_WIDTH = SWA_KV_HEADS * SWA_HEAD_DIM
SWA_SCALE = SWA_HEAD_DIM ** -0.5
WINDOW = 128
WIN_CHUNKS = WINDOW // CHUNK
REL_BUCKETS = 32
REL_MAX_DIST = 128

N_MEM = 256
MEM_HEADS = 4
MEM_HEAD_DIM = 128
MEM_INNER = MEM_HEADS * MEM_HEAD_DIM
MEM_SCALE = MEM_HEAD_DIM ** -0.5

PEER_HEADS = 8
PEER_NKEYS = 128
PEER_DHALF = 128
PEER_TOPK = 16
PEER_PAIRS = PEER_HEADS * PEER_TOPK

OFF_Q, OFF_F, OFF_I, OFF_G = 0, 1024, 2048, 3072
OFF_QSW, OFF_KSW, OFF_VSW = 4096, 5120, 5376
IN_WIDTH = 5632

VMEM_LIMIT_BYTES = 56 * 1024 * 1024

_NT = (((1,), (1,)), ((), ()))
_TN = (((0,), (0,)), ((), ()))


def _cparams(sem):
    return pltpu.CompilerParams(dimension_semantics=sem, vmem_limit_bytes=VMEM_LIMIT_BYTES)


def _rms(x, w):
    return x * lax.rsqrt(jnp.mean(x * x, axis=-1, keepdims=True) + EPS) * w


def _norm_matmul_kernel(x_ref, nw_ref, w_ref, o_ref, h_ref):
    @pl.when(pl.program_id(1) == 0)
    def _():
        h_ref[...] = _rms(x_ref[...], nw_ref[...]).astype(h_ref.dtype)

    o_ref[...] = jnp.dot(h_ref[...], w_ref[...], preferred_element_type=jnp.float32)


def norm_matmul(x, nw, w, tm=512, tn=512):
    n, d = x.shape
    m = w.shape[1]
    tm = min(tm, n)
    tn = min(tn, m)
    assert n % tm == 0 and m % tn == 0
    return pl.pallas_call(
        _norm_matmul_kernel,
        grid=(n // tm, m // tn),
        in_specs=[
            pl.BlockSpec((tm, d), lambda i, j: (i, 0)),
            pl.BlockSpec((1, d), lambda i, j: (0, 0)),
            pl.BlockSpec((d, tn), lambda i, j: (0, j)),
        ],
        out_specs=pl.BlockSpec((tm, tn), lambda i, j: (i, j)),
        out_shape=jax.ShapeDtypeStruct((n, m), jnp.float32),
        scratch_shapes=[pltpu.VMEM((tm, d), jnp.bfloat16)],
        compiler_params=_cparams(("parallel", "arbitrary")),
        name="norm_matmul",
    )(x, nw.reshape(1, d), w)


def _cumsum_rows(x):
    n = x.shape[0]
    row = lax.broadcasted_iota(jnp.int32, x.shape, 0)
    s = 1
    while s < n:
        x = x + jnp.where(row >= s, pltpu.roll(x, s, axis=0), 0.0)
        s *= 2
    return x


def _bcast_rows(x, idxs, g):
    return jnp.concatenate(
        [jnp.broadcast_to(x[i:i + 1, :], (g, x.shape[1])) for i in idxs], axis=0)


def _hgrn_chunk(q, fpre, v, lb, st):
    c = q.shape[0]
    f = lb + (1.0 - lb) * jax.nn.sigmoid(fpre)
    k = 1.0 - f
    lf = jnp.log(f)
    bc = _cumsum_rows(lf)
    be = bc - lf
    bf16 = jnp.bfloat16
    ti = lax.broadcasted_iota(jnp.int32, (c, c), 0)
    si = lax.broadcasted_iota(jnp.int32, (c, c), 1)

    ref0 = _bcast_rows(be, range(0, c, HG_SUB), HG_SUB)
    qd = (q * jnp.exp(bc - ref0)).astype(bf16)
    kd = (k * jnp.exp(ref0 - bc)).astype(bf16)
    a = lax.dot_general(qd, kd, _NT, preferred_element_type=jnp.float32)
    attn = jnp.where((ti // HG_SUB == si // HG_SUB) & (si <= ti), a, 0.0)
    g = 2 * HG_SUB
    while g <= c:
        half = g // 2
        ref = _bcast_rows(bc, range(half - 1, c, g), g)
        ql = (q * jnp.exp(jnp.minimum(bc - ref, 0.0))).astype(bf16)
        kl = (k * jnp.exp(jnp.minimum(ref - bc, 0.0))).astype(bf16)
        a = lax.dot_general(ql, kl, _NT, preferred_element_type=jnp.float32)
        m = (ti // g == si // g) & (ti % g >= half) & (si % g < half)
        attn = jnp.where(m, a, attn)
        g *= 2

    vb = v.astype(bf16)
    q_in = (q * jnp.exp(bc)).astype(bf16)
    o = (lax.dot_general(q_in, st.astype(bf16), _NT, preferred_element_type=jnp.float32)
         + jnp.dot(attn.astype(bf16), vb, preferred_element_type=jnp.float32))
    b_last = bc[c - 1:c, :]
    k_out = (k * jnp.exp(b_last - bc)).astype(bf16)
    st_new = st * jnp.exp(b_last) + lax.dot_general(vb, k_out, _TN,
                                                    preferred_element_type=jnp.float32)
    return o, st_new


def _hgrn_kernel(q_ref, f_ref, i_ref, g_ref, s0_ref, lb_ref, nw_ref, o_ref, sfin_ref, st_ref):
    ci = pl.program_id(2)

    @pl.when(ci == 0)
    def _():
        for hh in range(HG_HPS):
            st_ref[hh] = s0_ref[0, hh].T

    ct = q_ref.shape[1]
    nw = nw_ref[...]
    for j in range(ct // HG_CHUNK):
        sl = pl.ds(j * HG_CHUNK, HG_CHUNK)
        for hh in range(HG_HPS):
            hs = slice(hh * HG_DK, (hh + 1) * HG_DK)
            o, st_new = _hgrn_chunk(q_ref[0, sl, hs], f_ref[0, sl, hs], i_ref[0, sl, hs],
                                    lb_ref[:, hs], st_ref[hh])
            st_ref[hh] = st_new
            o = _rms(o, nw)
            o_ref[0, sl, hs] = (o * jax.nn.silu(g_ref[0, sl, hs])).astype(o_ref.dtype)

    @pl.when(ci == pl.num_programs(2) - 1)
    def _():
        for hh in range(HG_HPS):
            sfin_ref[0, hh] = st_ref[hh].T


def hgrn(proj3, s0, lb, nw):
    b, t, _ = proj3.shape
    ct = min(256, t)
    assert t % ct == 0 and ct % HG_CHUNK == 0 and HG_HEADS % HG_HPS == 0
    w = HG_HPS * HG_DK

    def col(off):
        return pl.BlockSpec((1, ct, w), lambda bi, h, c: (bi, c, off // w + h))

    return pl.pallas_call(
        _hgrn_kernel,
        grid=(b, HG_HEADS // HG_HPS, t // ct),
        in_specs=[
            col(OFF_Q), col(OFF_F), col(OFF_I), col(OFF_G),
            pl.BlockSpec((1, HG_HPS, HG_DK, HG_DV), lambda bi, h, c: (bi, h, 0, 0)),
            pl.BlockSpec((1, w), lambda bi, h, c: (0, h)),
            pl.BlockSpec((1, HG_DV), lambda bi, h, c: (0, 0)),
        ],
        out_specs=[
            pl.BlockSpec((1, ct, w), lambda bi, h, c: (bi, c, h)),
            pl.BlockSpec((1, HG_HPS, HG_DK, HG_DV), lambda bi, h, c: (bi, h, 0, 0)),
        ],
        out_shape=[
            jax.ShapeDtypeStruct((b, t, HG_WIDTH), jnp.bfloat16),
            jax.ShapeDtypeStruct((b, HG_HEADS, HG_DK, HG_DV), jnp.float32),
        ],
        scratch_shapes=[pltpu.VMEM((HG_HPS, HG_DV, HG_DK), jnp.float32)],
        compiler_params=_cparams(("parallel", "parallel", "arbitrary")),
        name="hgrn",
    )(proj3, proj3, proj3, proj3, s0, lb.reshape(1, -1), nw.reshape(1, -1))


def _swa_kernel(q_ref, k2_ref, k1_ref, k0_ref, v2_ref, v1_ref, v0_ref, bias_ref, sink_ref,
                o_ref, *, c_off):
    bf16 = jnp.bfloat16
    c = pl.program_id(1) + c_off
    hd = SWA_HEAD_DIM
    pad = jnp.zeros_like(k0_ref[0])
    kk = jnp.concatenate([k2_ref[0], k1_ref[0], k0_ref[0], pad], axis=0)
    vv = jnp.concatenate([v2_ref[0], v1_ref[0], v0_ref[0], pad], axis=0).astype(bf16)
    kt = kk.T.astype(bf16)
    n_keys = kk.shape[0]
    key_chunk = lax.broadcasted_iota(jnp.int32, (1, n_keys), 1) // CHUNK
    valid = ((c - WIN_CHUNKS + key_chunk) >= 0) & (key_chunk <= WIN_CHUNKS)
    zk = jnp.zeros((hd, n_keys), bf16)
    lane = lax.broadcasted_iota(jnp.int32, (n_keys, 2 * hd), 1)
    q = q_ref[0].astype(bf16)
    for kv in range(SWA_KV_HEADS):
        ktj = kt[kv * hd:(kv + 1) * hd, :]
        kt_lo = jnp.concatenate([ktj, zk], axis=0)
        kt_hi = jnp.concatenate([zk, ktj], axis=0)
        tile = vv[:, (kv // 2) * 2 * hd:(kv // 2 + 1) * 2 * hd]
        own_lo = kv % 2 == 0
        keep = (lane < hd) if own_lo else (lane >= hd)
        v_own = jnp.where(keep, tile, jnp.zeros_like(tile))
        v_swap = pltpu.roll(v_own.astype(jnp.float32), hd, axis=1).astype(bf16)
        v_lo, v_hi = (v_own, v_swap) if own_lo else (v_swap, v_own)
        c0 = kv * SWA_GROUP * hd
        qs = jnp.concatenate([q[:, c0:c0 + 2 * hd], q[:, c0 + 2 * hd:c0 + 4 * hd]], axis=0)
        probs = []
        for i, ktm in enumerate((kt_lo, kt_hi)):
            logits = jnp.dot(qs, ktm, preferred_element_type=jnp.float32)
            logits = logits * SWA_SCALE + bias_ref[kv, i]
            logits = jnp.where(valid, logits, NEG_INF)
            sink = sink_ref[kv, i]
            m = jnp.maximum(jnp.max(logits, axis=-1, keepdims=True), sink)
            e = jnp.exp(logits - m)
            p = e / (jnp.sum(e, axis=-1, keepdims=True) + jnp.exp(sink - m))
            probs.append(p.astype(bf16))
        o = (jnp.dot(probs[0], v_lo, preferred_element_type=jnp.float32)
             + jnp.dot(probs[1], v_hi, preferred_element_type=jnp.float32))
        o_ref[0, :, c0:c0 + 2 * hd] = o[0:CHUNK].astype(o_ref.dtype)
        o_ref[0, :, c0 + 2 * hd:c0 + 4 * hd] = o[CHUNK:2 * CHUNK].astype(o_ref.dtype)


def swa(proj3, bias, sink_col, cache_k=None, cache_v=None):
    b, t, _ = proj3.shape
    nc = t // CHUNK
    assert t % CHUNK == 0
    qspec = pl.BlockSpec((1, CHUNK, SWA_WIDTH), lambda bi, c: (bi, c, OFF_QSW // SWA_WIDTH))

    def cur(off):
        return pl.BlockSpec((1, CHUNK, SWA_KV_WIDTH), lambda bi, c: (bi, c, off // SWA_KV_WIDTH))

    if cache_k is None:
        def prev(off, back):
            return pl.BlockSpec((1, CHUNK, SWA_KV_WIDTH),
                                lambda bi, c: (bi, jnp.maximum(c - back, 0), off // SWA_KV_WIDTH))
        kspecs = [prev(OFF_KSW, 2), prev(OFF_KSW, 1), cur(OFF_KSW)]
        vspecs = [prev(OFF_VSW, 2), prev(OFF_VSW, 1), cur(OFF_VSW)]
        karrs = [proj3, proj3, proj3]
        varrs = [proj3, proj3, proj3]
        c_off = 0
    else:
        assert nc == 1 and cache_k.shape[1] == WINDOW

        def past(j):
            return pl.BlockSpec((1, CHUNK, SWA_KV_WIDTH), lambda bi, c: (bi, j, 0))
        kspecs = [past(0), past(1), cur(OFF_KSW)]
        vspecs = [past(0), past(1), cur(OFF_VSW)]
        karrs = [cache_k, cache_k, proj3]
        varrs = [cache_v, cache_v, proj3]
        c_off = WIN_CHUNKS
    rows = 2 * CHUNK
    n_keys = (WIN_CHUNKS + 2) * CHUNK
    return pl.pallas_call(
        functools.partial(_swa_kernel, c_off=c_off),
        grid=(b, nc),
        in_specs=[qspec] + kspecs + vspecs + [
            pl.BlockSpec((SWA_KV_HEADS, 2, rows, n_keys), lambda bi, c: (0, 0, 0, 0)),
            pl.BlockSpec((SWA_KV_HEADS, 2, rows, 1), lambda bi, c: (0, 0, 0, 0)),
        ],
        out_specs=pl.BlockSpec((1, CHUNK, SWA_WIDTH), lambda bi, c: (bi, c, 0)),
        out_shape=jax.ShapeDtypeStruct((b, t, SWA_WIDTH), jnp.bfloat16),
        compiler_params=_cparams(("parallel", "parallel")),
        name="swa",
    )(proj3, *karrs, *varrs, bias, sink_col)


def _t5_bias(rel, table):
    nb = REL_BUCKETS // 2
    max_exact = nb // 2
    side = jnp.where(rel > 0, nb, 0)
    n = jnp.abs(rel)
    n_f = jnp.maximum(n, max_exact).astype(jnp.float32)
    large = max_exact + (jnp.log(n_f / max_exact) / math.log(REL_MAX_DIST / max_exact)
                         * (nb - max_exact)).astype(jnp.int32)
    large = jnp.minimum(large, nb - 1)
    bucket = side + jnp.where(n < max_exact, n, large)
    return jnp.transpose(table[bucket].astype(jnp.float32), (2, 0, 1))


def _outproj_kernel(a1_ref, a2_ref, w_ref, x_ref, o_ref):
    k1 = a1_ref.shape[1]
    acc = jnp.dot(a1_ref[...], w_ref[0:k1, :], preferred_element_type=jnp.float32)
    acc += jnp.dot(a2_ref[...], w_ref[k1:, :], preferred_element_type=jnp.float32)
    o_ref[...] = x_ref[...] + acc


def outproj(a1, a2, w, x, tm=512, tn=2048):
    n, d = x.shape
    k1, k2 = a1.shape[1], a2.shape[1]
    tm = min(tm, n)
    assert n % tm == 0 and d % tn == 0
    return pl.pallas_call(
        _outproj_kernel,
        grid=(n // tm, d // tn),
        in_specs=[
            pl.BlockSpec((tm, k1), lambda i, j: (i, 0)),
            pl.BlockSpec((tm, k2), lambda i, j: (i, 0)),
            pl.BlockSpec((k1 + k2, tn), lambda i, j: (0, j)),
            pl.BlockSpec((tm, tn), lambda i, j: (i, j)),
        ],
        out_specs=pl.BlockSpec((tm, tn), lambda i, j: (i, j)),
        out_shape=jax.ShapeDtypeStruct((n, d), jnp.float32),
        compiler_params=_cparams(("parallel", "arbitrary")),
        name="outproj",
    )(a1, a2, w, x)


def _memattn_kernel(x_ref, nw_ref, wq_ref, mk_ref, mv_ref, wo_ref, o_ref):
    bf16 = jnp.bfloat16
    x = x_ref[0]
    h = _rms(x, nw_ref[...]).astype(bf16)
    q = jnp.dot(h, wq_ref[...], preferred_element_type=jnp.float32)
    mk = mk_ref[0].astype(bf16)
    mv = mv_ref[0].astype(bf16)
    outs = []
    for hh in range(MEM_HEADS):
        sl = slice(hh * MEM_HEAD_DIM, (hh + 1) * MEM_HEAD_DIM)
        logits = lax.dot_general(q[:, sl].astype(bf16), mk[:, sl], _NT,
                                 preferred_element_type=jnp.float32) * MEM_SCALE
        m = jnp.max(logits, axis=-1, keepdims=True)
        e = jnp.exp(logits - m)
        p = e / jnp.sum(e, axis=-1, keepdims=True)
        outs.append(jnp.dot(p.astype(bf16), mv[:, sl], preferred_element_type=jnp.float32))
    o = jnp.concatenate(outs, axis=-1).astype(bf16)
    o_ref[0] = x + jnp.dot(o, wo_ref[...], preferred_element_type=jnp.float32)


def memattn(x3, nw, wq, mk_arr, mv_arr, mk_col, mv_col, wo, tm=512):
    b, t, d = x3.shape
    tm = min(tm, t)
    assert t % tm == 0
    return pl.pallas_call(
        _memattn_kernel,
        grid=(b, t // tm),
        in_specs=[
            pl.BlockSpec((1, tm, d), lambda bi, i: (bi, i, 0)),
            pl.BlockSpec((1, d), lambda bi, i: (0, 0)),
            pl.BlockSpec((d, MEM_INNER), lambda bi, i: (0, 0)),
            pl.BlockSpec((1, N_MEM, MEM_INNER), lambda bi, i: (bi, 0, mk_col)),
            pl.BlockSpec((1, N_MEM, MEM_INNER), lambda bi, i: (bi, 0, mv_col)),
            pl.BlockSpec((MEM_INNER, d), lambda bi, i: (0, 0)),
        ],
        out_specs=pl.BlockSpec((1, tm, d), lambda bi, i: (bi, i, 0)),
        out_shape=jax.ShapeDtypeStruct((b, t, d), jnp.float32),
        compiler_params=_cparams(("parallel", "arbitrary")),
        name="memattn",
    )(x3, nw.reshape(1, d), wq, mk_arr, mv_arr, wo)


def _topk_step(s, row, payload=None):
    n = s.shape[0]
    m = jnp.max(s, axis=0, keepdims=True)
    idx = jnp.min(jnp.where(s == m, row, n), axis=0, keepdims=True)
    hit = row == idx
    pick = idx if payload is None else jnp.sum(jnp.where(hit, payload, 0), axis=0, keepdims=True)
    return m, pick, jnp.where(hit, -jnp.inf, s)


def _head_retrieval_pieces(q_ref, sk_ref, h, eid_out, gate_out, n_pieces):
    bf16 = jnp.bfloat16
    k = PEER_TOPK
    hk = k // 2
    static = isinstance(h, int)
    st = {}

    def reset(s, payload=None):
        st.update(s=s, payload=payload, vals=[], picks=[])

    def score(c):
        col = (h * 2 + c) * PEER_DHALF
        col = col if static else pl.multiple_of(col, PEER_DHALF)
        qh = q_ref[:, pl.ds(col, PEER_DHALF)].astype(bf16)
        reset(lax.dot_general(sk_ref[c, h], qh, _NT, preferred_element_type=jnp.float32))

    def extract(n_it):
        s = st["s"]
        row = lax.broadcasted_iota(jnp.int32, s.shape, 0)
        for _ in range(n_it):
            m, pick, s = _topk_step(s, row, st["payload"])
            st["vals"].append(m)
            st["picks"].append(pick)
        st["s"] = s

    def close(name):
        st[name] = (jnp.concatenate(st["vals"], axis=0), jnp.concatenate(st["picks"], axis=0))

    def candidates():
        (s1, i1), (s2, i2) = st["t0"], st["t1"]
        cand = jnp.concatenate(
            [s1[0:1] + s2] + [s1[a:a + 1] + s2[0:hk] for a in range(1, hk)] + [s1[hk:k] + s2[0:1]],
            axis=0)
        cidx = jnp.concatenate(
            [i1[0:1] * PEER_NKEYS + i2] + [i1[a:a + 1] * PEER_NKEYS + i2[0:hk] for a in range(1, hk)]
            + [i1[hk:k] * PEER_NKEYS + i2[0:1]], axis=0)
        reset(cand, cidx)

    def finish():
        close("t2")
        top_s, eid = st["t2"]
        e = jnp.exp(top_s - top_s[0:1])
        r0 = h * k if static else pl.multiple_of(h * k, k)
        eid_out[pl.ds(r0, k), :] = eid
        gate_out[pl.ds(r0, k), :] = e / jnp.sum(e, axis=0, keepdims=True)

    q4 = [lambda: extract(k // 4)] * 4
    work = ([lambda: score(0)] + q4 + [lambda: (close("t0"), score(1))] + q4
            + [lambda: (close("t1"), candidates())] + q4 + [finish])
    while len(work) > n_pieces:
        f0, f1 = work[0], work[1]
        work[0:2] = [lambda f0=f0, f1=f1: (f0(), f1())]
    return work


def _peer_retrieve_kernel(q_ref, sk_ref, eidx_ref, gate_ref, eid_t, gate_t):
    for h in range(PEER_HEADS):
        for piece in _head_retrieval_pieces(q_ref, sk_ref, h, eid_t, gate_t, 1):
            piece()
    eidx_ref[...] = eid_t[...].T
    gate_ref[...] = gate_t[...].T


def peer_retrieve(qp, subkeys, tn=128):
    n, d = qp.shape
    tn = min(tn, n)
    assert n % tn == 0
    return pl.pallas_call(
        _peer_retrieve_kernel,
        grid=(n // tn,),
        in_specs=[
            pl.BlockSpec((tn, d), lambda i: (i, 0)),
            pl.BlockSpec(subkeys.shape, lambda i: (0, 0, 0, 0)),
        ],
        out_specs=[
            pl.BlockSpec((tn, PEER_PAIRS), lambda i: (i, 0)),
            pl.BlockSpec((tn, PEER_PAIRS), lambda i: (i, 0)),
        ],
        out_shape=[
            jax.ShapeDtypeStruct((n, PEER_PAIRS), jnp.int32),
            jax.ShapeDtypeStruct((n, PEER_PAIRS), jnp.float32),
        ],
        scratch_shapes=[pltpu.VMEM((PEER_PAIRS, tn), jnp.int32),
                        pltpu.VMEM((PEER_PAIRS, tn), jnp.float32)],
        compiler_params=_cparams(("arbitrary",)),
        name="peer_retrieve",
    )(qp, subkeys)


PEER_SLOTS = 8
PEER_AHEAD = PEER_SLOTS - 2
PEER_GROUP = 16
PEER_DMA_QUEUES = 2
PEER_LANES = 128
PEER_EROWS = 16
PEER_PITCH = 24


def _even_odd_chunks(a):
    nc = a.shape[1] // PEER_LANES
    pick = lambda c: a[:, c * PEER_LANES:(c + 1) * PEER_LANES]
    return jnp.concatenate([pick(c) for c in range(0, nc, 2)] + [pick(c) for c in range(1, nc, 2)],
                           axis=1)


def _interleave_chunks(a):
    nc = a.shape[1] // PEER_LANES
    half = nc // 2
    pick = lambda c: a[:, c * PEER_LANES:(c + 1) * PEER_LANES]
    return jnp.concatenate([pick(c // 2 + (half if c % 2 else 0)) for c in range(nc)], axis=1)


def _peer_main_kernel(eidx0_ref, gate0_ref, qn_ref, sk_ref, x_ref, nw_ref, fw_ref, uv_hbm, o_ref,
                      h_ref, y_ref, dup_ref, w2_ref, eidx_ref, eid_t, gate_t,
                      eid_cur, gate_cur, eid_next, gate_next, *scratch):
    bf16 = jnp.bfloat16
    f32 = jnp.float32
    bufs, sem, csem = scratch[:PEER_SLOTS], scratch[PEER_SLOTS], scratch[PEER_SLOTS + 1]
    tb, d = x_ref.shape
    half = d // 2
    n_rows = PEER_EROWS // 2
    lanes2 = 2 * PEER_PAIRS

    step_i = pl.program_id(0)

    @pl.when(step_i == 0)
    def _():
        eid_cur[...] = eidx0_ref[pl.ds(0, tb), :]
        gate_cur[...] = gate0_ref[pl.ds(0, tb), :]
        eid_next[...] = eidx0_ref[pl.ds(tb, tb), :]
        gate_next[...] = gate0_ref[pl.ds(tb, tb), :]

    to_smem = (pltpu.make_async_copy(eid_cur, eidx_ref.at[pl.ds(0, tb)], csem.at[0]),
               pltpu.make_async_copy(eid_next, eidx_ref.at[pl.ds(tb, tb)], csem.at[1]))
    for cp in to_smem:
        cp.start()
    h_ref[...] = _even_odd_chunks(_rms(x_ref[...], nw_ref[...])).astype(bf16).astype(f32)
    dup_ref[...] = (lax.broadcasted_iota(jnp.int32, (PEER_PAIRS, lanes2), 1) // 2
                    == lax.broadcasted_iota(jnp.int32, (PEER_PAIRS, lanes2), 0)).astype(bf16)
    for cp in to_smem:
        cp.wait()

    rows = PEER_PAIRS * PEER_EROWS

    def issue(t, slot, part=0, parts=1, blk=0):
        n_p = PEER_PAIRS // parts
        row = t if blk == 0 else tb + t
        for p in range(part * n_p, (part + 1) * n_p):
            r = pl.multiple_of(eidx_ref[row, p] * PEER_EROWS, PEER_EROWS)
            pltpu.make_async_copy(uv_hbm.at[pl.ds(r, PEER_EROWS)],
                                  bufs[slot].at[pl.ds(p * PEER_PITCH, PEER_EROWS)],
                                  sem.at[slot]).start(priority=p % PEER_DMA_QUEUES)

    def wait(slot):
        pltpu.make_async_copy(uv_hbm.at[pl.ds(0, rows)], bufs[slot].at[pl.ds(0, rows)],
                              sem.at[slot]).wait()

    def words(slot, m):
        return bufs[slot][pl.ds(m, PEER_PAIRS, stride=PEER_PITCH), :]

    sub_w = lax.broadcasted_iota(jnp.int32, (8, lanes2), 0)
    par_w = lax.broadcasted_iota(jnp.int32, (8, lanes2), 1) % 2

    def u_side(t, slot, between=None):
        hrow = h_ref[pl.ds(t, 1), :]
        acc = None
        for m in range(n_rows):
            if between is not None:
                between(m)
            wd = words(slot, m)
            lo = lax.bitcast_convert_type(wd << 16, f32)
            hi = lax.bitcast_convert_type(wd & jnp.int32(-65536), f32)
            term = (lo * hrow[:, m * PEER_LANES:(m + 1) * PEER_LANES]
                    + hi * hrow[:, half + m * PEER_LANES:half + (m + 1) * PEER_LANES])
            acc = term if acc is None else acc + term
        act = jnp.sum(acc.T, axis=0, keepdims=True)
        gelu = 0.5 * act * (1.0 + lax.erf(act * np.float32(math.sqrt(0.5))))
        w = gate_cur[pl.ds(t, 1), :] * gelu
        wb = jnp.broadcast_to(w, (8, PEER_PAIRS)).astype(bf16)
        wd2 = jnp.dot(wb, dup_ref[...], preferred_element_type=f32)
        w2_ref[slot] = jnp.where(par_w == sub_w, wd2, 0.0)

    def v_side(t, slot):
        wv = jnp.concatenate([pltpu.bitcast(words(slot, n_rows + m), bf16)
                              for m in range(n_rows)], axis=1)
        y2 = jnp.dot(w2_ref[slot].astype(bf16), wv, preferred_element_type=f32)
        y_ref[pl.ds(t, 1), :] = jnp.concatenate([y2[0:1], y2[1:2]], axis=1)

    def step(t, j, do_issue=True):
        wait(j % PEER_SLOTS)
        spread = (lambda m: issue(t + PEER_AHEAD, (j + PEER_AHEAD) % PEER_SLOTS, m, n_rows)
                  ) if do_issue else None
        u_side(t, j % PEER_SLOTS, spread)
        v_side(t - 1, (j - 1) % PEER_SLOTS)

    @pl.when(step_i == 0)
    def _():
        for t in range(PEER_AHEAD):
            issue(t, t)

    wait(0)
    u_side(0, 0)
    issue(PEER_AHEAD, PEER_AHEAD)

    def body(g, carry):
        t0 = 1 + g * PEER_GROUP
        pieces = _head_retrieval_pieces(qn_ref, sk_ref, g, eid_t, gate_t, PEER_GROUP)
        for j in range(PEER_GROUP):
            step(t0 + j, 1 + j)
            pieces[j]()
        return carry

    n_main = (tb - PEER_AHEAD - 1) // PEER_GROUP
    assert n_main == PEER_HEADS - 1
    lax.fori_loop(0, n_main, body, 0)
    tail = range(1 + n_main * PEER_GROUP, tb)
    pieces = _head_retrieval_pieces(qn_ref, sk_ref, PEER_HEADS - 1, eid_t, gate_t, len(tail))
    for t, piece in zip(tail, pieces):
        step(t, t, do_issue=t + PEER_AHEAD < tb)
        if t + PEER_AHEAD >= tb:
            @pl.when(step_i + 1 < pl.num_programs(0))
            def _(t=t):
                issue(t + PEER_AHEAD - tb, (t + PEER_AHEAD) % PEER_SLOTS, blk=1)
        piece()
    v_side(tb - 1, (tb - 1) % PEER_SLOTS)
    eid_cur[...] = eid_next[...]
    gate_cur[...] = gate_next[...]
    eid_next[...] = eid_t[...].T
    gate_next[...] = gate_t[...].T


    o_ref[...] = _rms(x_ref[...] + _interleave_chunks(y_ref[...]), fw_ref[...])


def make_peer_table(u_tab, v_tab):
    e, d = u_tab.shape
    te = 256
    assert e % te == 0 and d == PEER_EROWS * PEER_LANES
    return pl.pallas_call(
        _pack_table_kernel,
        grid=(e // te,),
        in_specs=[pl.BlockSpec((te, d), lambda i: (i, 0)), pl.BlockSpec((te, d), lambda i: (i, 0))],
        out_specs=pl.BlockSpec((te * PEER_EROWS, PEER_LANES), lambda i: (i, 0)),
        out_shape=jax.ShapeDtypeStruct((e * PEER_EROWS, PEER_LANES), jnp.int32),
        compiler_params=_cparams(("arbitrary",)),
        name="peer_pack",
    )(u_tab, v_tab)


def _pack_table_kernel(u_ref, v_ref, o_ref):
    te = u_ref.shape[0]
    half = PEER_EROWS // 2

    def bf16_bits(x):
        return lax.bitcast_convert_type(x.astype(jnp.bfloat16).astype(jnp.float32), jnp.int32)

    for src, m0 in ((u_ref, 0), (v_ref, half)):
        for m in range(half):
            lo = bf16_bits(src[:, (2 * m) * PEER_LANES:(2 * m + 1) * PEER_LANES])
            hi = bf16_bits(src[:, (2 * m + 1) * PEER_LANES:(2 * m + 2) * PEER_LANES])
            word = lax.shift_right_logical(lo, jnp.int32(16)) | (hi & jnp.int32(-65536))
            o_ref[pl.ds(m0 + m, te, stride=PEER_EROWS), :] = word


def peer_main(qp, subkeys, x, nw, fw, uv_tab, tb=128):
    n, d = x.shape
    tb = min(tb, n)
    assert n % tb == 0 and tb > PEER_AHEAD + PEER_SLOTS + 1 and PEER_GROUP % PEER_SLOTS == 0
    assert uv_tab.shape[1] == PEER_LANES and d == PEER_EROWS * PEER_LANES
    nblk = n // tb
    assert nblk >= 2 and tb % PEER_SLOTS == 0
    eidx0, gate0 = peer_retrieve(qp[:2 * tb], subkeys, tn=tb)
    first = lambda i: (0, 0)
    return pl.pallas_call(
        _peer_main_kernel,
        grid=(nblk,),
        in_specs=[
            pl.BlockSpec((2 * tb, PEER_PAIRS), first),
            pl.BlockSpec((2 * tb, PEER_PAIRS), first),
            pl.BlockSpec((tb, qp.shape[1]), lambda i: (jnp.minimum(i + 2, nblk - 1), 0)),
            pl.BlockSpec(subkeys.shape, lambda i: (0, 0, 0, 0)),
            pl.BlockSpec((tb, d), lambda i: (i, 0)),
            pl.BlockSpec((1, d), first),
            pl.BlockSpec((1, d), first),
            pl.BlockSpec(memory_space=pl.ANY),
        ],
        out_specs=pl.BlockSpec((tb, d), lambda i: (i, 0)),
        out_shape=jax.ShapeDtypeStruct((n, d), jnp.float32),
        scratch_shapes=[
            pltpu.VMEM((tb, d), jnp.float32),
            pltpu.VMEM((tb, d), jnp.float32),
            pltpu.VMEM((PEER_PAIRS, 2 * PEER_PAIRS), jnp.bfloat16),
            pltpu.VMEM((PEER_SLOTS, 8, 2 * PEER_PAIRS), jnp.float32),
            pltpu.SMEM((2 * tb, PEER_PAIRS), jnp.int32),
            pltpu.VMEM((PEER_PAIRS, tb), jnp.int32),
            pltpu.VMEM((PEER_PAIRS, tb), jnp.float32),
            pltpu.VMEM((tb, PEER_PAIRS), jnp.int32),
            pltpu.VMEM((tb, PEER_PAIRS), jnp.float32),
            pltpu.VMEM((tb, PEER_PAIRS), jnp.int32),
            pltpu.VMEM((tb, PEER_PAIRS), jnp.float32),
            *[pltpu.VMEM((PEER_PAIRS * PEER_PITCH, PEER_LANES), jnp.int32)
              for _ in range(PEER_SLOTS)],
            pltpu.SemaphoreType.DMA((PEER_SLOTS,)),
            pltpu.SemaphoreType.DMA((2,)),
        ],
        compiler_params=_cparams(("arbitrary",)),
        name="peer_main",
    )(eidx0, gate0, qp, subkeys, x, nw.reshape(1, d), fw.reshape(1, d), uv_tab)


def _trunk(x, mk_arr, mv_arr, mk_col, mv_col, s0, cache_k, cache_v, p):
    b, t, d = x.shape
    n = b * t
    proj = norm_matmul(x.reshape(n, d), p["norm_mix_w"], p["w_in"], tn=IN_WIDTH // 2)
    proj3 = proj.reshape(b, t, IN_WIDTH)
    o_hg, s_fin = hgrn(proj3, s0, p["lb"], p["hg_norm_w"])
    o_sw = swa(proj3, p["bias"], p["sink_col"], cache_k, cache_v)
    x1 = outproj(o_hg.reshape(n, HG_WIDTH), o_sw.reshape(n, SWA_WIDTH), p["w_out"], x.reshape(n, d))
    x2 = memattn(x1.reshape(b, t, d), p["norm_cross_w"], p["mem_wq"], mk_arr, mv_arr,
                 mk_col, mv_col, p["mem_wo"]).reshape(n, d)
    qp = norm_matmul(x2, p["norm_ffn_w"], p["peer_wq"], tn=2048)
    y = peer_main(qp, p["peer_subkeys"], x2, p["norm_ffn_w"], p["final_norm_w"], p["peer_uv"])
    keep = min(WINDOW, t) if cache_k is None else t
    k_rows = proj3[:, t - keep:, OFF_KSW:OFF_KSW + SWA_KV_WIDTH]
    v_rows = proj3[:, t - keep:, OFF_VSW:OFF_VSW + SWA_KV_WIDTH]
    k_rows = k_rows.reshape(b, keep, SWA_KV_HEADS, SWA_HEAD_DIM)
    v_rows = v_rows.reshape(b, keep, SWA_KV_HEADS, SWA_HEAD_DIM)
    return y.reshape(b, t, d), s_fin, k_rows, v_rows


def kernel(x_prompt, x_sample, mem_prompt, state_hgrn, cache_swa_k, cache_swa_v, cache_mem_k, cache_mem_v, rel_bias, hg_lb_logits, norm_mix_w, w_in, hg_norm_w, swa_sinks, w_out, norm_mem_w, norm_cross_w, mem_wq, mem_wk, mem_wv, mem_wo, norm_ffn_w, peer_wq, peer_subkeys, peer_u, peer_v, final_norm_w):
    bf16 = jnp.bfloat16
    depth = w_in.shape[0]
    assert depth == 1
    l = 0
    lb_all = jnp.cumsum(jax.nn.softmax(hg_lb_logits.astype(jnp.float32), axis=0), axis=0)
    n_keys = (WIN_CHUNKS + 1) * CHUNK
    rel = jnp.arange(n_keys)[None, :] - (WIN_CHUNKS * CHUNK + jnp.arange(CHUNK))[:, None]
    bias = _t5_bias(rel, rel_bias).reshape(SWA_KV_HEADS, 2, 2, CHUNK, n_keys)
    bias = bias.transpose(0, 2, 1, 3, 4).reshape(SWA_KV_HEADS, 2, 2 * CHUNK, n_keys)
    bias = jnp.pad(bias, ((0, 0), (0, 0), (0, 0), (0, CHUNK)))
    sink_col = jnp.broadcast_to(
        swa_sinks[l].astype(jnp.float32).reshape(SWA_KV_HEADS, 2, 2, 1),
        (SWA_KV_HEADS, 2, 2, CHUNK)).transpose(0, 2, 1, 3).reshape(SWA_KV_HEADS, 2, 2 * CHUNK, 1)
    p = {
        "norm_mix_w": norm_mix_w[l], "w_in": w_in[l].astype(bf16), "lb": lb_all[l],
        "hg_norm_w": hg_norm_w[l], "bias": bias, "sink_col": sink_col,
        "w_out": w_out[l].astype(bf16), "norm_cross_w": norm_cross_w[l],
        "mem_wq": mem_wq[l].astype(bf16), "mem_wo": mem_wo[l].astype(bf16),
        "norm_ffn_w": norm_ffn_w[l], "peer_wq": peer_wq[l].astype(bf16),
        "peer_subkeys": peer_subkeys[l].astype(bf16), "peer_uv": make_peer_table(peer_u[l], peer_v[l]),
        "final_norm_w": final_norm_w,
    }
    bp, tp, d = x_prompt.shape
    bs, ts, _ = x_sample.shape

    wkv = jnp.concatenate([mem_wk[l], mem_wv[l]], axis=1).astype(bf16)
    kv = norm_matmul(mem_prompt.reshape(bp * N_MEM, d), norm_mem_w[l], wkv)
    kv3 = kv.reshape(bp, N_MEM, 2 * MEM_INNER)
    mk = kv3[:, :, :MEM_INNER].reshape(bp, N_MEM, MEM_HEADS, MEM_HEAD_DIM)
    mv = kv3[:, :, MEM_INNER:].reshape(bp, N_MEM, MEM_HEADS, MEM_HEAD_DIM)

    s0 = jnp.zeros((bp, HG_HEADS, HG_DK, HG_DV), jnp.float32)
    yp, sp, kp, vp = _trunk(x_prompt, kv3, kv3, 0, 1, s0, None, None, p)

    cmk = cache_mem_k[l].reshape(bs, N_MEM, MEM_INNER)
    cmv = cache_mem_v[l].reshape(bs, N_MEM, MEM_INNER)
    ck = cache_swa_k[l].reshape(bs, -1, SWA_KV_WIDTH)
    cv = cache_swa_v[l].reshape(bs, -1, SWA_KV_WIDTH)
    ys, ss, ks_new, vs_new = _trunk(x_sample, cmk, cmv, 0, 0, state_hgrn[l], ck, cv, p)

    return (yp, ys, sp[None], kp[None], vp[None], mk[None], mv[None],
            ss[None], ks_new[None], vs_new[None])
```

```python
import functools
import math

import numpy as np
import jax
import jax.numpy as jnp
from jax import lax
from jax.experimental import pallas as pl
from jax.experimental.pallas import tpu as pltpu

EPS = 1e-6
NEG_INF = -1e30
CHUNK = 64

HG_HEADS = 8
HG_DK = 128
HG_DV = 128
HG_WIDTH = HG_HEADS * HG_DV
HG_SUB = 16
HG_CHUNK = 64
HG_HPS = 8

SWA_HEADS = 16
SWA_KV_HEADS = 4
SWA_GROUP = SWA_HEADS // SWA_KV_HEADS
SWA_HEAD_DIM = 64
SWA_WIDTH = SWA_HEADS * SWA_HEAD_DIM
SWA_KV_WIDTH = SWA_KV_HEADS * SWA_HEAD_DIM
SWA_SCALE = SWA_HEAD_DIM ** -0.5
WINDOW = 128
WIN_CHUNKS = WINDOW // CHUNK
SWA_QCHUNKS = 2
REL_BUCKETS = 32
REL_MAX_DIST = 128

N_MEM = 256
MEM_HEADS = 4
MEM_HEAD_DIM = 128
MEM_INNER = MEM_HEADS * MEM_HEAD_DIM
MEM_SCALE = MEM_HEAD_DIM ** -0.5

PEER_HEADS = 8
PEER_NKEYS = 128
PEER_DHALF = 128
PEER_TOPK = 16
PEER_PAIRS = PEER_HEADS * PEER_TOPK

OFF_Q, OFF_F, OFF_I, OFF_G = 0, 1024, 2048, 3072
OFF_QSW, OFF_KSW, OFF_VSW = 4096, 5120, 5376
IN_WIDTH = 5632

VMEM_LIMIT_BYTES = 56 * 1024 * 1024

_NT = (((1,), (1,)), ((), ()))
_TN = (((0,), (0,)), ((), ()))


def _cparams(sem):
    return pltpu.CompilerParams(dimension_semantics=sem, vmem_limit_bytes=VMEM_LIMIT_BYTES)


def _rms(x, w):
    return x * lax.rsqrt(jnp.mean(x * x, axis=-1, keepdims=True) + EPS) * w


def _norm_matmul_kernel(x_ref, nw_ref, w_ref, o_ref, h_ref):
    @pl.when(pl.program_id(1) == 0)
    def _():
        h_ref[...] = _rms(x_ref[...], nw_ref[...]).astype(h_ref.dtype)

    o_ref[...] = jnp.dot(h_ref[...], w_ref[...], preferred_element_type=jnp.float32)


def norm_matmul(x, nw, w, tm=512, tn=512):
    n, d = x.shape
    m = w.shape[1]
    tm = min(tm, n)
    tn = min(tn, m)
    assert n % tm == 0 and m % tn == 0
    return pl.pallas_call(
        _norm_matmul_kernel,
        grid=(n // tm, m // tn),
        in_specs=[
            pl.BlockSpec((tm, d), lambda i, j: (i, 0)),
            pl.BlockSpec((1, d), lambda i, j: (0, 0)),
            pl.BlockSpec((d, tn), lambda i, j: (0, j)),
        ],
        out_specs=pl.BlockSpec((tm, tn), lambda i, j: (i, j)),
        out_shape=jax.ShapeDtypeStruct((n, m), jnp.float32),
        scratch_shapes=[pltpu.VMEM((tm, d), jnp.bfloat16)],
        compiler_params=_cparams(("parallel", "arbitrary")),
        name="norm_matmul",
    )(x, nw.reshape(1, d), w)


def _cumsum_rows(x):
    n = x.shape[0]
    row = lax.broadcasted_iota(jnp.int32, x.shape, 0)
    s = 1
    while s < n:
        x = x + jnp.where(row >= s, pltpu.roll(x, s, axis=0), 0.0)
        s *= 2
    return x


def _bcast_rows(x, idxs, g):
    return jnp.concatenate(
        [jnp.broadcast_to(x[i:i + 1, :], (g, x.shape[1])) for i in idxs], axis=0)


def _hgrn_chunk(q, fpre, v, lb, st):
    c = q.shape[0]
    f = lb + (1.0 - lb) * jax.nn.sigmoid(fpre)
    k = 1.0 - f
    lf = jnp.log(f)
    bc = _cumsum_rows(lf)
    be = bc - lf
    bf16 = jnp.bfloat16
    ti = lax.broadcasted_iota(jnp.int32, (c, c), 0)
    si = lax.broadcasted_iota(jnp.int32, (c, c), 1)

    ref0 = _bcast_rows(be, range(0, c, HG_SUB), HG_SUB)
    qd = (q * jnp.exp(bc - ref0)).astype(bf16)
    kd = (k * jnp.exp(ref0 - bc)).astype(bf16)
    a = lax.dot_general(qd, kd, _NT, preferred_element_type=jnp.float32)
    attn = jnp.where((ti // HG_SUB == si // HG_SUB) & (si <= ti), a, 0.0)
    g = 2 * HG_SUB
    while g <= c:
        half = g // 2
        ref = _bcast_rows(bc, range(half - 1, c, g), g)
        ql = (q * jnp.exp(jnp.minimum(bc - ref, 0.0))).astype(bf16)
        kl = (k * jnp.exp(jnp.minimum(ref - bc, 0.0))).astype(bf16)
        a = lax.dot_general(ql, kl, _NT, preferred_element_type=jnp.float32)
        m = (ti // g == si // g) & (ti % g >= half) & (si % g < half)
        attn = jnp.where(m, a, attn)
        g *= 2

    vb = v.astype(bf16)
    q_in = (q * jnp.exp(bc)).astype(bf16)
    o = (lax.dot_general(q_in, st.astype(bf16), _NT, preferred_element_type=jnp.float32)
         + jnp.dot(attn.astype(bf16), vb, preferred_element_type=jnp.float32))
    b_last = bc[c - 1:c, :]
    k_out = (k * jnp.exp(b_last - bc)).astype(bf16)
    st_new = st * jnp.exp(b_last) + lax.dot_general(vb, k_out, _TN,
                                                    preferred_element_type=jnp.float32)
    return o, st_new


def _hgrn_kernel(q_ref, f_ref, i_ref, g_ref, s0_ref, lb_ref, nw_ref, o_ref, sfin_ref, st_ref):
    ci = pl.program_id(2)

    @pl.when(ci == 0)
    def _():
        for hh in range(HG_HPS):
            st_ref[hh] = s0_ref[0, hh].T

    ct = q_ref.shape[1]
    nw = nw_ref[...]
    for j in range(ct // HG_CHUNK):
        sl = pl.ds(j * HG_CHUNK, HG_CHUNK)
        for hh in range(HG_HPS):
            hs = slice(hh * HG_DK, (hh + 1) * HG_DK)
            o, st_new = _hgrn_chunk(q_ref[0, sl, hs], f_ref[0, sl, hs], i_ref[0, sl, hs],
                                    lb_ref[:, hs], st_ref[hh])
            st_ref[hh] = st_new
            o = _rms(o, nw)
            o_ref[0, sl, hs] = (o * jax.nn.silu(g_ref[0, sl, hs])).astype(o_ref.dtype)

    @pl.when(ci == pl.num_programs(2) - 1)
    def _():
        for hh in range(HG_HPS):
            sfin_ref[0, hh] = st_ref[hh].T


def hgrn(proj3, s0, lb, nw):
    b, t, _ = proj3.shape
    ct = min(256, t)
    assert t % ct == 0 and ct % HG_CHUNK == 0 and HG_HEADS % HG_HPS == 0
    w = HG_HPS * HG_DK

    def col(off):
        return pl.BlockSpec((1, ct, w), lambda bi, h, c: (bi, c, off // w + h))

    return pl.pallas_call(
        _hgrn_kernel,
        grid=(b, HG_HEADS // HG_HPS, t // ct),
        in_specs=[
            col(OFF_Q), col(OFF_F), col(OFF_I), col(OFF_G),
            pl.BlockSpec((1, HG_HPS, HG_DK, HG_DV), lambda bi, h, c: (bi, h, 0, 0)),
            pl.BlockSpec((1, w), lambda bi, h, c: (0, h)),
            pl.BlockSpec((1, HG_DV), lambda bi, h, c: (0, 0)),
        ],
        out_specs=[
            pl.BlockSpec((1, ct, w), lambda bi, h, c: (bi, c, h)),
            pl.BlockSpec((1, HG_HPS, HG_DK, HG_DV), lambda bi, h, c: (bi, h, 0, 0)),
        ],
        out_shape=[
            jax.ShapeDtypeStruct((b, t, HG_WIDTH), jnp.bfloat16),
            jax.ShapeDtypeStruct((b, HG_HEADS, HG_DK, HG_DV), jnp.float32),
        ],
        scratch_shapes=[pltpu.VMEM((HG_HPS, HG_DV, HG_DK), jnp.float32)],
        compiler_params=_cparams(("parallel", "parallel", "arbitrary")),
        name="hgrn",
    )(proj3, proj3, proj3, proj3, s0, lb.reshape(1, -1), nw.reshape(1, -1))


def _swa_kernel(*refs, c_off, qc):
    nk = qc + WIN_CHUNKS
    q_ref, k_refs, v_refs = refs[0], refs[1:1 + nk], refs[1 + nk:1 + 2 * nk]
    bias_ref, sink_ref, o_ref = refs[1 + 2 * nk:]
    bf16 = jnp.bfloat16
    c = pl.program_id(1) * qc + c_off
    hd = SWA_HEAD_DIM
    n_keys = bias_ref.shape[-1]
    pad = [jnp.zeros((n_keys - nk * CHUNK, SWA_KV_WIDTH), jnp.float32)] if n_keys > nk * CHUNK else []
    kk = jnp.concatenate([r[0] for r in k_refs] + pad, axis=0)
    vv = jnp.concatenate([r[0] for r in v_refs] + pad, axis=0).astype(bf16)
    kt = kk.T.astype(bf16)
    key_chunk = lax.broadcasted_iota(jnp.int32, (1, n_keys), 1) // CHUNK
    valid = (c - WIN_CHUNKS + key_chunk) >= 0
    zk = jnp.zeros((hd, n_keys), bf16)
    lane = lax.broadcasted_iota(jnp.int32, (n_keys, 2 * hd), 1)
    q = q_ref[0].astype(bf16)
    nq = qc * CHUNK
    for kv in range(SWA_KV_HEADS):
        ktj = kt[kv * hd:(kv + 1) * hd, :]
        kt_lo = jnp.concatenate([ktj, zk], axis=0)
        kt_hi = jnp.concatenate([zk, ktj], axis=0)
        tile = vv[:, (kv // 2) * 2 * hd:(kv // 2 + 1) * 2 * hd]
        own_lo = kv % 2 == 0
        keep = (lane < hd) if own_lo else (lane >= hd)
        v_own = jnp.where(keep, tile, jnp.zeros_like(tile))
        v_swap = pltpu.roll(v_own.astype(jnp.float32), hd, axis=1).astype(bf16)
        v_lo, v_hi = (v_own, v_swap) if own_lo else (v_swap, v_own)
        c0 = kv * SWA_GROUP * hd
        qs = jnp.concatenate([q[:, c0:c0 + 2 * hd], q[:, c0 + 2 * hd:c0 + 4 * hd]], axis=0)
        probs = []
        for i, ktm in enumerate((kt_lo, kt_hi)):
            logits = jnp.dot(qs, ktm, preferred_element_type=jnp.float32)
            logits = logits * SWA_SCALE + bias_ref[kv, i]
            logits = jnp.where(valid, logits, NEG_INF)
            sink = sink_ref[kv, i]
            m = jnp.maximum(jnp.max(logits, axis=-1, keepdims=True), sink)
            e = jnp.exp(logits - m)
            p = e / (jnp.sum(e, axis=-1, keepdims=True) + jnp.exp(sink - m))
            probs.append(p.astype(bf16))
        o = (jnp.dot(probs[0], v_lo, preferred_element_type=jnp.float32)
             + jnp.dot(probs[1], v_hi, preferred_element_type=jnp.float32))
        o_ref[0, :, c0:c0 + 2 * hd] = o[0:nq].astype(o_ref.dtype)
        o_ref[0, :, c0 + 2 * hd:c0 + 4 * hd] = o[nq:2 * nq].astype(o_ref.dtype)


def swa(proj3, rel_bias, sinks, cache_k=None, cache_v=None):
    b, t, _ = proj3.shape
    nc = t // CHUNK
    assert t % CHUNK == 0
    qc = SWA_QCHUNKS if (cache_k is None and nc % SWA_QCHUNKS == 0) else 1
    nk = qc + WIN_CHUNKS
    qspec = pl.BlockSpec((1, qc * CHUNK, SWA_WIDTH), lambda bi, c: (bi, c, OFF_QSW // SWA_WIDTH))

    def chunk(off, j):
        if cache_k is None:
            return pl.BlockSpec(
                (1, CHUNK, SWA_KV_WIDTH),
                lambda bi, c: (bi, jnp.maximum(c * qc - WIN_CHUNKS + j, 0), off // SWA_KV_WIDTH))
        if j < WIN_CHUNKS:
            return pl.BlockSpec((1, CHUNK, SWA_KV_WIDTH), lambda bi, c: (bi, j, 0))
        return pl.BlockSpec((1, CHUNK, SWA_KV_WIDTH), lambda bi, c: (bi, c, off // SWA_KV_WIDTH))

    if cache_k is None:
        karrs, varrs, c_off = [proj3] * nk, [proj3] * nk, 0
    else:
        assert nc == 1 and cache_k.shape[1] == WINDOW
        karrs = [cache_k] * WIN_CHUNKS + [proj3]
        varrs = [cache_v] * WIN_CHUNKS + [proj3]
        c_off = WIN_CHUNKS
    kspecs = [chunk(OFF_KSW, j) for j in range(nk)]
    vspecs = [chunk(OFF_VSW, j) for j in range(nk)]

    n_win = (WIN_CHUNKS + 1) * CHUNK
    n_keys = -(-nk * CHUNK // 128) * 128
    rel = jnp.arange(n_win)[None, :] - (WIN_CHUNKS * CHUNK + jnp.arange(CHUNK))[:, None]
    band = _t5_bias(rel, rel_bias)
    per_q = [jnp.pad(band, ((0, 0), (0, 0), (qi * CHUNK, n_keys - n_win - qi * CHUNK)),
                     constant_values=NEG_INF) for qi in range(qc)]
    bias = jnp.stack(per_q, axis=1)
    bias = bias.reshape(SWA_KV_HEADS, 2, 2, qc * CHUNK, n_keys).transpose(0, 2, 1, 3, 4)
    bias = bias.reshape(SWA_KV_HEADS, 2, 2 * qc * CHUNK, n_keys)
    rows = 2 * qc * CHUNK
    sink_col = jnp.broadcast_to(sinks.astype(jnp.float32).reshape(SWA_KV_HEADS, 2, 2, 1),
                                (SWA_KV_HEADS, 2, 2, qc * CHUNK))
    sink_col = sink_col.transpose(0, 2, 1, 3).reshape(SWA_KV_HEADS, 2, rows, 1)
    return pl.pallas_call(
        functools.partial(_swa_kernel, c_off=c_off, qc=qc),
        grid=(b, nc // qc),
        in_specs=[qspec] + kspecs + vspecs + [
            pl.BlockSpec((SWA_KV_HEADS, 2, rows, n_keys), lambda bi, c: (0, 0, 0, 0)),
            pl.BlockSpec((SWA_KV_HEADS, 2, rows, 1), lambda bi, c: (0, 0, 0, 0)),
        ],
        out_specs=pl.BlockSpec((1, qc * CHUNK, SWA_WIDTH), lambda bi, c: (bi, c, 0)),
        out_shape=jax.ShapeDtypeStruct((b, t, SWA_WIDTH), jnp.bfloat16),
        compiler_params=_cparams(("parallel", "parallel")),
        name="swa",
    )(proj3, *karrs, *varrs, bias, sink_col)


def _t5_bias(rel, table):
    nb = REL_BUCKETS // 2
    max_exact = nb // 2
    side = jnp.where(rel > 0, nb, 0)
    n = jnp.abs(rel)
    n_f = jnp.maximum(n, max_exact).astype(jnp.float32)
    large = max_exact + (jnp.log(n_f / max_exact) / math.log(REL_MAX_DIST / max_exact)
                         * (nb - max_exact)).astype(jnp.int32)
    large = jnp.minimum(large, nb - 1)
    bucket = side + jnp.where(n < max_exact, n, large)
    return jnp.transpose(table[bucket].astype(jnp.float32), (2, 0, 1))


def _outproj_kernel(a1_ref, a2_ref, w_ref, x_ref, o_ref):
    k1 = a1_ref.shape[1]
    acc = jnp.dot(a1_ref[...], w_ref[0:k1, :], preferred_element_type=jnp.float32)
    acc += jnp.dot(a2_ref[...], w_ref[k1:, :], preferred_element_type=jnp.float32)
    o_ref[...] = x_ref[...] + acc


def outproj(a1, a2, w, x, tm=512, tn=2048):
    n, d = x.shape
    k1, k2 = a1.shape[1], a2.shape[1]
    tm = min(tm, n)
    assert n % tm == 0 and d % tn == 0
    return pl.pallas_call(
        _outproj_kernel,
        grid=(n // tm, d // tn),
        in_specs=[
            pl.BlockSpec((tm, k1), lambda i, j: (i, 0)),
            pl.BlockSpec((tm, k2), lambda i, j: (i, 0)),
            pl.BlockSpec((k1 + k2, tn), lambda i, j: (0, j)),
            pl.BlockSpec((tm, tn), lambda i, j: (i, j)),
        ],
        out_specs=pl.BlockSpec((tm, tn), lambda i, j: (i, j)),
        out_shape=jax.ShapeDtypeStruct((n, d), jnp.float32),
        compiler_params=_cparams(("parallel", "arbitrary")),
        name="outproj",
    )(a1, a2, w, x)


def _memattn_kernel(x_ref, nw_ref, wq_ref, mk_ref, mv_ref, wo_ref, o_ref):
    bf16 = jnp.bfloat16
    x = x_ref[0]
    h = _rms(x, nw_ref[...]).astype(bf16)
    q = jnp.dot(h, wq_ref[...], preferred_element_type=jnp.float32)
    mk = mk_ref[0].astype(bf16)
    mv = mv_ref[0].astype(bf16)
    outs = []
    for hh in range(MEM_HEADS):
        sl = slice(hh * MEM_HEAD_DIM, (hh + 1) * MEM_HEAD_DIM)
        logits = lax.dot_general(q[:, sl].astype(bf16), mk[:, sl], _NT,
                                 preferred_element_type=jnp.float32) * MEM_SCALE
        m = jnp.max(logits, axis=-1, keepdims=True)
        e = jnp.exp(logits - m)
        p = e / jnp.sum(e, axis=-1, keepdims=True)
        outs.append(jnp.dot(p.astype(bf16), mv[:, sl], preferred_element_type=jnp.float32))
    o = jnp.concatenate(outs, axis=-1).astype(bf16)
    o_ref[0] = x + jnp.dot(o, wo_ref[...], preferred_element_type=jnp.float32)


def memattn(x3, nw, wq, mk_arr, mv_arr, mk_col, mv_col, wo, tm=512):
    b, t, d = x3.shape
    tm = min(tm, t)
    assert t % tm == 0
    return pl.pallas_call(
        _memattn_kernel,
        grid=(b, t // tm),
        in_specs=[
            pl.BlockSpec((1, tm, d), lambda bi, i: (bi, i, 0)),
            pl.BlockSpec((1, d), lambda bi, i: (0, 0)),
            pl.BlockSpec((d, MEM_INNER), lambda bi, i: (0, 0)),
            pl.BlockSpec((1, N_MEM, MEM_INNER), lambda bi, i: (bi, 0, mk_col)),
            pl.BlockSpec((1, N_MEM, MEM_INNER), lambda bi, i: (bi, 0, mv_col)),
            pl.BlockSpec((MEM_INNER, d), lambda bi, i: (0, 0)),
        ],
        out_specs=pl.BlockSpec((1, tm, d), lambda bi, i: (bi, i, 0)),
        out_shape=jax.ShapeDtypeStruct((b, t, d), jnp.float32),
        compiler_params=_cparams(("parallel", "arbitrary")),
        name="memattn",
    )(x3, nw.reshape(1, d), wq, mk_arr, mv_arr, wo)


def _topk_step(s, row, payload=None):
    n = s.shape[0]
    m = jnp.max(s, axis=0, keepdims=True)
    idx = jnp.min(jnp.where(s == m, row, n), axis=0, keepdims=True)
    hit = row == idx
    pick = idx if payload is None else jnp.sum(jnp.where(hit, payload, 0), axis=0, keepdims=True)
    return m, pick, jnp.where(hit, -jnp.inf, s)


def _head_retrieval_pieces(q_ref, sk_ref, h, eid_out, gate_out, n_pieces):
    bf16 = jnp.bfloat16
    k = PEER_TOPK
    hk = k // 2
    static = isinstance(h, int)
    st = {}

    def reset(s, payload=None):
        st.update(s=s, payload=payload, vals=[], picks=[])

    def score(c):
        col = (h * 2 + c) * PEER_DHALF
        col = col if static else pl.multiple_of(col, PEER_DHALF)
        qh = q_ref[:, pl.ds(col, PEER_DHALF)].astype(bf16)
        reset(lax.dot_general(sk_ref[c, h], qh, _NT, preferred_element_type=jnp.float32))

    def extract(n_it):
        s = st["s"]
        row = lax.broadcasted_iota(jnp.int32, s.shape, 0)
        for _ in range(n_it):
            m, pick, s = _topk_step(s, row, st["payload"])
            st["vals"].append(m)
            st["picks"].append(pick)
        st["s"] = s

    def close(name):
        st[name] = (jnp.concatenate(st["vals"], axis=0), jnp.concatenate(st["picks"], axis=0))

    def candidates():
        (s1, i1), (s2, i2) = st["t0"], st["t1"]
        cand = jnp.concatenate(
            [s1[0:1] + s2] + [s1[a:a + 1] + s2[0:hk] for a in range(1, hk)] + [s1[hk:k] + s2[0:1]],
            axis=0)
        cidx = jnp.concatenate(
            [i1[0:1] * PEER_NKEYS + i2] + [i1[a:a + 1] * PEER_NKEYS + i2[0:hk] for a in range(1, hk)]
            + [i1[hk:k] * PEER_NKEYS + i2[0:1]], axis=0)
        reset(cand, cidx)

    def finish():
        close("t2")
        top_s, eid = st["t2"]
        e = jnp.exp(top_s - top_s[0:1])
        r0 = h * k if static else pl.multiple_of(h * k, k)
        eid_out[pl.ds(r0, k), :] = eid
        gate_out[pl.ds(r0, k), :] = e / jnp.sum(e, axis=0, keepdims=True)

    q4 = [lambda: extract(k // 4)] * 4
    work = ([lambda: score(0)] + q4 + [lambda: (close("t0"), score(1))] + q4
            + [lambda: (close("t1"), candidates())] + q4 + [finish])
    while len(work) > n_pieces:
        f0, f1 = work[0], work[1]
        work[0:2] = [lambda f0=f0, f1=f1: (f0(), f1())]
    return work


def _peer_retrieve_kernel(q_ref, sk_ref, eidx_ref, gate_ref, eid_t, gate_t):
    for h in range(PEER_HEADS):
        for piece in _head_retrieval_pieces(q_ref, sk_ref, h, eid_t, gate_t, 1):
            piece()
    eidx_ref[...] = eid_t[...].T
    gate_ref[...] = gate_t[...].T


def peer_retrieve(qp, subkeys, tn=128):
    n, d = qp.shape
    tn = min(tn, n)
    assert n % tn == 0
    return pl.pallas_call(
        _peer_retrieve_kernel,
        grid=(n // tn,),
        in_specs=[
            pl.BlockSpec((tn, d), lambda i: (i, 0)),
            pl.BlockSpec(subkeys.shape, lambda i: (0, 0, 0, 0)),
        ],
        out_specs=[
            pl.BlockSpec((tn, PEER_PAIRS), lambda i: (i, 0)),
            pl.BlockSpec((tn, PEER_PAIRS), lambda i: (i, 0)),
        ],
        out_shape=[
            jax.ShapeDtypeStruct((n, PEER_PAIRS), jnp.int32),
            jax.ShapeDtypeStruct((n, PEER_PAIRS), jnp.float32),
        ],
        scratch_shapes=[pltpu.VMEM((PEER_PAIRS, tn), jnp.int32),
                        pltpu.VMEM((PEER_PAIRS, tn), jnp.float32)],
        compiler_params=_cparams(("arbitrary",)),
        name="peer_retrieve",
    )(qp, subkeys)


PEER_SLOTS = 8
PEER_AHEAD = PEER_SLOTS - 2
PEER_GROUP = 16
PEER_DMA_QUEUES = 2
PEER_LANES = 128
PEER_EROWS = 16
PEER_PITCH = 24


def _even_odd_chunks(a):
    nc = a.shape[1] // PEER_LANES
    pick = lambda c: a[:, c * PEER_LANES:(c + 1) * PEER_LANES]
    return jnp.concatenate([pick(c) for c in range(0, nc, 2)] + [pick(c) for c in range(1, nc, 2)],
                           axis=1)


def _interleave_chunks(a):
    nc = a.shape[1] // PEER_LANES
    half = nc // 2
    pick = lambda c: a[:, c * PEER_LANES:(c + 1) * PEER_LANES]
    return jnp.concatenate([pick(c // 2 + (half if c % 2 else 0)) for c in range(nc)], axis=1)


def _peer_main_kernel(eidx0_ref, gate0_ref, qn_ref, sk_ref, x_ref, nw_ref, fw_ref, uv_hbm, o_ref,
                      h_ref, y_ref, dup_ref, w2_ref, eidx_ref, gate_ref, eid_t, gate_t,
                      eid_next, gate_next, *scratch):
    bf16 = jnp.bfloat16
    f32 = jnp.float32
    bufs, sem, csem = scratch[:PEER_SLOTS], scratch[PEER_SLOTS], scratch[PEER_SLOTS + 1]
    tb, d = x_ref.shape
    half = d // 2
    n_rows = PEER_EROWS // 2
    lanes2 = 2 * PEER_PAIRS

    @pl.when(pl.program_id(0) == 0)
    def _():
        eid_next[...] = eidx0_ref[...]
        gate_next[...] = gate0_ref[...]

    to_smem = pltpu.make_async_copy(eid_next, eidx_ref, csem)
    to_smem.start()
    gate_ref[...] = gate_next[...]
    h_ref[...] = _even_odd_chunks(_rms(x_ref[...], nw_ref[...])).astype(bf16).astype(f32)
    dup_ref[...] = (lax.broadcasted_iota(jnp.int32, (PEER_PAIRS, lanes2), 1) // 2
                    == lax.broadcasted_iota(jnp.int32, (PEER_PAIRS, lanes2), 0)).astype(bf16)
    to_smem.wait()

    rows = PEER_PAIRS * PEER_EROWS

    def issue(t, slot, part=0, parts=1):
        n_p = PEER_PAIRS // parts
        for p in range(part * n_p, (part + 1) * n_p):
            r = pl.multiple_of(eidx_ref[t, p] * PEER_EROWS, PEER_EROWS)
            pltpu.make_async_copy(uv_hbm.at[pl.ds(r, PEER_EROWS)],
                                  bufs[slot].at[pl.ds(p * PEER_PITCH, PEER_EROWS)],
                                  sem.at[slot]).start(priority=p % PEER_DMA_QUEUES)

    def wait(slot):
        pltpu.make_async_copy(uv_hbm.at[pl.ds(0, rows)], bufs[slot].at[pl.ds(0, rows)],
                              sem.at[slot]).wait()

    def words(slot, m):
        return bufs[slot][pl.ds(m, PEER_PAIRS, stride=PEER_PITCH), :]

    sub_w = lax.broadcasted_iota(jnp.int32, (8, lanes2), 0)
    par_w = lax.broadcasted_iota(jnp.int32, (8, lanes2), 1) % 2

    def u_side(t, slot, between=None):
        hrow = h_ref[pl.ds(t, 1), :]
        acc = None
        for m in range(n_rows):
            if between is not None:
                between(m)
            wd = words(slot, m)
            lo = lax.bitcast_convert_type(wd << 16, f32)
            hi = lax.bitcast_convert_type(wd & jnp.int32(-65536), f32)
            term = (lo * hrow[:, m * PEER_LANES:(m + 1) * PEER_LANES]
                    + hi * hrow[:, half + m * PEER_LANES:half + (m + 1) * PEER_LANES])
            acc = term if acc is None else acc + term
        act = jnp.sum(acc.T, axis=0, keepdims=True)
        gelu = 0.5 * act * (1.0 + lax.erf(act * np.float32(math.sqrt(0.5))))
        w = gate_ref[pl.ds(t, 1), :] * gelu
        wb = jnp.broadcast_to(w, (8, PEER_PAIRS)).astype(bf16)
        wd2 = jnp.dot(wb, dup_ref[...], preferred_element_type=f32)
        w2_ref[slot] = jnp.where(par_w == sub_w, wd2, 0.0)

    def v_side(t, slot):
        wv = jnp.concatenate([pltpu.bitcast(words(slot, n_rows + m), bf16)
                              for m in range(n_rows)], axis=1)
        y2 = jnp.dot(w2_ref[slot].astype(bf16), wv, preferred_element_type=f32)
        y_ref[pl.ds(t, 1), :] = jnp.concatenate([y2[0:1], y2[1:2]], axis=1)

    def step(t, j, do_issue=True):
        wait(j % PEER_SLOTS)
        spread = (lambda m: issue(t + PEER_AHEAD, (j + PEER_AHEAD) % PEER_SLOTS, m, n_rows)
                  ) if do_issue else None
        u_side(t, j % PEER_SLOTS, spread)
        v_side(t - 1, (j - 1) % PEER_SLOTS)

    for t in range(PEER_AHEAD):
        issue(t, t)
    wait(0)
    u_side(0, 0)
    issue(PEER_AHEAD, PEER_AHEAD)

    def body(g, carry):
        t0 = 1 + g * PEER_GROUP
        pieces = _head_retrieval_pieces(qn_ref, sk_ref, g, eid_t, gate_t, PEER_GROUP)
        for j in range(PEER_GROUP):
            step(t0 + j, 1 + j)
            pieces[j]()
        return carry

    n_main = (tb - PEER_AHEAD - 1) // PEER_GROUP
    assert n_main == PEER_HEADS - 1
    lax.fori_loop(0, n_main, body, 0)
    tail = range(1 + n_main * PEER_GROUP, tb)
    pieces = _head_retrieval_pieces(qn_ref, sk_ref, PEER_HEADS - 1, eid_t, gate_t, len(tail))
    for t, piece in zip(tail, pieces):
        step(t, t, do_issue=t + PEER_AHEAD < tb)
        piece()
    v_side(tb - 1, (tb - 1) % PEER_SLOTS)
    eid_next[...] = eid_t[...].T
    gate_next[...] = gate_t[...].T


    o_ref[...] = _rms(x_ref[...] + _interleave_chunks(y_ref[...]), fw_ref[...])


def make_peer_table(u_tab, v_tab):
    e, d = u_tab.shape
    te = 256
    assert e % te == 0 and d == PEER_EROWS * PEER_LANES
    return pl.pallas_call(
        _pack_table_kernel,
        grid=(e // te,),
        in_specs=[pl.BlockSpec((te, d), lambda i: (i, 0)), pl.BlockSpec((te, d), lambda i: (i, 0))],
        out_specs=pl.BlockSpec((te * PEER_EROWS, PEER_LANES), lambda i: (i, 0)),
        out_shape=jax.ShapeDtypeStruct((e * PEER_EROWS, PEER_LANES), jnp.int32),
        compiler_params=_cparams(("arbitrary",)),
        name="peer_pack",
    )(u_tab, v_tab)


def _pack_table_kernel(u_ref, v_ref, o_ref):
    te = u_ref.shape[0]
    half = PEER_EROWS // 2

    def bf16_bits(x):
        return lax.bitcast_convert_type(x.astype(jnp.bfloat16).astype(jnp.float32), jnp.int32)

    for src, m0 in ((u_ref, 0), (v_ref, half)):
        for m in range(half):
            lo = bf16_bits(src[:, (2 * m) * PEER_LANES:(2 * m + 1) * PEER_LANES])
            hi = bf16_bits(src[:, (2 * m + 1) * PEER_LANES:(2 * m + 2) * PEER_LANES])
            word = lax.shift_right_logical(lo, jnp.int32(16)) | (hi & jnp.int32(-65536))
            o_ref[pl.ds(m0 + m, te, stride=PEER_EROWS), :] = word


def peer_main(qp, subkeys, x, nw, fw, uv_tab, tb=128):
    n, d = x.shape
    tb = min(tb, n)
    assert n % tb == 0 and tb > PEER_AHEAD + PEER_SLOTS + 1 and PEER_GROUP % PEER_SLOTS == 0
    assert uv_tab.shape[1] == PEER_LANES and d == PEER_EROWS * PEER_LANES
    nblk = n // tb
    eidx0, gate0 = peer_retrieve(qp[:tb], subkeys, tn=tb)
    first = lambda i: (0, 0)
    return pl.pallas_call(
        _peer_main_kernel,
        grid=(nblk,),
        in_specs=[
            pl.BlockSpec((tb, PEER_PAIRS), first),
            pl.BlockSpec((tb, PEER_PAIRS), first),
            pl.BlockSpec((tb, qp.shape[1]), lambda i: (jnp.minimum(i + 1, nblk - 1), 0)),
            pl.BlockSpec(subkeys.shape, lambda i: (0, 0, 0, 0)),
            pl.BlockSpec((tb, d), lambda i: (i, 0)),
            pl.BlockSpec((1, d), first),
            pl.BlockSpec((1, d), first),
            pl.BlockSpec(memory_space=pl.ANY),
        ],
        out_specs=pl.BlockSpec((tb, d), lambda i: (i, 0)),
        out_shape=jax.ShapeDtypeStruct((n, d), jnp.float32),
        scratch_shapes=[
            pltpu.VMEM((tb, d), jnp.float32),
            pltpu.VMEM((tb, d), jnp.float32),
            pltpu.VMEM((PEER_PAIRS, 2 * PEER_PAIRS), jnp.bfloat16),
            pltpu.VMEM((PEER_SLOTS, 8, 2 * PEER_PAIRS), jnp.float32),
            pltpu.SMEM((tb, PEER_PAIRS), jnp.int32),
            pltpu.VMEM((tb, PEER_PAIRS), jnp.float32),
            pltpu.VMEM((PEER_PAIRS, tb), jnp.int32),
            pltpu.VMEM((PEER_PAIRS, tb), jnp.float32),
            pltpu.VMEM((tb, PEER_PAIRS), jnp.int32),
            pltpu.VMEM((tb, PEER_PAIRS), jnp.float32),
            *[pltpu.VMEM((PEER_PAIRS * PEER_PITCH, PEER_LANES), jnp.int32)
              for _ in range(PEER_SLOTS)],
            pltpu.SemaphoreType.DMA((PEER_SLOTS,)),
            pltpu.SemaphoreType.DMA(()),
        ],
        compiler_params=_cparams(("arbitrary",)),
        name="peer_main",
    )(eidx0, gate0, qp, subkeys, x, nw.reshape(1, d), fw.reshape(1, d), uv_tab)


def _trunk(x, mk_arr, mv_arr, mk_col, mv_col, s0, cache_k, cache_v, p):
    b, t, d = x.shape
    n = b * t
    proj = norm_matmul(x.reshape(n, d), p["norm_mix_w"], p["w_in"], tn=IN_WIDTH // 2)
    proj3 = proj.reshape(b, t, IN_WIDTH)
    o_hg, s_fin = hgrn(proj3, s0, p["lb"], p["hg_norm_w"])
    o_sw = swa(proj3, p["rel_bias"], p["swa_sinks"], cache_k, cache_v)
    x1 = outproj(o_hg.reshape(n, HG_WIDTH), o_sw.reshape(n, SWA_WIDTH), p["w_out"], x.reshape(n, d))
    x2 = memattn(x1.reshape(b, t, d), p["norm_cross_w"], p["mem_wq"], mk_arr, mv_arr,
                 mk_col, mv_col, p["mem_wo"]).reshape(n, d)
    qp = norm_matmul(x2, p["norm_ffn_w"], p["peer_wq"], tn=2048)
    y = peer_main(qp, p["peer_subkeys"], x2, p["norm_ffn_w"], p["final_norm_w"], p["peer_uv"])
    keep = min(WINDOW, t) if cache_k is None else t
    k_rows = proj3[:, t - keep:, OFF_KSW:OFF_KSW + SWA_KV_WIDTH]
    v_rows = proj3[:, t - keep:, OFF_VSW:OFF_VSW + SWA_KV_WIDTH]
    k_rows = k_rows.reshape(b, keep, SWA_KV_HEADS, SWA_HEAD_DIM)
    v_rows = v_rows.reshape(b, keep, SWA_KV_HEADS, SWA_HEAD_DIM)
    return y.reshape(b, t, d), s_fin, k_rows, v_rows


def kernel(x_prompt, x_sample, mem_prompt, state_hgrn, cache_swa_k, cache_swa_v, cache_mem_k, cache_mem_v, rel_bias, hg_lb_logits, norm_mix_w, w_in, hg_norm_w, swa_sinks, w_out, norm_mem_w, norm_cross_w, mem_wq, mem_wk, mem_wv, mem_wo, norm_ffn_w, peer_wq, peer_subkeys, peer_u, peer_v, final_norm_w):
    bf16 = jnp.bfloat16
    depth = w_in.shape[0]
    assert depth == 1
    l = 0
    lb_all = jnp.cumsum(jax.nn.softmax(hg_lb_logits.astype(jnp.float32), axis=0), axis=0)
    p = {
        "norm_mix_w": norm_mix_w[l], "w_in": w_in[l].astype(bf16), "lb": lb_all[l],
        "hg_norm_w": hg_norm_w[l], "rel_bias": rel_bias, "swa_sinks": swa_sinks[l],
        "w_out": w_out[l].astype(bf16), "norm_cross_w": norm_cross_w[l],
        "mem_wq": mem_wq[l].astype(bf16), "mem_wo": mem_wo[l].astype(bf16),
        "norm_ffn_w": norm_ffn_w[l], "peer_wq": peer_wq[l].astype(bf16),
        "peer_subkeys": peer_subkeys[l].astype(bf16), "peer_uv": make_peer_table(peer_u[l], peer_v[l]),
        "final_norm_w": final_norm_w,
    }
    bp, tp, d = x_prompt.shape
    bs, ts, _ = x_sample.shape

    wkv = jnp.concatenate([mem_wk[l], mem_wv[l]], axis=1).astype(bf16)
    kv = norm_matmul(mem_prompt.reshape(bp * N_MEM, d), norm_mem_w[l], wkv)
    kv3 = kv.reshape(bp, N_MEM, 2 * MEM_INNER)
    mk = kv3[:, :, :MEM_INNER].reshape(bp, N_MEM, MEM_HEADS, MEM_HEAD_DIM)
    mv = kv3[:, :, MEM_INNER:].reshape(bp, N_MEM, MEM_HEADS, MEM_HEAD_DIM)

    s0 = jnp.zeros((bp, HG_HEADS, HG_DK, HG_DV), jnp.float32)
    yp, sp, kp, vp = _trunk(x_prompt, kv3, kv3, 0, 1, s0, None, None, p)

    cmk = cache_mem_k[l].reshape(bs, N_MEM, MEM_INNER)
    cmv = cache_mem_v[l].reshape(bs, N_MEM, MEM_INNER)
    ck = cache_swa_k[l].reshape(bs, -1, SWA_KV_WIDTH)
    cv = cache_swa_v[l].reshape(bs, -1, SWA_KV_WIDTH)
    ys, ss, ks_new, vs_new = _trunk(x_sample, cmk, cmv, 0, 0, state_hgrn[l], ck, cv, p)

    return (yp, ys, sp[None], kp[None], vp[None], mk[None], mv[None],
            ss[None], ks_new[None], vs_new[None])
```

```python
import functools
import math

import numpy as np
import jax
import jax.numpy as jnp
from jax import lax
from jax.experimental import pallas as pl
from jax.experimental.pallas import tpu as pltpu

EPS = 1e-6
NEG_INF = -1e30
CHUNK = 64

HG_HEADS = 8
HG_DK = 128
HG_DV = 128
HG_WIDTH = HG_HEADS * HG_DV
HG_SUB = 16
HG_CHUNK = 64
HG_HPS = 8

SWA_HEADS = 16
SWA_KV_HEADS = 4
SWA_GROUP = SWA_HEADS // SWA_KV_HEADS
SWA_HEAD_DIM = 64
SWA_WIDTH = SWA_HEADS * SWA_HEAD_DIM
SWA_KV_WIDTH = SWA_KV_HEADS * SWA_HEAD_DIM
SWA_SCALE = SWA_HEAD_DIM ** -0.5
WINDOW = 128
WIN_CHUNKS = WINDOW // CHUNK
SWA_QCHUNKS = 2
REL_BUCKETS = 32
REL_MAX_DIST = 128

N_MEM = 256
MEM_HEADS = 4
MEM_HEAD_DIM = 128
MEM_INNER = MEM_HEADS * MEM_HEAD_DIM
MEM_SCALE = MEM_HEAD_DIM ** -0.5

PEER_HEADS = 8
PEER_NKEYS = 128
PEER_DHALF = 128
PEER_TOPK = 16
PEER_PAIRS = PEER_HEADS * PEER_TOPK

OFF_Q, OFF_F, OFF_I, OFF_G = 0, 1024, 2048, 3072
OFF_QSW, OFF_KSW, OFF_VSW = 4096, 5120, 5376
IN_WIDTH = 5632

VMEM_LIMIT_BYTES = 56 * 1024 * 1024

_NT = (((1,), (1,)), ((), ()))
_TN = (((0,), (0,)), ((), ()))


def _cparams(sem):
    return pltpu.CompilerParams(dimension_semantics=sem, vmem_limit_bytes=VMEM_LIMIT_BYTES)


def _rms(x, w):
    return x * lax.rsqrt(jnp.mean(x * x, axis=-1, keepdims=True) + EPS) * w


def _norm_matmul_kernel(x_ref, nw_ref, w_ref, o_ref, h_ref):
    @pl.when(pl.program_id(1) == 0)
    def _():
        h_ref[...] = _rms(x_ref[...], nw_ref[...]).astype(h_ref.dtype)

    o_ref[...] = jnp.dot(h_ref[...], w_ref[...], preferred_element_type=jnp.float32)


def norm_matmul(x, nw, w, tm=512, tn=512):
    n, d = x.shape
    m = w.shape[1]
    tm = min(tm, n)
    tn = min(tn, m)
    assert n % tm == 0 and m % tn == 0
    return pl.pallas_call(
        _norm_matmul_kernel,
        grid=(n // tm, m // tn),
        in_specs=[
            pl.BlockSpec((tm, d), lambda i, j: (i, 0)),
            pl.BlockSpec((1, d), lambda i, j: (0, 0)),
            pl.BlockSpec((d, tn), lambda i, j: (0, j)),
        ],
        out_specs=pl.BlockSpec((tm, tn), lambda i, j: (i, j)),
        out_shape=jax.ShapeDtypeStruct((n, m), jnp.float32),
        scratch_shapes=[pltpu.VMEM((tm, d), jnp.bfloat16)],
        compiler_params=_cparams(("parallel", "arbitrary")),
        name="norm_matmul",
    )(x, nw.reshape(1, d), w)


def _cumsum_rows(x):
    n = x.shape[0]
    row = lax.broadcasted_iota(jnp.int32, x.shape, 0)
    s = 1
    while s < n:
        x = x + jnp.where(row >= s, pltpu.roll(x, s, axis=0), 0.0)
        s *= 2
    return x


def _bcast_rows(x, idxs, g):
    return jnp.concatenate(
        [jnp.broadcast_to(x[i:i + 1, :], (g, x.shape[1])) for i in idxs], axis=0)


def _hgrn_chunk(q, fpre, v, lb, st):
    c = q.shape[0]
    f = lb + (1.0 - lb) * jax.nn.sigmoid(fpre)
    k = 1.0 - f
    lf = jnp.log(f)
    bc = _cumsum_rows(lf)
    be = bc - lf
    bf16 = jnp.bfloat16
    ti = lax.broadcasted_iota(jnp.int32, (c, c), 0)
    si = lax.broadcasted_iota(jnp.int32, (c, c), 1)

    ref0 = _bcast_rows(be, range(0, c, HG_SUB), HG_SUB)
    qd = (q * jnp.exp(bc - ref0)).astype(bf16)
    kd = (k * jnp.exp(ref0 - bc)).astype(bf16)
    a = lax.dot_general(qd, kd, _NT, preferred_element_type=jnp.float32)
    attn = jnp.where((ti // HG_SUB == si // HG_SUB) & (si <= ti), a, 0.0)
    g = 2 * HG_SUB
    while g <= c:
        half = g // 2
        ref = _bcast_rows(bc, range(half - 1, c, g), g)
        ql = (q * jnp.exp(jnp.minimum(bc - ref, 0.0))).astype(bf16)
        kl = (k * jnp.exp(jnp.minimum(ref - bc, 0.0))).astype(bf16)
        a = lax.dot_general(ql, kl, _NT, preferred_element_type=jnp.float32)
        m = (ti // g == si // g) & (ti % g >= half) & (si % g < half)
        attn = jnp.where(m, a, attn)
        g *= 2

    vb = v.astype(bf16)
    q_in = (q * jnp.exp(bc)).astype(bf16)
    o = (lax.dot_general(q_in, st.astype(bf16), _NT, preferred_element_type=jnp.float32)
         + jnp.dot(attn.astype(bf16), vb, preferred_element_type=jnp.float32))
    b_last = bc[c - 1:c, :]
    k_out = (k * jnp.exp(b_last - bc)).astype(bf16)
    st_new = st * jnp.exp(b_last) + lax.dot_general(vb, k_out, _TN,
                                                    preferred_element_type=jnp.float32)
    return o, st_new


def _hgrn_kernel(q_ref, f_ref, i_ref, g_ref, s0_ref, lb_ref, nw_ref, o_ref, sfin_ref, st_ref):
    ci = pl.program_id(2)

    @pl.when(ci == 0)
    def _():
        for hh in range(HG_HPS):
            st_ref[hh] = s0_ref[0, hh].T

    ct = q_ref.shape[1]
    nw = nw_ref[...]
    for j in range(ct // HG_CHUNK):
        sl = pl.ds(j * HG_CHUNK, HG_CHUNK)
        for hh in range(HG_HPS):
            hs = slice(hh * HG_DK, (hh + 1) * HG_DK)
            o, st_new = _hgrn_chunk(q_ref[0, sl, hs], f_ref[0, sl, hs], i_ref[0, sl, hs],
                                    lb_ref[:, hs], st_ref[hh])
            st_ref[hh] = st_new
            o = _rms(o, nw)
            o_ref[0, sl, hs] = (o * jax.nn.silu(g_ref[0, sl, hs])).astype(o_ref.dtype)

    @pl.when(ci == pl.num_programs(2) - 1)
    def _():
        for hh in range(HG_HPS):
            sfin_ref[0, hh] = st_ref[hh].T


def hgrn(proj3, s0, lb, nw):
    b, t, _ = proj3.shape
    ct = min(256, t)
    assert t % ct == 0 and ct % HG_CHUNK == 0 and HG_HEADS % HG_HPS == 0
    w = HG_HPS * HG_DK

    def col(off):
        return pl.BlockSpec((1, ct, w), lambda bi, h, c: (bi, c, off // w + h))

    return pl.pallas_call(
        _hgrn_kernel,
        grid=(b, HG_HEADS // HG_HPS, t // ct),
        in_specs=[
            col(OFF_Q), col(OFF_F), col(OFF_I), col(OFF_G),
            pl.BlockSpec((1, HG_HPS, HG_DK, HG_DV), lambda bi, h, c: (bi, h, 0, 0)),
            pl.BlockSpec((1, w), lambda bi, h, c: (0, h)),
            pl.BlockSpec((1, HG_DV), lambda bi, h, c: (0, 0)),
        ],
        out_specs=[
            pl.BlockSpec((1, ct, w), lambda bi, h, c: (bi, c, h)),
            pl.BlockSpec((1, HG_HPS, HG_DK, HG_DV), lambda bi, h, c: (bi, h, 0, 0)),
        ],
        out_shape=[
            jax.ShapeDtypeStruct((b, t, HG_WIDTH), jnp.bfloat16),
            jax.ShapeDtypeStruct((b, HG_HEADS, HG_DK, HG_DV), jnp.float32),
        ],
        scratch_shapes=[pltpu.VMEM((HG_HPS, HG_DV, HG_DK), jnp.float32)],
        compiler_params=_cparams(("parallel", "parallel", "arbitrary")),
        name="hgrn",
    )(proj3, proj3, proj3, proj3, s0, lb.reshape(1, -1), nw.reshape(1, -1))


def _swa_kernel(*refs, c_off, qc):
    nk = qc + WIN_CHUNKS
    q_ref, k_refs, v_refs = refs[0], refs[1:1 + nk], refs[1 + nk:1 + 2 * nk]
    bias_ref, sink_ref, o_ref = refs[1 + 2 * nk:]
    bf16 = jnp.bfloat16
    c = pl.program_id(1) * qc + c_off
    hd = SWA_HEAD_DIM
    n_keys = bias_ref.shape[-1]
    pad = [jnp.zeros((n_keys - nk * CHUNK, SWA_KV_WIDTH), jnp.float32)] if n_keys > nk * CHUNK else []
    kk = jnp.concatenate([r[0] for r in k_refs] + pad, axis=0)
    vv = jnp.concatenate([r[0] for r in v_refs] + pad, axis=0).astype(bf16)
    kt = kk.T.astype(bf16)
    key_chunk = lax.broadcasted_iota(jnp.int32, (1, n_keys), 1) // CHUNK
    valid = (c - WIN_CHUNKS + key_chunk) >= 0
    zk = jnp.zeros((hd, n_keys), bf16)
    lane = lax.broadcasted_iota(jnp.int32, (n_keys, 2 * hd), 1)
    q = q_ref[0].astype(bf16)
    nq = qc * CHUNK
    for kv in range(SWA_KV_HEADS):
        ktj = kt[kv * hd:(kv + 1) * hd, :]
        kt_lo = jnp.concatenate([ktj, zk], axis=0)
        kt_hi = jnp.concatenate([zk, ktj], axis=0)
        tile = vv[:, (kv // 2) * 2 * hd:(kv // 2 + 1) * 2 * hd]
        own_lo = kv % 2 == 0
        keep = (lane < hd) if own_lo else (lane >= hd)
        v_own = jnp.where(keep, tile, jnp.zeros_like(tile))
        v_swap = pltpu.roll(v_own.astype(jnp.float32), hd, axis=1).astype(bf16)
        v_lo, v_hi = (v_own, v_swap) if own_lo else (v_swap, v_own)
        c0 = kv * SWA_GROUP * hd
        qs = jnp.concatenate([q[:, c0:c0 + 2 * hd], q[:, c0 + 2 * hd:c0 + 4 * hd]], axis=0)
        probs = []
        for i, ktm in enumerate((kt_lo, kt_hi)):
            logits = jnp.dot(qs, ktm, preferred_element_type=jnp.float32)
            logits = logits * SWA_SCALE + bias_ref[kv, i]
            logits = jnp.where(valid, logits, NEG_INF)
            sink = sink_ref[kv, i]
            m = jnp.maximum(jnp.max(logits, axis=-1, keepdims=True), sink)
            e = jnp.exp(logits - m)
            p = e / (jnp.sum(e, axis=-1, keepdims=True) + jnp.exp(sink - m))
            probs.append(p.astype(bf16))
        o = (jnp.dot(probs[0], v_lo, preferred_element_type=jnp.float32)
             + jnp.dot(probs[1], v_hi, preferred_element_type=jnp.float32))
        o_ref[0, :, c0:c0 + 2 * hd] = o[0:nq].astype(o_ref.dtype)
        o_ref[0, :, c0 + 2 * hd:c0 + 4 * hd] = o[nq:2 * nq].astype(o_ref.dtype)


def swa(proj3, rel_bias, sinks, cache_k=None, cache_v=None):
    b, t, _ = proj3.shape
    nc = t // CHUNK
    assert t % CHUNK == 0
    qc = SWA_QCHUNKS if (cache_k is None and nc % SWA_QCHUNKS == 0) else 1
    nk = qc + WIN_CHUNKS
    qspec = pl.BlockSpec((1, qc * CHUNK, SWA_WIDTH), lambda bi, c: (bi, c, OFF_QSW // SWA_WIDTH))

    def chunk(off, j):
        if cache_k is None:
            return pl.BlockSpec(
                (1, CHUNK, SWA_KV_WIDTH),
                lambda bi, c: (bi, jnp.maximum(c * qc - WIN_CHUNKS + j, 0), off // SWA_KV_WIDTH))
        if j < WIN_CHUNKS:
            return pl.BlockSpec((1, CHUNK, SWA_KV_WIDTH), lambda bi, c: (bi, j, 0))
        return pl.BlockSpec((1, CHUNK, SWA_KV_WIDTH), lambda bi, c: (bi, c, off // SWA_KV_WIDTH))

    if cache_k is None:
        karrs, varrs, c_off = [proj3] * nk, [proj3] * nk, 0
    else:
        assert nc == 1 and cache_k.shape[1] == WINDOW
        karrs = [cache_k] * WIN_CHUNKS + [proj3]
        varrs = [cache_v] * WIN_CHUNKS + [proj3]
        c_off = WIN_CHUNKS
    kspecs = [chunk(OFF_KSW, j) for j in range(nk)]
    vspecs = [chunk(OFF_VSW, j) for j in range(nk)]

    n_win = (WIN_CHUNKS + 1) * CHUNK
    n_keys = -(-nk * CHUNK // 128) * 128
    rel = jnp.arange(n_win)[None, :] - (WIN_CHUNKS * CHUNK + jnp.arange(CHUNK))[:, None]
    band = _t5_bias(rel, rel_bias)
    per_q = [jnp.pad(band, ((0, 0), (0, 0), (qi * CHUNK, n_keys - n_win - qi * CHUNK)),
                     constant_values=NEG_INF) for qi in range(qc)]
    bias = jnp.stack(per_q, axis=1)
    bias = bias.reshape(SWA_KV_HEADS, 2, 2, qc * CHUNK, n_keys).transpose(0, 2, 1, 3, 4)
    bias = bias.reshape(SWA_KV_HEADS, 2, 2 * qc * CHUNK, n_keys)
    rows = 2 * qc * CHUNK
    sink_col = jnp.broadcast_to(sinks.astype(jnp.float32).reshape(SWA_KV_HEADS, 2, 2, 1),
                                (SWA_KV_HEADS, 2, 2, qc * CHUNK))
    sink_col = sink_col.transpose(0, 2, 1, 3).reshape(SWA_KV_HEADS, 2, rows, 1)
    return pl.pallas_call(
        functools.partial(_swa_kernel, c_off=c_off, qc=qc),
        grid=(b, nc // qc),
        in_specs=[qspec] + kspecs + vspecs + [
            pl.BlockSpec((SWA_KV_HEADS, 2, rows, n_keys), lambda bi, c: (0, 0, 0, 0)),
            pl.BlockSpec((SWA_KV_HEADS, 2, rows, 1), lambda bi, c: (0, 0, 0, 0)),
        ],
        out_specs=pl.BlockSpec((1, qc * CHUNK, SWA_WIDTH), lambda bi, c: (bi, c, 0)),
        out_shape=jax.ShapeDtypeStruct((b, t, SWA_WIDTH), jnp.bfloat16),
        compiler_params=_cparams(("parallel", "parallel")),
        name="swa",
    )(proj3, *karrs, *varrs, bias, sink_col)


def _t5_bias(rel, table):
    nb = REL_BUCKETS // 2
    max_exact = nb // 2
    side = jnp.where(rel > 0, nb, 0)
    n = jnp.abs(rel)
    n_f = jnp.maximum(n, max_exact).astype(jnp.float32)
    large = max_exact + (jnp.log(n_f / max_exact) / math.log(REL_MAX_DIST / max_exact)
                         * (nb - max_exact)).astype(jnp.int32)
    large = jnp.minimum(large, nb - 1)
    bucket = side + jnp.where(n < max_exact, n, large)
    return jnp.transpose(table[bucket].astype(jnp.float32), (2, 0, 1))


def _outproj_kernel(a1_ref, a2_ref, w_ref, x_ref, o_ref):
    k1 = a1_ref.shape[1]
    acc = jnp.dot(a1_ref[...], w_ref[0:k1, :], preferred_element_type=jnp.float32)
    acc += jnp.dot(a2_ref[...], w_ref[k1:, :], preferred_element_type=jnp.float32)
    o_ref[...] = x_ref[...] + acc


def outproj(a1, a2, w, x, tm=512, tn=2048):
    n, d = x.shape
    k1, k2 = a1.shape[1], a2.shape[1]
    tm = min(tm, n)
    assert n % tm == 0 and d % tn == 0
    return pl.pallas_call(
        _outproj_kernel,
        grid=(n // tm, d // tn),
        in_specs=[
            pl.BlockSpec((tm, k1), lambda i, j: (i, 0)),
            pl.BlockSpec((tm, k2), lambda i, j: (i, 0)),
            pl.BlockSpec((k1 + k2, tn), lambda i, j: (0, j)),
            pl.BlockSpec((tm, tn), lambda i, j: (i, j)),
        ],
        out_specs=pl.BlockSpec((tm, tn), lambda i, j: (i, j)),
        out_shape=jax.ShapeDtypeStruct((n, d), jnp.float32),
        compiler_params=_cparams(("parallel", "arbitrary")),
        name="outproj",
    )(a1, a2, w, x)


def _memattn_kernel(x_ref, nw_ref, wq_ref, mk_ref, mv_ref, wo_ref, o_ref):
    bf16 = jnp.bfloat16
    x = x_ref[0]
    h = _rms(x, nw_ref[...]).astype(bf16)
    q = jnp.dot(h, wq_ref[...], preferred_element_type=jnp.float32)
    mk = mk_ref[0].astype(bf16)
    mv = mv_ref[0].astype(bf16)
    outs = []
    for hh in range(MEM_HEADS):
        sl = slice(hh * MEM_HEAD_DIM, (hh + 1) * MEM_HEAD_DIM)
        logits = lax.dot_general(q[:, sl].astype(bf16), mk[:, sl], _NT,
                                 preferred_element_type=jnp.float32) * MEM_SCALE
        m = jnp.max(logits, axis=-1, keepdims=True)
        e = jnp.exp(logits - m)
        p = e / jnp.sum(e, axis=-1, keepdims=True)
        outs.append(jnp.dot(p.astype(bf16), mv[:, sl], preferred_element_type=jnp.float32))
    o = jnp.concatenate(outs, axis=-1).astype(bf16)
    o_ref[0] = x + jnp.dot(o, wo_ref[...], preferred_element_type=jnp.float32)


def memattn(x3, nw, wq, mk_arr, mv_arr, mk_col, mv_col, wo, tm=512):
    b, t, d = x3.shape
    tm = min(tm, t)
    assert t % tm == 0
    return pl.pallas_call(
        _memattn_kernel,
        grid=(b, t // tm),
        in_specs=[
            pl.BlockSpec((1, tm, d), lambda bi, i: (bi, i, 0)),
            pl.BlockSpec((1, d), lambda bi, i: (0, 0)),
            pl.BlockSpec((d, MEM_INNER), lambda bi, i: (0, 0)),
            pl.BlockSpec((1, N_MEM, MEM_INNER), lambda bi, i: (bi, 0, mk_col)),
            pl.BlockSpec((1, N_MEM, MEM_INNER), lambda bi, i: (bi, 0, mv_col)),
            pl.BlockSpec((MEM_INNER, d), lambda bi, i: (0, 0)),
        ],
        out_specs=pl.BlockSpec((1, tm, d), lambda bi, i: (bi, i, 0)),
        out_shape=jax.ShapeDtypeStruct((b, t, d), jnp.float32),
        compiler_params=_cparams(("parallel", "arbitrary")),
        name="memattn",
    )(x3, nw.reshape(1, d), wq, mk_arr, mv_arr, wo)


def _topk_step(s, row, payload=None):
    n = s.shape[0]
    m = jnp.max(s, axis=0, keepdims=True)
    idx = jnp.min(jnp.where(s == m, row, n), axis=0, keepdims=True)
    hit = row == idx
    pick = idx if payload is None else jnp.sum(jnp.where(hit, payload, 0), axis=0, keepdims=True)
    return m, pick, jnp.where(hit, -jnp.inf, s)


def _head_retrieval_pieces(q_ref, sk_ref, h, eid_out, gate_out, n_pieces):
    bf16 = jnp.bfloat16
    k = PEER_TOPK
    hk = k // 2
    static = isinstance(h, int)
    st = {}

    def reset(s, payload=None):
        st.update(s=s, payload=payload, vals=[], picks=[])

    def score(c):
        col = (h * 2 + c) * PEER_DHALF
        col = col if static else pl.multiple_of(col, PEER_DHALF)
        qh = q_ref[:, pl.ds(col, PEER_DHALF)].astype(bf16)
        reset(lax.dot_general(sk_ref[c, h], qh, _NT, preferred_element_type=jnp.float32))

    def extract(n_it):
        s = st["s"]
        row = lax.broadcasted_iota(jnp.int32, s.shape, 0)
        for _ in range(n_it):
            m, pick, s = _topk_step(s, row, st["payload"])
            st["vals"].append(m)
            st["picks"].append(pick)
        st["s"] = s

    def close(name):
        st[name] = (jnp.concatenate(st["vals"], axis=0), jnp.concatenate(st["picks"], axis=0))

    def candidates():
        (s1, i1), (s2, i2) = st["t0"], st["t1"]
        cand = jnp.concatenate(
            [s1[0:1] + s2] + [s1[a:a + 1] + s2[0:hk] for a in range(1, hk)] + [s1[hk:k] + s2[0:1]],
            axis=0)
        cidx = jnp.concatenate(
            [i1[0:1] * PEER_NKEYS + i2] + [i1[a:a + 1] * PEER_NKEYS + i2[0:hk] for a in range(1, hk)]
            + [i1[hk:k] * PEER_NKEYS + i2[0:1]], axis=0)
        reset(cand, cidx)

    def finish():
        close("t2")
        top_s, eid = st["t2"]
        e = jnp.exp(top_s - top_s[0:1])
        r0 = h * k if static else pl.multiple_of(h * k, k)
        eid_out[pl.ds(r0, k), :] = eid
        gate_out[pl.ds(r0, k), :] = e / jnp.sum(e, axis=0, keepdims=True)

    q4 = [lambda: extract(k // 4)] * 4
    work = ([lambda: score(0)] + q4 + [lambda: (close("t0"), score(1))] + q4
            + [lambda: (close("t1"), candidates())] + q4 + [finish])
    while len(work) > n_pieces:
        f0, f1 = work[0], work[1]
        work[0:2] = [lambda f0=f0, f1=f1: (f0(), f1())]
    return work


def _peer_retrieve_kernel(q_ref, sk_ref, eidx_ref, gate_ref, eid_t, gate_t):
    for h in range(PEER_HEADS):
        for piece in _head_retrieval_pieces(q_ref, sk_ref, h, eid_t, gate_t, 1):
            piece()
    eidx_ref[...] = eid_t[...].T
    gate_ref[...] = gate_t[...].T


def peer_retrieve(qp, subkeys, tn=128):
    n, d = qp.shape
    tn = min(tn, n)
    assert n % tn == 0
    return pl.pallas_call(
        _peer_retrieve_kernel,
        grid=(n // tn,),
        in_specs=[
            pl.BlockSpec((tn, d), lambda i: (i, 0)),
            pl.BlockSpec(subkeys.shape, lambda i: (0, 0, 0, 0)),
        ],
        out_specs=[
            pl.BlockSpec((tn, PEER_PAIRS), lambda i: (i, 0)),
            pl.BlockSpec((tn, PEER_PAIRS), lambda i: (i, 0)),
        ],
        out_shape=[
            jax.ShapeDtypeStruct((n, PEER_PAIRS), jnp.int32),
            jax.ShapeDtypeStruct((n, PEER_PAIRS), jnp.float32),
        ],
        scratch_shapes=[pltpu.VMEM((PEER_PAIRS, tn), jnp.int32),
                        pltpu.VMEM((PEER_PAIRS, tn), jnp.float32)],
        compiler_params=_cparams(("arbitrary",)),
        name="peer_retrieve",
    )(qp, subkeys)


PEER_SLOTS = 8
PEER_AHEAD = PEER_SLOTS - 2
PEER_GROUP = 16
PEER_DMA_QUEUES = 2
PEER_LANES = 128
PEER_EROWS = 16
PEER_PITCH = 20


def _even_odd_chunks(a):
    nc = a.shape[1] // PEER_LANES
    pick = lambda c: a[:, c * PEER_LANES:(c + 1) * PEER_LANES]
    return jnp.concatenate([pick(c) for c in range(0, nc, 2)] + [pick(c) for c in range(1, nc, 2)],
                           axis=1)


def _interleave_chunks(a):
    nc = a.shape[1] // PEER_LANES
    half = nc // 2
    pick = lambda c: a[:, c * PEER_LANES:(c + 1) * PEER_LANES]
    return jnp.concatenate([pick(c // 2 + (half if c % 2 else 0)) for c in range(nc)], axis=1)


def _peer_main_kernel(eidx0_ref, gate0_ref, qn_ref, sk_ref, x_ref, nw_ref, fw_ref, uv_hbm, o_ref,
                      h_ref, y_ref, dup_ref, w2_ref, eidx_ref, gate_ref, eid_t, gate_t,
                      eid_next, gate_next, *scratch):
    bf16 = jnp.bfloat16
    f32 = jnp.float32
    bufs, sem, csem = scratch[:PEER_SLOTS], scratch[PEER_SLOTS], scratch[PEER_SLOTS + 1]
    tb, d = x_ref.shape
    half = d // 2
    n_rows = PEER_EROWS // 2
    lanes2 = 2 * PEER_PAIRS

    @pl.when(pl.program_id(0) == 0)
    def _():
        eid_next[...] = eidx0_ref[...]
        gate_next[...] = gate0_ref[...]

    to_smem = pltpu.make_async_copy(eid_next, eidx_ref, csem)
    to_smem.start()
    gate_ref[...] = gate_next[...]
    h_ref[...] = _even_odd_chunks(_rms(x_ref[...], nw_ref[...])).astype(bf16).astype(f32)
    dup_ref[...] = (lax.broadcasted_iota(jnp.int32, (PEER_PAIRS, lanes2), 1) // 2
                    == lax.broadcasted_iota(jnp.int32, (PEER_PAIRS, lanes2), 0)).astype(bf16)
    to_smem.wait()

    rows = PEER_PAIRS * PEER_EROWS

    def issue(t, slot, part=0, parts=1):
        n_p = PEER_PAIRS // parts
        for p in range(part * n_p, (part + 1) * n_p):
            r = pl.multiple_of(eidx_ref[t, p] * PEER_EROWS, PEER_EROWS)
            pltpu.make_async_copy(uv_hbm.at[pl.ds(r, PEER_EROWS)],
                                  bufs[slot].at[pl.ds(p * PEER_PITCH, PEER_EROWS)],
                                  sem.at[slot]).start(priority=p % PEER_DMA_QUEUES)

    def wait(slot):
        pltpu.make_async_copy(uv_hbm.at[pl.ds(0, rows)], bufs[slot].at[pl.ds(0, rows)],
                              sem.at[slot]).wait()

    def words(slot, m):
        return bufs[slot][pl.ds(m, PEER_PAIRS, stride=PEER_PITCH), :]

    sub_w = lax.broadcasted_iota(jnp.int32, (8, lanes2), 0)
    par_w = lax.broadcasted_iota(jnp.int32, (8, lanes2), 1) % 2

    def u_side(t, slot, between=None):
        hrow = h_ref[pl.ds(t, 1), :]
        acc = None
        for m in range(n_rows):
            if between is not None:
                between(m)
            wd = words(slot, m)
            lo = lax.bitcast_convert_type(wd << 16, f32)
            hi = lax.bitcast_convert_type(wd & jnp.int32(-65536), f32)
            term = (lo * hrow[:, m * PEER_LANES:(m + 1) * PEER_LANES]
                    + hi * hrow[:, half + m * PEER_LANES:half + (m + 1) * PEER_LANES])
            acc = term if acc is None else acc + term
        act = jnp.sum(acc.T, axis=0, keepdims=True)
        gelu = 0.5 * act * (1.0 + lax.erf(act * np.float32(math.sqrt(0.5))))
        w = gate_ref[pl.ds(t, 1), :] * gelu
        wb = jnp.broadcast_to(w, (8, PEER_PAIRS)).astype(bf16)
        wd2 = jnp.dot(wb, dup_ref[...], preferred_element_type=f32)
        w2_ref[slot] = jnp.where(par_w == sub_w, wd2, 0.0)

    def v_side(t, slot):
        wv = jnp.concatenate([pltpu.bitcast(words(slot, n_rows + m), bf16)
                              for m in range(n_rows)], axis=1)
        y2 = jnp.dot(w2_ref[slot].astype(bf16), wv, preferred_element_type=f32)
        y_ref[pl.ds(t, 1), :] = jnp.concatenate([y2[0:1], y2[1:2]], axis=1)

    def step(t, j, do_issue=True):
        wait(j % PEER_SLOTS)
        spread = (lambda m: issue(t + PEER_AHEAD, (j + PEER_AHEAD) % PEER_SLOTS, m, n_rows)
                  ) if do_issue else None
        u_side(t, j % PEER_SLOTS, spread)
        v_side(t - 1, (j - 1) % PEER_SLOTS)

    for t in range(PEER_AHEAD):
        issue(t, t)
    wait(0)
    u_side(0, 0)
    issue(PEER_AHEAD, PEER_AHEAD)

    def body(g, carry):
        t0 = 1 + g * PEER_GROUP
        pieces = _head_retrieval_pieces(qn_ref, sk_ref, g, eid_t, gate_t, PEER_GROUP)
        for j in range(PEER_GROUP):
            step(t0 + j, 1 + j)
            pieces[j]()
        return carry

    n_main = (tb - PEER_AHEAD - 1) // PEER_GROUP
    assert n_main == PEER_HEADS - 1
    lax.fori_loop(0, n_main, body, 0)
    tail = range(1 + n_main * PEER_GROUP, tb)
    pieces = _head_retrieval_pieces(qn_ref, sk_ref, PEER_HEADS - 1, eid_t, gate_t, len(tail))
    for t, piece in zip(tail, pieces):
        step(t, t, do_issue=t + PEER_AHEAD < tb)
        piece()
    v_side(tb - 1, (tb - 1) % PEER_SLOTS)
    eid_next[...] = eid_t[...].T
    gate_next[...] = gate_t[...].T


    o_ref[...] = _rms(x_ref[...] + _interleave_chunks(y_ref[...]), fw_ref[...])


def make_peer_table(u_tab, v_tab):
    e, d = u_tab.shape
    te = 256
    assert e % te == 0 and d == PEER_EROWS * PEER_LANES
    return pl.pallas_call(
        _pack_table_kernel,
        grid=(e // te,),
        in_specs=[pl.BlockSpec((te, d), lambda i: (i, 0)), pl.BlockSpec((te, d), lambda i: (i, 0))],
        out_specs=pl.BlockSpec((te * PEER_EROWS, PEER_LANES), lambda i: (i, 0)),
        out_shape=jax.ShapeDtypeStruct((e * PEER_EROWS, PEER_LANES), jnp.int32),
        compiler_params=_cparams(("arbitrary",)),
        name="peer_pack",
    )(u_tab, v_tab)


def _pack_table_kernel(u_ref, v_ref, o_ref):
    te = u_ref.shape[0]
    half = PEER_EROWS // 2

    def bf16_bits(x):
        return lax.bitcast_convert_type(x.astype(jnp.bfloat16).astype(jnp.float32), jnp.int32)

    for src, m0 in ((u_ref, 0), (v_ref, half)):
        for m in range(half):
            lo = bf16_bits(src[:, (2 * m) * PEER_LANES:(2 * m + 1) * PEER_LANES])
            hi = bf16_bits(src[:, (2 * m + 1) * PEER_LANES:(2 * m + 2) * PEER_LANES])
            word = lax.shift_right_logical(lo, jnp.int32(16)) | (hi & jnp.int32(-65536))
            o_ref[pl.ds(m0 + m, te, stride=PEER_EROWS), :] = word


def peer_main(qp, subkeys, x, nw, fw, uv_tab, tb=128):
    n, d = x.shape
    tb = min(tb, n)
    assert n % tb == 0 and tb > PEER_AHEAD + PEER_SLOTS + 1 and PEER_GROUP % PEER_SLOTS == 0
    assert uv_tab.shape[1] == PEER_LANES and d == PEER_EROWS * PEER_LANES
    nblk = n // tb
    eidx0, gate0 = peer_retrieve(qp[:tb], subkeys, tn=tb)
    first = lambda i: (0, 0)
    return pl.pallas_call(
        _peer_main_kernel,
        grid=(nblk,),
        in_specs=[
            pl.BlockSpec((tb, PEER_PAIRS), first),
            pl.BlockSpec((tb, PEER_PAIRS), first),
            pl.BlockSpec((tb, qp.shape[1]), lambda i: (jnp.minimum(i + 1, nblk - 1), 0)),
            pl.BlockSpec(subkeys.shape, lambda i: (0, 0, 0, 0)),
            pl.BlockSpec((tb, d), lambda i: (i, 0)),
            pl.BlockSpec((1, d), first),
            pl.BlockSpec((1, d), first),
            pl.BlockSpec(memory_space=pl.ANY),
        ],
        out_specs=pl.BlockSpec((tb, d), lambda i: (i, 0)),
        out_shape=jax.ShapeDtypeStruct((n, d), jnp.float32),
        scratch_shapes=[
            pltpu.VMEM((tb, d), jnp.float32),
            pltpu.VMEM((tb, d), jnp.float32),
            pltpu.VMEM((PEER_PAIRS, 2 * PEER_PAIRS), jnp.bfloat16),
            pltpu.VMEM((PEER_SLOTS, 8, 2 * PEER_PAIRS), jnp.float32),
            pltpu.SMEM((tb, PEER_PAIRS), jnp.int32),
            pltpu.VMEM((tb, PEER_PAIRS), jnp.float32),
            pltpu.VMEM((PEER_PAIRS, tb), jnp.int32),
            pltpu.VMEM((PEER_PAIRS, tb), jnp.float32),
            pltpu.VMEM((tb, PEER_PAIRS), jnp.int32),
            pltpu.VMEM((tb, PEER_PAIRS), jnp.float32),
            *[pltpu.VMEM((PEER_PAIRS * PEER_PITCH, PEER_LANES), jnp.int32)
              for _ in range(PEER_SLOTS)],
            pltpu.SemaphoreType.DMA((PEER_SLOTS,)),
            pltpu.SemaphoreType.DMA(()),
        ],
        compiler_params=_cparams(("arbitrary",)),
        name="peer_main",
    )(eidx0, gate0, qp, subkeys, x, nw.reshape(1, d), fw.reshape(1, d), uv_tab)


def _trunk(x, mk_arr, mv_arr, mk_col, mv_col, s0, cache_k, cache_v, p):
    b, t, d = x.shape
    n = b * t
    proj = norm_matmul(x.reshape(n, d), p["norm_mix_w"], p["w_in"], tn=IN_WIDTH // 2)
    proj3 = proj.reshape(b, t, IN_WIDTH)
    o_hg, s_fin = hgrn(proj3, s0, p["lb"], p["hg_norm_w"])
    o_sw = swa(proj3, p["rel_bias"], p["swa_sinks"], cache_k, cache_v)
    x1 = outproj(o_hg.reshape(n, HG_WIDTH), o_sw.reshape(n, SWA_WIDTH), p["w_out"], x.reshape(n, d))
    x2 = memattn(x1.reshape(b, t, d), p["norm_cross_w"], p["mem_wq"], mk_arr, mv_arr,
                 mk_col, mv_col, p["mem_wo"]).reshape(n, d)
    qp = norm_matmul(x2, p["norm_ffn_w"], p["peer_wq"], tn=2048)
    y = peer_main(qp, p["peer_subkeys"], x2, p["norm_ffn_w"], p["final_norm_w"], p["peer_uv"])
    keep = min(WINDOW, t) if cache_k is None else t
    k_rows = proj3[:, t - keep:, OFF_KSW:OFF_KSW + SWA_KV_WIDTH]
    v_rows = proj3[:, t - keep:, OFF_VSW:OFF_VSW + SWA_KV_WIDTH]
    k_rows = k_rows.reshape(b, keep, SWA_KV_HEADS, SWA_HEAD_DIM)
    v_rows = v_rows.reshape(b, keep, SWA_KV_HEADS, SWA_HEAD_DIM)
    return y.reshape(b, t, d), s_fin, k_rows, v_rows


def kernel(x_prompt, x_sample, mem_prompt, state_hgrn, cache_swa_k, cache_swa_v, cache_mem_k, cache_mem_v, rel_bias, hg_lb_logits, norm_mix_w, w_in, hg_norm_w, swa_sinks, w_out, norm_mem_w, norm_cross_w, mem_wq, mem_wk, mem_wv, mem_wo, norm_ffn_w, peer_wq, peer_subkeys, peer_u, peer_v, final_norm_w):
    bf16 = jnp.bfloat16
    depth = w_in.shape[0]
    assert depth == 1
    l = 0
    lb_all = jnp.cumsum(jax.nn.softmax(hg_lb_logits.astype(jnp.float32), axis=0), axis=0)
    p = {
        "norm_mix_w": norm_mix_w[l], "w_in": w_in[l].astype(bf16), "lb": lb_all[l],
        "hg_norm_w": hg_norm_w[l], "rel_bias": rel_bias, "swa_sinks": swa_sinks[l],
        "w_out": w_out[l].astype(bf16), "norm_cross_w": norm_cross_w[l],
        "mem_wq": mem_wq[l].astype(bf16), "mem_wo": mem_wo[l].astype(bf16),
        "norm_ffn_w": norm_ffn_w[l], "peer_wq": peer_wq[l].astype(bf16),
        "peer_subkeys": peer_subkeys[l].astype(bf16), "peer_uv": make_peer_table(peer_u[l], peer_v[l]),
        "final_norm_w": final_norm_w,
    }
    bp, tp, d = x_prompt.shape
    bs, ts, _ = x_sample.shape

    wkv = jnp.concatenate([mem_wk[l], mem_wv[l]], axis=1).astype(bf16)
    kv = norm_matmul(mem_prompt.reshape(bp * N_MEM, d), norm_mem_w[l], wkv)
    kv3 = kv.reshape(bp, N_MEM, 2 * MEM_INNER)
    mk = kv3[:, :, :MEM_INNER].reshape(bp, N_MEM, MEM_HEADS, MEM_HEAD_DIM)
    mv = kv3[:, :, MEM_INNER:].reshape(bp, N_MEM, MEM_HEADS, MEM_HEAD_DIM)

    s0 = jnp.zeros((bp, HG_HEADS, HG_DK, HG_DV), jnp.float32)
    yp, sp, kp, vp = _trunk(x_prompt, kv3, kv3, 0, 1, s0, None, None, p)

    cmk = cache_mem_k[l].reshape(bs, N_MEM, MEM_INNER)
    cmv = cache_mem_v[l].reshape(bs, N_MEM, MEM_INNER)
    ck = cache_swa_k[l].reshape(bs, -1, SWA_KV_WIDTH)
    cv = cache_swa_v[l].reshape(bs, -1, SWA_KV_WIDTH)
    ys, ss, ks_new, vs_new = _trunk(x_sample, cmk, cmv, 0, 0, state_hgrn[l], ck, cv, p)

    return (yp, ys, sp[None], kp[None], vp[None], mk[None], mv[None],
            ss[None], ks_new[None], vs_new[None])
```

```python
import functools
import math

import numpy as np
import jax
import jax.numpy as jnp
from jax import lax
from jax.experimental import pallas as pl
from jax.experimental.pallas import tpu as pltpu

EPS = 1e-6
NEG_INF = -1e30
CHUNK = 64

HG_HEADS = 8
HG_DK = 128
HG_DV = 128
HG_WIDTH = HG_HEADS * HG_DV
HG_SUB = 16
HG_CHUNK = 64
HG_HPS = 8

SWA_HEADS = 16
SWA_KV_HEADS = 4
SWA_GROUP = SWA_HEADS // SWA_KV_HEADS
SWA_HEAD_DIM = 64
SWA_WIDTH = SWA_HEADS * SWA_HEAD_DIM
SWA_KV_WIDTH = SWA_KV_HEADS * SWA_HEAD_DIM
SWA_SCALE = SWA_HEAD_DIM ** -0.5
WINDOW = 128
WIN_CHUNKS = WINDOW // CHUNK
SWA_QCHUNKS = 2
REL_BUCKETS = 32
REL_MAX_DIST = 128

N_MEM = 256
MEM_HEADS = 4
MEM_HEAD_DIM = 128
MEM_INNER = MEM_HEADS * MEM_HEAD_DIM
MEM_SCALE = MEM_HEAD_DIM ** -0.5

PEER_HEADS = 8
PEER_NKEYS = 128
PEER_DHALF = 128
PEER_TOPK = 16
PEER_PAIRS = PEER_HEADS * PEER_TOPK

OFF_Q, OFF_F, OFF_I, OFF_G = 0, 1024, 2048, 3072
OFF_QSW, OFF_KSW, OFF_VSW = 4096, 5120, 5376
IN_WIDTH = 5632

VMEM_LIMIT_BYTES = 56 * 1024 * 1024

_NT = (((1,), (1,)), ((), ()))
_TN = (((0,), (0,)), ((), ()))


def _cparams(sem):
    return pltpu.CompilerParams(dimension_semantics=sem, vmem_limit_bytes=VMEM_LIMIT_BYTES)


def _rms(x, w):
    return x * lax.rsqrt(jnp.mean(x * x, axis=-1, keepdims=True) + EPS) * w


def _norm_matmul_kernel(x_ref, nw_ref, w_ref, o_ref, h_ref):
    @pl.when(pl.program_id(1) == 0)
    def _():
        h_ref[...] = _rms(x_ref[...], nw_ref[...]).astype(h_ref.dtype)

    o_ref[...] = jnp.dot(h_ref[...], w_ref[...], preferred_element_type=jnp.float32)


def norm_matmul(x, nw, w, tm=512, tn=512):
    n, d = x.shape
    m = w.shape[1]
    tm = min(tm, n)
    tn = min(tn, m)
    assert n % tm == 0 and m % tn == 0
    return pl.pallas_call(
        _norm_matmul_kernel,
        grid=(n // tm, m // tn),
        in_specs=[
            pl.BlockSpec((tm, d), lambda i, j: (i, 0)),
            pl.BlockSpec((1, d), lambda i, j: (0, 0)),
            pl.BlockSpec((d, tn), lambda i, j: (0, j)),
        ],
        out_specs=pl.BlockSpec((tm, tn), lambda i, j: (i, j)),
        out_shape=jax.ShapeDtypeStruct((n, m), jnp.float32),
        scratch_shapes=[pltpu.VMEM((tm, d), jnp.bfloat16)],
        compiler_params=_cparams(("parallel", "arbitrary")),
        name="norm_matmul",
    )(x, nw.reshape(1, d), w)


def _cumsum_rows(x):
    n = x.shape[0]
    row = lax.broadcasted_iota(jnp.int32, x.shape, 0)
    s = 1
    while s < n:
        x = x + jnp.where(row >= s, pltpu.roll(x, s, axis=0), 0.0)
        s *= 2
    return x


def _bcast_rows(x, idxs, g):
    return jnp.concatenate(
        [jnp.broadcast_to(x[i:i + 1, :], (g, x.shape[1])) for i in idxs], axis=0)


def _hgrn_chunk(q, fpre, v, lb, st):
    c = q.shape[0]
    f = lb + (1.0 - lb) * jax.nn.sigmoid(fpre)
    k = 1.0 - f
    lf = jnp.log(f)
    bc = _cumsum_rows(lf)
    be = bc - lf
    bf16 = jnp.bfloat16
    ti = lax.broadcasted_iota(jnp.int32, (c, c), 0)
    si = lax.broadcasted_iota(jnp.int32, (c, c), 1)

    ref0 = _bcast_rows(be, range(0, c, HG_SUB), HG_SUB)
    qd = (q * jnp.exp(bc - ref0)).astype(bf16)
    kd = (k * jnp.exp(ref0 - bc)).astype(bf16)
    a = lax.dot_general(qd, kd, _NT, preferred_element_type=jnp.float32)
    attn = jnp.where((ti // HG_SUB == si // HG_SUB) & (si <= ti), a, 0.0)
    g = 2 * HG_SUB
    while g <= c:
        half = g // 2
        ref = _bcast_rows(bc, range(half - 1, c, g), g)
        ql = (q * jnp.exp(jnp.minimum(bc - ref, 0.0))).astype(bf16)
        kl = (k * jnp.exp(jnp.minimum(ref - bc, 0.0))).astype(bf16)
        a = lax.dot_general(ql, kl, _NT, preferred_element_type=jnp.float32)
        m = (ti // g == si // g) & (ti % g >= half) & (si % g < half)
        attn = jnp.where(m, a, attn)
        g *= 2

    vb = v.astype(bf16)
    q_in = (q * jnp.exp(bc)).astype(bf16)
    o = (lax.dot_general(q_in, st.astype(bf16), _NT, preferred_element_type=jnp.float32)
         + jnp.dot(attn.astype(bf16), vb, preferred_element_type=jnp.float32))
    b_last = bc[c - 1:c, :]
    k_out = (k * jnp.exp(b_last - bc)).astype(bf16)
    st_new = st * jnp.exp(b_last) + lax.dot_general(vb, k_out, _TN,
                                                    preferred_element_type=jnp.float32)
    return o, st_new


def _hgrn_kernel(q_ref, f_ref, i_ref, g_ref, s0_ref, lb_ref, nw_ref, o_ref, sfin_ref, st_ref):
    ci = pl.program_id(2)

    @pl.when(ci == 0)
    def _():
        for hh in range(HG_HPS):
            st_ref[hh] = s0_ref[0, hh].T

    ct = q_ref.shape[1]
    nw = nw_ref[...]
    for j in range(ct // HG_CHUNK):
        sl = pl.ds(j * HG_CHUNK, HG_CHUNK)
        for hh in range(HG_HPS):
            hs = slice(hh * HG_DK, (hh + 1) * HG_DK)
            o, st_new = _hgrn_chunk(q_ref[0, sl, hs], f_ref[0, sl, hs], i_ref[0, sl, hs],
                                    lb_ref[:, hs], st_ref[hh])
            st_ref[hh] = st_new
            o = _rms(o, nw)
            o_ref[0, sl, hs] = (o * jax.nn.silu(g_ref[0, sl, hs])).astype(o_ref.dtype)

    @pl.when(ci == pl.num_programs(2) - 1)
    def _():
        for hh in range(HG_HPS):
            sfin_ref[0, hh] = st_ref[hh].T


def hgrn(proj3, s0, lb, nw):
    b, t, _ = proj3.shape
    ct = min(256, t)
    assert t % ct == 0 and ct % HG_CHUNK == 0 and HG_HEADS % HG_HPS == 0
    w = HG_HPS * HG_DK

    def col(off):
        return pl.BlockSpec((1, ct, w), lambda bi, h, c: (bi, c, off // w + h))

    return pl.pallas_call(
        _hgrn_kernel,
        grid=(b, HG_HEADS // HG_HPS, t // ct),
        in_specs=[
            col(OFF_Q), col(OFF_F), col(OFF_I), col(OFF_G),
            pl.BlockSpec((1, HG_HPS, HG_DK, HG_DV), lambda bi, h, c: (bi, h, 0, 0)),
            pl.BlockSpec((1, w), lambda bi, h, c: (0, h)),
            pl.BlockSpec((1, HG_DV), lambda bi, h, c: (0, 0)),
        ],
        out_specs=[
            pl.BlockSpec((1, ct, w), lambda bi, h, c: (bi, c, h)),
            pl.BlockSpec((1, HG_HPS, HG_DK, HG_DV), lambda bi, h, c: (bi, h, 0, 0)),
        ],
        out_shape=[
            jax.ShapeDtypeStruct((b, t, HG_WIDTH), jnp.bfloat16),
            jax.ShapeDtypeStruct((b, HG_HEADS, HG_DK, HG_DV), jnp.float32),
        ],
        scratch_shapes=[pltpu.VMEM((HG_HPS, HG_DV, HG_DK), jnp.float32)],
        compiler_params=_cparams(("parallel", "parallel", "arbitrary")),
        name="hgrn",
    )(proj3, proj3, proj3, proj3, s0, lb.reshape(1, -1), nw.reshape(1, -1))


def _swa_kernel(*refs, c_off, qc):
    nk = qc + WIN_CHUNKS
    q_ref, k_refs, v_refs = refs[0], refs[1:1 + nk], refs[1 + nk:1 + 2 * nk]
    bias_ref, sink_ref, o_ref = refs[1 + 2 * nk:]
    bf16 = jnp.bfloat16
    c = pl.program_id(1) * qc + c_off
    hd = SWA_HEAD_DIM
    n_keys = bias_ref.shape[-1]
    pad = [jnp.zeros((n_keys - nk * CHUNK, SWA_KV_WIDTH), jnp.float32)] if n_keys > nk * CHUNK else []
    kk = jnp.concatenate([r[0] for r in k_refs] + pad, axis=0)
    vv = jnp.concatenate([r[0] for r in v_refs] + pad, axis=0).astype(bf16)
    kt = kk.T.astype(bf16)
    key_chunk = lax.broadcasted_iota(jnp.int32, (1, n_keys), 1) // CHUNK
    valid = (c - WIN_CHUNKS + key_chunk) >= 0
    zk = jnp.zeros((hd, n_keys), bf16)
    lane = lax.broadcasted_iota(jnp.int32, (n_keys, 2 * hd), 1)
    q = q_ref[0].astype(bf16)
    nq = qc * CHUNK
    for kv in range(SWA_KV_HEADS):
        ktj = kt[kv * hd:(kv + 1) * hd, :]
        kt_lo = jnp.concatenate([ktj, zk], axis=0)
        kt_hi = jnp.concatenate([zk, ktj], axis=0)
        tile = vv[:, (kv // 2) * 2 * hd:(kv // 2 + 1) * 2 * hd]
        own_lo = kv % 2 == 0
        keep = (lane < hd) if own_lo else (lane >= hd)
        v_own = jnp.where(keep, tile, jnp.zeros_like(tile))
        v_swap = pltpu.roll(v_own.astype(jnp.float32), hd, axis=1).astype(bf16)
        v_lo, v_hi = (v_own, v_swap) if own_lo else (v_swap, v_own)
        c0 = kv * SWA_GROUP * hd
        qs = jnp.concatenate([q[:, c0:c0 + 2 * hd], q[:, c0 + 2 * hd:c0 + 4 * hd]], axis=0)
        probs = []
        for i, ktm in enumerate((kt_lo, kt_hi)):
            logits = jnp.dot(qs, ktm, preferred_element_type=jnp.float32)
            logits = logits * SWA_SCALE + bias_ref[kv, i]
            logits = jnp.where(valid, logits, NEG_INF)
            sink = sink_ref[kv, i]
            m = jnp.maximum(jnp.max(logits, axis=-1, keepdims=True), sink)
            e = jnp.exp(logits - m)
            p = e / (jnp.sum(e, axis=-1, keepdims=True) + jnp.exp(sink - m))
            probs.append(p.astype(bf16))
        o = (jnp.dot(probs[0], v_lo, preferred_element_type=jnp.float32)
             + jnp.dot(probs[1], v_hi, preferred_element_type=jnp.float32))
        o_ref[0, :, c0:c0 + 2 * hd] = o[0:nq].astype(o_ref.dtype)
        o_ref[0, :, c0 + 2 * hd:c0 + 4 * hd] = o[nq:2 * nq].astype(o_ref.dtype)


def swa(proj3, rel_bias, sinks, cache_k=None, cache_v=None):
    b, t, _ = proj3.shape
    nc = t // CHUNK
    assert t % CHUNK == 0
    qc = SWA_QCHUNKS if (cache_k is None and nc % SWA_QCHUNKS == 0) else 1
    nk = qc + WIN_CHUNKS
    qspec = pl.BlockSpec((1, qc * CHUNK, SWA_WIDTH), lambda bi, c: (bi, c, OFF_QSW // SWA_WIDTH))

    def chunk(off, j):
        if cache_k is None:
            return pl.BlockSpec(
                (1, CHUNK, SWA_KV_WIDTH),
                lambda bi, c: (bi, jnp.maximum(c * qc - WIN_CHUNKS + j, 0), off // SWA_KV_WIDTH))
        if j < WIN_CHUNKS:
            return pl.BlockSpec((1, CHUNK, SWA_KV_WIDTH), lambda bi, c: (bi, j, 0))
        return pl.BlockSpec((1, CHUNK, SWA_KV_WIDTH), lambda bi, c: (bi, c, off // SWA_KV_WIDTH))

    if cache_k is None:
        karrs, varrs, c_off = [proj3] * nk, [proj3] * nk, 0
    else:
        assert nc == 1 and cache_k.shape[1] == WINDOW
        karrs = [cache_k] * WIN_CHUNKS + [proj3]
        varrs = [cache_v] * WIN_CHUNKS + [proj3]
        c_off = WIN_CHUNKS
    kspecs = [chunk(OFF_KSW, j) for j in range(nk)]
    vspecs = [chunk(OFF_VSW, j) for j in range(nk)]

    n_win = (WIN_CHUNKS + 1) * CHUNK
    n_keys = -(-nk * CHUNK // 128) * 128
    rel = jnp.arange(n_win)[None, :] - (WIN_CHUNKS * CHUNK + jnp.arange(CHUNK))[:, None]
    band = _t5_bias(rel, rel_bias)
    per_q = [jnp.pad(band, ((0, 0), (0, 0), (qi * CHUNK, n_keys - n_win - qi * CHUNK)),
                     constant_values=NEG_INF) for qi in range(qc)]
    bias = jnp.stack(per_q, axis=1)
    bias = bias.reshape(SWA_KV_HEADS, 2, 2, qc * CHUNK, n_keys).transpose(0, 2, 1, 3, 4)
    bias = bias.reshape(SWA_KV_HEADS, 2, 2 * qc * CHUNK, n_keys)
    rows = 2 * qc * CHUNK
    sink_col = jnp.broadcast_to(sinks.astype(jnp.float32).reshape(SWA_KV_HEADS, 2, 2, 1),
                                (SWA_KV_HEADS, 2, 2, qc * CHUNK))
    sink_col = sink_col.transpose(0, 2, 1, 3).reshape(SWA_KV_HEADS, 2, rows, 1)
    return pl.pallas_call(
        functools.partial(_swa_kernel, c_off=c_off, qc=qc),
        grid=(b, nc // qc),
        in_specs=[qspec] + kspecs + vspecs + [
            pl.BlockSpec((SWA_KV_HEADS, 2, rows, n_keys), lambda bi, c: (0, 0, 0, 0)),
            pl.BlockSpec((SWA_KV_HEADS, 2, rows, 1), lambda bi, c: (0, 0, 0, 0)),
        ],
        out_specs=pl.BlockSpec((1, qc * CHUNK, SWA_WIDTH), lambda bi, c: (bi, c, 0)),
        out_shape=jax.ShapeDtypeStruct((b, t, SWA_WIDTH), jnp.bfloat16),
        compiler_params=_cparams(("parallel", "parallel")),
        name="swa",
    )(proj3, *karrs, *varrs, bias, sink_col)


def _t5_bias(rel, table):
    nb = REL_BUCKETS // 2
    max_exact = nb // 2
    side = jnp.where(rel > 0, nb, 0)
    n = jnp.abs(rel)
    n_f = jnp.maximum(n, max_exact).astype(jnp.float32)
    large = max_exact + (jnp.log(n_f / max_exact) / math.log(REL_MAX_DIST / max_exact)
                         * (nb - max_exact)).astype(jnp.int32)
    large = jnp.minimum(large, nb - 1)
    bucket = side + jnp.where(n < max_exact, n, large)
    return jnp.transpose(table[bucket].astype(jnp.float32), (2, 0, 1))


def _outproj_kernel(a1_ref, a2_ref, w_ref, x_ref, o_ref):
    k1 = a1_ref.shape[1]
    acc = jnp.dot(a1_ref[...], w_ref[0:k1, :], preferred_element_type=jnp.float32)
    acc += jnp.dot(a2_ref[...], w_ref[k1:, :], preferred_element_type=jnp.float32)
    o_ref[...] = x_ref[...] + acc


def outproj(a1, a2, w, x, tm=512, tn=2048):
    n, d = x.shape
    k1, k2 = a1.shape[1], a2.shape[1]
    tm = min(tm, n)
    assert n % tm == 0 and d % tn == 0
    return pl.pallas_call(
        _outproj_kernel,
        grid=(n // tm, d // tn),
        in_specs=[
            pl.BlockSpec((tm, k1), lambda i, j: (i, 0)),
            pl.BlockSpec((tm, k2), lambda i, j: (i, 0)),
            pl.BlockSpec((k1 + k2, tn), lambda i, j: (0, j)),
            pl.BlockSpec((tm, tn), lambda i, j: (i, j)),
        ],
        out_specs=pl.BlockSpec((tm, tn), lambda i, j: (i, j)),
        out_shape=jax.ShapeDtypeStruct((n, d), jnp.float32),
        compiler_params=_cparams(("parallel", "arbitrary")),
        name="outproj",
    )(a1, a2, w, x)


def _memattn_kernel(x_ref, nw_ref, wq_ref, mk_ref, mv_ref, wo_ref, o_ref):
    bf16 = jnp.bfloat16
    x = x_ref[0]
    h = _rms(x, nw_ref[...]).astype(bf16)
    q = jnp.dot(h, wq_ref[...], preferred_element_type=jnp.float32)
    mk = mk_ref[0].astype(bf16)
    mv = mv_ref[0].astype(bf16)
    outs = []
    for hh in range(MEM_HEADS):
        sl = slice(hh * MEM_HEAD_DIM, (hh + 1) * MEM_HEAD_DIM)
        logits = lax.dot_general(q[:, sl].astype(bf16), mk[:, sl], _NT,
                                 preferred_element_type=jnp.float32) * MEM_SCALE
        m = jnp.max(logits, axis=-1, keepdims=True)
        e = jnp.exp(logits - m)
        p = e / jnp.sum(e, axis=-1, keepdims=True)
        outs.append(jnp.dot(p.astype(bf16), mv[:, sl], preferred_element_type=jnp.float32))
    o = jnp.concatenate(outs, axis=-1).astype(bf16)
    o_ref[0] = x + jnp.dot(o, wo_ref[...], preferred_element_type=jnp.float32)


def memattn(x3, nw, wq, mk_arr, mv_arr, mk_col, mv_col, wo, tm=512):
    b, t, d = x3.shape
    tm = min(tm, t)
    assert t % tm == 0
    return pl.pallas_call(
        _memattn_kernel,
        grid=(b, t // tm),
        in_specs=[
            pl.BlockSpec((1, tm, d), lambda bi, i: (bi, i, 0)),
            pl.BlockSpec((1, d), lambda bi, i: (0, 0)),
            pl.BlockSpec((d, MEM_INNER), lambda bi, i: (0, 0)),
            pl.BlockSpec((1, N_MEM, MEM_INNER), lambda bi, i: (bi, 0, mk_col)),
            pl.BlockSpec((1, N_MEM, MEM_INNER), lambda bi, i: (bi, 0, mv_col)),
            pl.BlockSpec((MEM_INNER, d), lambda bi, i: (0, 0)),
        ],
        out_specs=pl.BlockSpec((1, tm, d), lambda bi, i: (bi, i, 0)),
        out_shape=jax.ShapeDtypeStruct((b, t, d), jnp.float32),
        compiler_params=_cparams(("parallel", "arbitrary")),
        name="memattn",
    )(x3, nw.reshape(1, d), wq, mk_arr, mv_arr, wo)


def _topk_step(s, row, payload=None):
    n = s.shape[0]
    m = jnp.max(s, axis=0, keepdims=True)
    idx = jnp.min(jnp.where(s == m, row, n), axis=0, keepdims=True)
    hit = row == idx
    pick = idx if payload is None else jnp.sum(jnp.where(hit, payload, 0), axis=0, keepdims=True)
    return m, pick, jnp.where(hit, -jnp.inf, s)


def _head_retrieval_pieces(q_ref, sk_ref, h, eid_out, gate_out, n_pieces):
    bf16 = jnp.bfloat16
    k = PEER_TOPK
    hk = k // 2
    static = isinstance(h, int)
    st = {}

    def reset(s, payload=None):
        st.update(s=s, payload=payload, vals=[], picks=[])

    def score(c):
        col = (h * 2 + c) * PEER_DHALF
        col = col if static else pl.multiple_of(col, PEER_DHALF)
        qh = q_ref[:, pl.ds(col, PEER_DHALF)].astype(bf16)
        reset(lax.dot_general(sk_ref[c, h], qh, _NT, preferred_element_type=jnp.float32))

    def extract(n_it):
        s = st["s"]
        row = lax.broadcasted_iota(jnp.int32, s.shape, 0)
        for _ in range(n_it):
            m, pick, s = _topk_step(s, row, st["payload"])
            st["vals"].append(m)
            st["picks"].append(pick)
        st["s"] = s

    def close(name):
        st[name] = (jnp.concatenate(st["vals"], axis=0), jnp.concatenate(st["picks"], axis=0))

    def candidates():
        (s1, i1), (s2, i2) = st["t0"], st["t1"]
        cand = jnp.concatenate(
            [s1[0:1] + s2] + [s1[a:a + 1] + s2[0:hk] for a in range(1, hk)] + [s1[hk:k] + s2[0:1]],
            axis=0)
        cidx = jnp.concatenate(
            [i1[0:1] * PEER_NKEYS + i2] + [i1[a:a + 1] * PEER_NKEYS + i2[0:hk] for a in range(1, hk)]
            + [i1[hk:k] * PEER_NKEYS + i2[0:1]], axis=0)
        reset(cand, cidx)

    def finish():
        close("t2")
        top_s, eid = st["t2"]
        e = jnp.exp(top_s - top_s[0:1])
        r0 = h * k if static else pl.multiple_of(h * k, k)
        eid_out[pl.ds(r0, k), :] = eid
        gate_out[pl.ds(r0, k), :] = e / jnp.sum(e, axis=0, keepdims=True)

    q4 = [lambda: extract(k // 4)] * 4
    work = ([lambda: score(0)] + q4 + [lambda: (close("t0"), score(1))] + q4
            + [lambda: (close("t1"), candidates())] + q4 + [finish])
    while len(work) > n_pieces:
        f0, f1 = work[0], work[1]
        work[0:2] = [lambda f0=f0, f1=f1: (f0(), f1())]
    return work


def _peer_retrieve_kernel(q_ref, sk_ref, eidx_ref, gate_ref, eid_t, gate_t):
    for h in range(PEER_HEADS):
        for piece in _head_retrieval_pieces(q_ref, sk_ref, h, eid_t, gate_t, 1):
            piece()
    eidx_ref[...] = eid_t[...].T
    gate_ref[...] = gate_t[...].T


def peer_retrieve(qp, subkeys, tn=128):
    n, d = qp.shape
    tn = min(tn, n)
    assert n % tn == 0
    return pl.pallas_call(
        _peer_retrieve_kernel,
        grid=(n // tn,),
        in_specs=[
            pl.BlockSpec((tn, d), lambda i: (i, 0)),
            pl.BlockSpec(subkeys.shape, lambda i: (0, 0, 0, 0)),
        ],
        out_specs=[
            pl.BlockSpec((tn, PEER_PAIRS), lambda i: (i, 0)),
            pl.BlockSpec((tn, PEER_PAIRS), lambda i: (i, 0)),
        ],
        out_shape=[
            jax.ShapeDtypeStruct((n, PEER_PAIRS), jnp.int32),
            jax.ShapeDtypeStruct((n, PEER_PAIRS), jnp.float32),
        ],
        scratch_shapes=[pltpu.VMEM((PEER_PAIRS, tn), jnp.int32),
                        pltpu.VMEM((PEER_PAIRS, tn), jnp.float32)],
        compiler_params=_cparams(("arbitrary",)),
        name="peer_retrieve",
    )(qp, subkeys)


PEER_SLOTS = 8
PEER_AHEAD = PEER_SLOTS - 2
PEER_GROUP = 16
PEER_DMA_QUEUES = 2
PEER_LANES = 128
PEER_EROWS = 16
PEER_PITCH = 20


def _even_odd_chunks(a):
    nc = a.shape[1] // PEER_LANES
    pick = lambda c: a[:, c * PEER_LANES:(c + 1) * PEER_LANES]
    return jnp.concatenate([pick(c) for c in range(0, nc, 2)] + [pick(c) for c in range(1, nc, 2)],
                           axis=1)


def _interleave_chunks(a):
    nc = a.shape[1] // PEER_LANES
    half = nc // 2
    pick = lambda c: a[:, c * PEER_LANES:(c + 1) * PEER_LANES]
    return jnp.concatenate([pick(c // 2 + (half if c % 2 else 0)) for c in range(nc)], axis=1)


def _peer_main_kernel(eidx0_ref, gate0_ref, qn_ref, sk_ref, x_ref, nw_ref, fw_ref, uv_hbm, o_ref,
                      h_ref, y_ref, dup_ref, w2_ref, eidx_ref, gate_ref, eid_t, gate_t,
                      eid_next, gate_next, *scratch):
    bf16 = jnp.bfloat16
    f32 = jnp.float32
    bufs, sem, csem = scratch[:PEER_SLOTS], scratch[PEER_SLOTS], scratch[PEER_SLOTS + 1]
    tb, d = x_ref.shape
    half = d // 2
    n_rows = PEER_EROWS // 2
    lanes2 = 2 * PEER_PAIRS

    @pl.when(pl.program_id(0) == 0)
    def _():
        eid_next[...] = eidx0_ref[...]
        gate_next[...] = gate0_ref[...]

    to_smem = pltpu.make_async_copy(eid_next, eidx_ref, csem)
    to_smem.start()
    gate_ref[...] = gate_next[...]
    h_ref[...] = _even_odd_chunks(_rms(x_ref[...], nw_ref[...])).astype(bf16).astype(f32)
    dup_ref[...] = (lax.broadcasted_iota(jnp.int32, (PEER_PAIRS, lanes2), 1) // 2
                    == lax.broadcasted_iota(jnp.int32, (PEER_PAIRS, lanes2), 0)).astype(bf16)
    to_smem.wait()

    rows = PEER_PAIRS * PEER_EROWS

    def issue(t, slot, part=0, parts=1):
        n_p = PEER_PAIRS // parts
        for p in range(part * n_p, (part + 1) * n_p):
            r = pl.multiple_of(eidx_ref[t, p] * PEER_EROWS, PEER_EROWS)
            pltpu.make_async_copy(uv_hbm.at[pl.ds(r, PEER_EROWS)],
                                  bufs[slot].at[pl.ds(p * PEER_PITCH, PEER_EROWS)],
                                  sem.at[slot]).start(priority=p % PEER_DMA_QUEUES)

    def wait(slot):
        pltpu.make_async_copy(uv_hbm.at[pl.ds(0, rows)], bufs[slot].at[pl.ds(0, rows)],
                              sem.at[slot]).wait()

    def words(slot, m):
        return bufs[slot][pl.ds(m, PEER_PAIRS, stride=PEER_PITCH), :]

    sub_w = lax.broadcasted_iota(jnp.int32, (8, lanes2), 0)
    par_w = lax.broadcasted_iota(jnp.int32, (8, lanes2), 1) % 2

    def u_side(t, slot, between=None):
        hrow = h_ref[pl.ds(t, 1), :]
        acc = None
        for m in range(n_rows):
            if between is not None:
                between(m)
            wd = words(slot, m)
            lo = lax.bitcast_convert_type(wd << 16, f32)
            hi = lax.bitcast_convert_type(wd & jnp.int32(-65536), f32)
            term = (lo * hrow[:, m * PEER_LANES:(m + 1) * PEER_LANES]
                    + hi * hrow[:, half + m * PEER_LANES:half + (m + 1) * PEER_LANES])
            acc = term if acc is None else acc + term
        act = jnp.sum(acc.T, axis=0, keepdims=True)
        gelu = 0.5 * act * (1.0 + lax.erf(act * np.float32(math.sqrt(0.5))))
        w = gate_ref[pl.ds(t, 1), :] * gelu
        wb = jnp.broadcast_to(w, (8, PEER_PAIRS)).astype(bf16)
        wd2 = jnp.dot(wb, dup_ref[...], preferred_element_type=f32)
        w2_ref[slot] = jnp.where(par_w == sub_w, wd2, 0.0)

    def v_side(t, slot):
        wv = jnp.concatenate([pltpu.bitcast(words(slot, n_rows + m), bf16)
                              for m in range(n_rows)], axis=1)
        y2 = jnp.dot(w2_ref[slot].astype(bf16), wv, preferred_element_type=f32)
        y_ref[pl.ds(t, 1), :] = jnp.concatenate([y2[0:1], y2[1:2]], axis=1)

    def step(t, j, do_issue=True):
        wait(j % PEER_SLOTS)
        spread = (lambda m: issue(t + PEER_AHEAD, (j + PEER_AHEAD) % PEER_SLOTS, m, n_rows)
                  ) if do_issue else None
        v_side(t - 1, (j - 1) % PEER_SLOTS)
        u_side(t, j % PEER_SLOTS, spread)

    for t in range(PEER_AHEAD):
        issue(t, t)
    wait(0)
    u_side(0, 0)
    issue(PEER_AHEAD, PEER_AHEAD)

    def body(g, carry):
        t0 = 1 + g * PEER_GROUP
        pieces = _head_retrieval_pieces(qn_ref, sk_ref, g, eid_t, gate_t, PEER_GROUP)
        for j in range(PEER_GROUP):
            step(t0 + j, 1 + j)
            pieces[j]()
        return carry

    n_main = (tb - PEER_AHEAD - 1) // PEER_GROUP
    assert n_main == PEER_HEADS - 1
    lax.fori_loop(0, n_main, body, 0)
    tail = range(1 + n_main * PEER_GROUP, tb)
    pieces = _head_retrieval_pieces(qn_ref, sk_ref, PEER_HEADS - 1, eid_t, gate_t, len(tail))
    for t, piece in zip(tail, pieces):
        step(t, t, do_issue=t + PEER_AHEAD < tb)
        piece()
    v_side(tb - 1, (tb - 1) % PEER_SLOTS)
    eid_next[...] = eid_t[...].T
    gate_next[...] = gate_t[...].T


    o_ref[...] = _rms(x_ref[...] + _interleave_chunks(y_ref[...]), fw_ref[...])


def make_peer_table(u_tab, v_tab):
    e, d = u_tab.shape
    te = 256
    assert e % te == 0 and d == PEER_EROWS * PEER_LANES
    return pl.pallas_call(
        _pack_table_kernel,
        grid=(e // te,),
        in_specs=[pl.BlockSpec((te, d), lambda i: (i, 0)), pl.BlockSpec((te, d), lambda i: (i, 0))],
        out_specs=pl.BlockSpec((te * PEER_EROWS, PEER_LANES), lambda i: (i, 0)),
        out_shape=jax.ShapeDtypeStruct((e * PEER_EROWS, PEER_LANES), jnp.int32),
        compiler_params=_cparams(("arbitrary",)),
        name="peer_pack",
    )(u_tab, v_tab)


def _pack_table_kernel(u_ref, v_ref, o_ref):
    te = u_ref.shape[0]
    half = PEER_EROWS // 2

    def bf16_bits(x):
        return lax.bitcast_convert_type(x.astype(jnp.bfloat16).astype(jnp.float32), jnp.int32)

    for src, m0 in ((u_ref, 0), (v_ref, half)):
        for m in range(half):
            lo = bf16_bits(src[:, (2 * m) * PEER_LANES:(2 * m + 1) * PEER_LANES])
            hi = bf16_bits(src[:, (2 * m + 1) * PEER_LANES:(2 * m + 2) * PEER_LANES])
            word = lax.shift_right_logical(lo, jnp.int32(16)) | (hi & jnp.int32(-65536))
            o_ref[pl.ds(m0 + m, te, stride=PEER_EROWS), :] = word


def peer_main(qp, subkeys, x, nw, fw, uv_tab, tb=128):
    n, d = x.shape
    tb = min(tb, n)
    assert n % tb == 0 and tb > PEER_AHEAD + PEER_SLOTS + 1 and PEER_GROUP % PEER_SLOTS == 0
    assert uv_tab.shape[1] == PEER_LANES and d == PEER_EROWS * PEER_LANES
    nblk = n // tb
    eidx0, gate0 = peer_retrieve(qp[:tb], subkeys, tn=tb)
    first = lambda i: (0, 0)
    return pl.pallas_call(
        _peer_main_kernel,
        grid=(nblk,),
        in_specs=[
            pl.BlockSpec((tb, PEER_PAIRS), first),
            pl.BlockSpec((tb, PEER_PAIRS), first),
            pl.BlockSpec((tb, qp.shape[1]), lambda i: (jnp.minimum(i + 1, nblk - 1), 0)),
            pl.BlockSpec(subkeys.shape, lambda i: (0, 0, 0, 0)),
            pl.BlockSpec((tb, d), lambda i: (i, 0)),
            pl.BlockSpec((1, d), first),
            pl.BlockSpec((1, d), first),
            pl.BlockSpec(memory_space=pl.ANY),
        ],
        out_specs=pl.BlockSpec((tb, d), lambda i: (i, 0)),
        out_shape=jax.ShapeDtypeStruct((n, d), jnp.float32),
        scratch_shapes=[
            pltpu.VMEM((tb, d), jnp.float32),
            pltpu.VMEM((tb, d), jnp.float32),
            pltpu.VMEM((PEER_PAIRS, 2 * PEER_PAIRS), jnp.bfloat16),
            pltpu.VMEM((PEER_SLOTS, 8, 2 * PEER_PAIRS), jnp.float32),
            pltpu.SMEM((tb, PEER_PAIRS), jnp.int32),
            pltpu.VMEM((tb, PEER_PAIRS), jnp.float32),
            pltpu.VMEM((PEER_PAIRS, tb), jnp.int32),
            pltpu.VMEM((PEER_PAIRS, tb), jnp.float32),
            pltpu.VMEM((tb, PEER_PAIRS), jnp.int32),
            pltpu.VMEM((tb, PEER_PAIRS), jnp.float32),
            *[pltpu.VMEM((PEER_PAIRS * PEER_PITCH, PEER_LANES), jnp.int32)
              for _ in range(PEER_SLOTS)],
            pltpu.SemaphoreType.DMA((PEER_SLOTS,)),
            pltpu.SemaphoreType.DMA(()),
        ],
        compiler_params=_cparams(("arbitrary",)),
        name="peer_main",
    )(eidx0, gate0, qp, subkeys, x, nw.reshape(1, d), fw.reshape(1, d), uv_tab)


def _trunk(x, mk_arr, mv_arr, mk_col, mv_col, s0, cache_k, cache_v, p):
    b, t, d = x.shape
    n = b * t
    proj = norm_matmul(x.reshape(n, d), p["norm_mix_w"], p["w_in"], tn=IN_WIDTH // 2)
    proj3 = proj.reshape(b, t, IN_WIDTH)
    o_hg, s_fin = hgrn(proj3, s0, p["lb"], p["hg_norm_w"])
    o_sw = swa(proj3, p["rel_bias"], p["swa_sinks"], cache_k, cache_v)
    x1 = outproj(o_hg.reshape(n, HG_WIDTH), o_sw.reshape(n, SWA_WIDTH), p["w_out"], x.reshape(n, d))
    x2 = memattn(x1.reshape(b, t, d), p["norm_cross_w"], p["mem_wq"], mk_arr, mv_arr,
                 mk_col, mv_col, p["mem_wo"]).reshape(n, d)
    qp = norm_matmul(x2, p["norm_ffn_w"], p["peer_wq"], tn=2048)
    y = peer_main(qp, p["peer_subkeys"], x2, p["norm_ffn_w"], p["final_norm_w"], p["peer_uv"])
    keep = min(WINDOW, t) if cache_k is None else t
    k_rows = proj3[:, t - keep:, OFF_KSW:OFF_KSW + SWA_KV_WIDTH]
    v_rows = proj3[:, t - keep:, OFF_VSW:OFF_VSW + SWA_KV_WIDTH]
    k_rows = k_rows.reshape(b, keep, SWA_KV_HEADS, SWA_HEAD_DIM)
    v_rows = v_rows.reshape(b, keep, SWA_KV_HEADS, SWA_HEAD_DIM)
    return y.reshape(b, t, d), s_fin, k_rows, v_rows


def kernel(x_prompt, x_sample, mem_prompt, state_hgrn, cache_swa_k, cache_swa_v, cache_mem_k, cache_mem_v, rel_bias, hg_lb_logits, norm_mix_w, w_in, hg_norm_w, swa_sinks, w_out, norm_mem_w, norm_cross_w, mem_wq, mem_wk, mem_wv, mem_wo, norm_ffn_w, peer_wq, peer_subkeys, peer_u, peer_v, final_norm_w):
    bf16 = jnp.bfloat16
    depth = w_in.shape[0]
    assert depth == 1
    l = 0
    lb_all = jnp.cumsum(jax.nn.softmax(hg_lb_logits.astype(jnp.float32), axis=0), axis=0)
    p = {
        "norm_mix_w": norm_mix_w[l], "w_in": w_in[l].astype(bf16), "lb": lb_all[l],
        "hg_norm_w": hg_norm_w[l], "rel_bias": rel_bias, "swa_sinks": swa_sinks[l],
        "w_out": w_out[l].astype(bf16), "norm_cross_w": norm_cross_w[l],
        "mem_wq": mem_wq[l].astype(bf16), "mem_wo": mem_wo[l].astype(bf16),
        "norm_ffn_w": norm_ffn_w[l], "peer_wq": peer_wq[l].astype(bf16),
        "peer_subkeys": peer_subkeys[l].astype(bf16), "peer_uv": make_peer_table(peer_u[l], peer_v[l]),
        "final_norm_w": final_norm_w,
    }
    bp, tp, d = x_prompt.shape
    bs, ts, _ = x_sample.shape

    wkv = jnp.concatenate([mem_wk[l], mem_wv[l]], axis=1).astype(bf16)
    kv = norm_matmul(mem_prompt.reshape(bp * N_MEM, d), norm_mem_w[l], wkv)
    kv3 = kv.reshape(bp, N_MEM, 2 * MEM_INNER)
    mk = kv3[:, :, :MEM_INNER].reshape(bp, N_MEM, MEM_HEADS, MEM_HEAD_DIM)
    mv = kv3[:, :, MEM_INNER:].reshape(bp, N_MEM, MEM_HEADS, MEM_HEAD_DIM)

    s0 = jnp.zeros((bp, HG_HEADS, HG_DK, HG_DV), jnp.float32)
    yp, sp, kp, vp = _trunk(x_prompt, kv3, kv3, 0, 1, s0, None, None, p)

    cmk = cache_mem_k[l].reshape(bs, N_MEM, MEM_INNER)
    cmv = cache_mem_v[l].reshape(bs, N_MEM, MEM_INNER)
    ck = cache_swa_k[l].reshape(bs, -1, SWA_KV_WIDTH)
    cv = cache_swa_v[l].reshape(bs, -1, SWA_KV_WIDTH)
    ys, ss, ks_new, vs_new = _trunk(x_sample, cmk, cmv, 0, 0, state_hgrn[l], ck, cv, p)

    return (yp, ys, sp[None], kp[None], vp[None], mk[None], mv[None],
            ss[None], ks_new[None], vs_new[None])
```
